```python
import jax, jax.numpy as jnp
from jax import lax
import numpy as np

D_MODEL = 2048
BATCH = 16
SEQ = 256
DEPTH = 1
DEC_BATCH = 4
DEC_SEQ = 2048
PAST_LEN = 512

GRID_W = 64
D_MIX = D_MODEL
D_RG = D_MIX // 2
D_CH = D_MIX - D_RG
N_RG_HEADS = 8
RG_HEAD = D_RG // N_RG_HEADS
N_CH_HEADS = 8
CH_HEAD = D_CH // N_CH_HEADS
CHUNK = 128
ROWS_PER_CHUNK = CHUNK // GRID_W
CONV_W = 4
CONV_LEFT = 2
RG_C = 8.0
D_FF = 4 * D_MODEL
N_MOD = 6
D_IN = 2 * D_RG + 2 * D_CH
EPS = 1e-6

kernel_name = "hybrid_rglru_chunkmlp_diffusion_step"


def _rmsnorm(x, g):
    xf = x.astype(jnp.float32)
    y = xf * lax.rsqrt(jnp.mean(xf * xf, axis=-1, keepdims=True) + EPS)
    return (y * g.astype(jnp.float32)).astype(x.dtype)


def _modulation(cond, w_ada, b_ada):
    m = jax.nn.silu(cond) @ w_ada + b_ada
    return jnp.split(m[:, None, :], N_MOD, axis=-1)


def _centred_dwconv(x, w, b):
    L = x.shape[1]
    xp = jnp.pad(x, ((0, 0), (CONV_LEFT, CONV_W - 1 - CONV_LEFT), (0, 0)))
    out = b
    for k in range(CONV_W):
        out = out + w[k] * xp[:, k:k + L]
    return out


def _block_diag(x, w, b):
    B, L, _ = x.shape
    xh = x.reshape(B, L, N_RG_HEADS, RG_HEAD)
    return jnp.einsum('blhi,hij->blhj', xh, w.astype(jnp.float32)).reshape(B, L, D_RG) + b.astype(jnp.float32)


def _linear_scan(a, bx, h0, reverse):
    def step(h, inp):
        a_t, b_t = inp
        h = a_t * h + b_t
        return h, h
    h_last, hs = lax.scan(step, h0, (jnp.swapaxes(a, 0, 1), jnp.swapaxes(bx, 0, 1)), reverse=reverse)
    return jnp.swapaxes(hs, 0, 1), h_last


def _rglru_dir(xf, h0, wa, ba, wi, bi, lam, reverse):
    r = jax.nn.sigmoid(_block_diag(xf, wa, ba))
    i = jax.nn.sigmoid(_block_diag(xf, wi, bi))
    log_a = -RG_C * r * jax.nn.softplus(-lam.astype(jnp.float32))
    a = jnp.exp(log_a)
    mult = jnp.sqrt(jnp.maximum(-jnp.expm1(2.0 * log_a), 0.0))
    return _linear_scan(a, mult * (i * xf), h0, reverse)


def _chunk_sgu(u, v, w_s, b_s, n_chunks):
    B, L, _ = v.shape
    vh = v.reshape(B, n_chunks, CHUNK, N_CH_HEADS, CH_HEAD)
    mixed = jnp.einsum('hpq,bnqhd->bnphd', w_s, vh) + jnp.swapaxes(b_s, 0, 1)[None, None, :, :, None]
    return u * mixed.reshape(B, L, D_CH)


def _layer(x, cond, h0, n_chunks, p):
    shift1, scale1, gate1, shift2, scale2, gate2 = _modulation(cond, p['w_ada'], p['b_ada'])
    h = _rmsnorm(x, p['norm1_g']) * (1.0 + scale1) + shift1
    proj = h @ p['w_in']
    y_rg = proj[..., :D_RG]
    x_rg = proj[..., D_RG:2 * D_RG]
    uv = jax.nn.gelu(proj[..., 2 * D_RG:])
    xc = _centred_dwconv(x_rg, p['conv_w'], p['conv_b']).astype(jnp.float32)
    hf, hf_last = _rglru_dir(xc, h0[:, 0], p['ga_w'][0], p['ga_b'][0], p['gi_w'][0], p['gi_b'][0], p['lam'][0], False)
    hb, hb_last = _rglru_dir(xc, h0[:, 1], p['ga_w'][1], p['ga_b'][1], p['gi_w'][1], p['gi_b'][1], p['lam'][1], True)
    rg_out = jax.nn.gelu(y_rg) * (hf + hb).astype(x.dtype)
    ch_out = _chunk_sgu(uv[..., :D_CH], uv[..., D_CH:], p['sgu_w'], p['sgu_b'], n_chunks)
    mix = jnp.concatenate([rg_out, ch_out], axis=-1) @ p['w_out']
    x = x + gate1 * mix
    h2 = _rmsnorm(x, p['norm2_g']) * (1.0 + scale2) + shift2
    ff = jnp.square(jax.nn.relu(h2 @ p['w_ff1'])) @ p['w_ff2']
    x = x + gate2 * ff
    return x, jnp.stack([hf_last, hb_last], axis=1)


def setup_inputs(seed: int = 0) -> dict:
    key = jax.random.key(seed)
    ks = jax.random.split(key, 24)
    f32 = jnp.float32
    nrm = lambda k, shape, s: jax.random.normal(k, shape, f32) * s
    u = jax.random.uniform(ks[14], (DEPTH, 2, D_RG), f32, 0.9, 0.999)
    s = u ** (1.0 / RG_C)
    lam = jnp.log(s) - jnp.log1p(-s)
    return {
        'x_prompt': nrm(ks[0], (BATCH, SEQ, D_MODEL), 1.0),
        'x_sample': nrm(ks[1], (DEC_BATCH, DEC_SEQ, D_MODEL), 1.0),
        'c': nrm(ks[2], (DEC_BATCH, D_MODEL), 1.0),
        'state_rglru': nrm(ks[3], (DEC_BATCH, DEPTH, 2, D_RG), 0.5),
        'c_ctx': nrm(ks[4], (D_MODEL,), 1.0),
        'norm1_g': 1.0 + nrm(ks[5], (DEPTH, D_MODEL), 0.02),
        'w_ada': nrm(ks[6], (DEPTH, D_MODEL, N_MOD * D_MODEL), 0.5 * D_MODEL ** -0.5),
        'b_ada': nrm(ks[7], (DEPTH, N_MOD * D_MODEL), 0.02),
        'w_in': nrm(ks[8], (DEPTH, D_MODEL, D_IN), D_MODEL ** -0.5),
        'conv_w': nrm(ks[9], (DEPTH, CONV_W, D_RG), CONV_W ** -0.5),
        'conv_b': nrm(ks[10], (DEPTH, D_RG), 0.02),
        'ga_w': nrm(ks[11], (DEPTH, 2, N_RG_HEADS, RG_HEAD, RG_HEAD), RG_HEAD ** -0.5),
        'ga_b': nrm(ks[12], (DEPTH, 2, D_RG), 0.02),
        'gi_w': nrm(ks[13], (DEPTH, 2, N_RG_HEADS, RG_HEAD, RG_HEAD), RG_HEAD ** -0.5),
        'gi_b': nrm(ks[15], (DEPTH, 2, D_RG), 0.02),
        'lru_lambda': lam,
        'sgu_w': nrm(ks[16], (DEPTH, N_CH_HEADS, CHUNK, CHUNK), CHUNK ** -0.5),
        'sgu_b': nrm(ks[17], (DEPTH, N_CH_HEADS, CHUNK), 0.02),
        'w_out': nrm(ks[18], (DEPTH, D_MIX, D_MODEL), D_MIX ** -0.5),
        'norm2_g': 1.0 + nrm(ks[19], (DEPTH, D_MODEL), 0.02),
        'w_ff1': nrm(ks[20], (DEPTH, D_MODEL, D_FF), D_MODEL ** -0.5),
        'w_ff2': nrm(ks[21], (DEPTH, D_FF, D_MODEL), D_FF ** -0.5),
        'final_g': 1.0 + nrm(ks[22], (D_MODEL,), 0.02),
    }


def reference(x_prompt, x_sample, c, state_rglru, c_ctx, norm1_g, w_ada, b_ada, w_in,
              conv_w, conv_b, ga_w, ga_b, gi_w, gi_b, lru_lambda, sgu_w, sgu_b, w_out,
              norm2_g, w_ff1, w_ff2, final_g):
    b_ctx, l_ctx = x_prompt.shape[0], x_prompt.shape[1]
    ctx_chunks = l_ctx // CHUNK
    rows = x_sample.shape[1] // GRID_W
    lat_chunks = rows // ROWS_PER_CHUNK
    cond_ctx = jnp.broadcast_to(c_ctx, (b_ctx, D_MODEL))
    xp = x_prompt
    xs = x_sample
    ctx_states = []
    for l in range(DEPTH):
        p = {'w_ada': w_ada[l], 'b_ada': b_ada[l], 'norm1_g': norm1_g[l], 'w_in': w_in[l],
             'conv_w': conv_w[l], 'conv_b': conv_b[l], 'ga_w': ga_w[l], 'ga_b': ga_b[l],
             'gi_w': gi_w[l], 'gi_b': gi_b[l], 'lam': lru_lambda[l], 'sgu_w': sgu_w[l],
             'sgu_b': sgu_b[l], 'w_out': w_out[l], 'norm2_g': norm2_g[l],
             'w_ff1': w_ff1[l], 'w_ff2': w_ff2[l]}
        h0_ctx = jnp.zeros((b_ctx, 2, D_RG), jnp.float32)
        xp, st = _layer(xp, cond_ctx, h0_ctx, ctx_chunks, p)
        ctx_states.append(st)
        xs, _ = _layer(xs, c, state_rglru[:, l].astype(jnp.float32), lat_chunks, p)
    y_prompt = _rmsnorm(xp, final_g)
    y_sample = _rmsnorm(xs, final_g)
    new_state_rglru = jnp.stack(ctx_states, axis=1).astype(x_prompt.dtype)
    return (y_prompt, y_sample, new_state_rglru)
```

```python
import functools

import jax
import jax.numpy as jnp
from jax import lax
from jax.experimental import pallas as pl
from jax.experimental.pallas import tpu as pltpu

F32 = jnp.float32
BF16 = jnp.bfloat16

N_RG_HEADS = 8
N_CH_HEADS = 8
CHUNK = 128
CONV_W = 4
CONV_LEFT = 2
RG_C = 8.0
N_MOD = 6
EPS = 1e-6

SUBLANES = 8
LANES = 128
MOD_ROWS = 8
VMEM_LIMIT_BYTES = 56 * 1024 * 1024

ROW_TILE = 512
FF_TILE = 512
SEQ_TILE = 256
MOD_COL_TILE = 1024


def _rms(x):
    return x * lax.rsqrt(jnp.mean(x * x, axis=-1, keepdims=True) + EPS)


def _mod_kernel(c_ref, w_ref, b_ref, o_ref):
    s = jax.nn.silu(c_ref[...]).astype(BF16)
    o_ref[...] = jnp.dot(s, w_ref[...].astype(BF16), preferred_element_type=F32) + b_ref[...]


def _modulation(cond, w_ada, b_ada):
    d = cond.shape[1]
    n = w_ada.shape[1]
    out = pl.pallas_call(
        _mod_kernel,
        grid=(n // MOD_COL_TILE,),
        in_specs=[
            pl.BlockSpec((MOD_ROWS, d), lambda j: (0, 0)),
            pl.BlockSpec((d, MOD_COL_TILE), lambda j: (0, j)),
            pl.BlockSpec((1, MOD_COL_TILE), lambda j: (0, j)),
        ],
        out_specs=pl.BlockSpec((MOD_ROWS, MOD_COL_TILE), lambda j: (0, j)),
        out_shape=jax.ShapeDtypeStruct((MOD_ROWS, n), F32),
        compiler_params=pltpu.CompilerParams(
            dimension_semantics=("arbitrary",), vmem_limit_bytes=VMEM_LIMIT_BYTES),
        name="modulation",
    )(cond, w_ada, b_ada.reshape(1, n))
    return out.reshape(MOD_ROWS, N_MOD, 1, d)


def _mod_spec(which, mod_base, rows_per_mod, d):
    def index(i, *_):
        return (mod_base + (i * ROW_TILE) // rows_per_mod, which, 0, 0)
    return pl.BlockSpec((None, None, 1, d), index)


def _inproj_kernel(x_ref, g_ref, sc_ref, sh_ref, w_ref, gy_ref, xrg_ref, u_ref, v_ref):
    d_rg = gy_ref.shape[1]
    h = (_rms(x_ref[...]) * g_ref[...]) * (1.0 + sc_ref[...]) + sh_ref[...]
    hb = h.astype(BF16)

    def proj(n):
        return jnp.dot(hb, w_ref[:, n * d_rg:(n + 1) * d_rg], preferred_element_type=F32)

    gy_ref[...] = jax.nn.gelu(proj(0))
    xrg_ref[...] = proj(1)
    u_ref[...] = jax.nn.gelu(proj(2))
    v_ref[...] = jax.nn.gelu(proj(3)).astype(BF16)


def _inproj(x, mod, mod_base, rows_per_mod, norm_g, w_in):
    m, d = x.shape
    d_in = w_in.shape[1]
    d_rg = d_in // 4
    row = lambda i: (i, 0)
    return pl.pallas_call(
        _inproj_kernel,
        grid=(m // ROW_TILE,),
        in_specs=[
            pl.BlockSpec((ROW_TILE, d), row),
            pl.BlockSpec((1, d), lambda i: (0, 0)),
            _mod_spec(1, mod_base, rows_per_mod, d),
            _mod_spec(0, mod_base, rows_per_mod, d),
            pl.BlockSpec((d, d_in), lambda i: (0, 0), pipeline_mode=pl.Buffered(1)),
        ],
        out_specs=[pl.BlockSpec((ROW_TILE, d_rg), row)] * 4,
        out_shape=[
            jax.ShapeDtypeStruct((m, d_rg), F32),
            jax.ShapeDtypeStruct((m, d_rg), F32),
            jax.ShapeDtypeStruct((m, d_rg), F32),
            jax.ShapeDtypeStruct((m, d_rg), BF16),
        ],
        compiler_params=pltpu.CompilerParams(
            dimension_semantics=("arbitrary",), vmem_limit_bytes=VMEM_LIMIT_BYTES),
        name="in_projection",
    )(x, norm_g.reshape(1, d), mod, mod, w_in)


def _local_scan(a, b, reverse):
    row = lax.broadcasted_iota(jnp.int32, a.shape, 1)
    for dist in (1, 2, 4):
        if reverse:
            shift, valid = SUBLANES - dist, row < SUBLANES - dist
        else:
            shift, valid = dist, row >= dist
        a_sh = pltpu.roll(a, shift, axis=1)
        b_sh = pltpu.roll(b, shift, axis=1)
        b = jnp.where(valid, b + a * b_sh, b)
        a = jnp.where(valid, a * a_sh, a)
    return a, b


def _seqmix_kernel(x_ref, xprev_ref, xnext_ref, gy_ref, u_ref, v_ref, h0_ref,
                   cw_ref, cb_ref, gw_ref, gb_ref, lam_ref, sw_ref, sb_ref,
                   mix_ref, st_ref, hf_scr, carry_scr, xpad_scr):
    tc, d_rg = x_ref.shape
    groups = tc // SUBLANES
    head = d_rg // N_RG_HEADS
    ch_head = d_rg // N_CH_HEADS
    p = pl.program_id(1)
    j = pl.program_id(2)
    n_chunks = pl.num_programs(2)
    jj = jnp.where(p == 0, j, n_chunks - 1 - j)

    halo = xprev_ref.shape[0]
    xpad_scr[0:halo, :] = jnp.where(jj == 0, 0.0, xprev_ref[...])
    xpad_scr[halo:halo + tc, :] = x_ref[...]
    xpad_scr[halo + tc:2 * halo + tc, :] = jnp.where(jj == n_chunks - 1, 0.0, xnext_ref[...])

    @pl.when(j == 0)
    def _():
        carry_scr[...] = h0_ref[pl.ds(p, 1), :]

    def direction(reverse):
        dirn = 1 if reverse else 0
        row0 = pl.multiple_of(jj * tc, tc)
        for hd in range(N_RG_HEADS):
            cols = slice(hd * head, (hd + 1) * head)
            xc = cb_ref[:, cols]
            for k in range(CONV_W):
                off = halo + k - CONV_LEFT
                xc = xc + cw_ref[k:k + 1, cols] * xpad_scr[off:off + tc, cols]
            g = jnp.dot(xc.astype(BF16), gw_ref[dirn, hd], preferred_element_type=F32)
            r = jax.nn.sigmoid(g[:, :head] + gb_ref[dirn, 0:1, cols])
            i = jax.nn.sigmoid(g[:, head:] + gb_ref[dirn, 1:2, cols])
            log_a = (-RG_C * r) * jax.nn.softplus(-lam_ref[dirn:dirn + 1, cols])
            a = jnp.exp(log_a)
            mult = jnp.sqrt(jnp.maximum(-jnp.tanh(log_a) * (a * a + 1.0), 0.0))
            bx = mult * (i * xc)
            a_loc, b_loc = _local_scan(a.reshape(groups, SUBLANES, head),
                                       bx.reshape(groups, SUBLANES, head), reverse)
            c = carry_scr[:, cols]
            hs = [None] * groups
            order = range(groups - 1, -1, -1) if reverse else range(groups)
            edge = 0 if reverse else SUBLANES - 1
            for k in order:
                hk = b_loc[k] + a_loc[k] * c
                hs[k] = hk
                c = hk[edge:edge + 1, :]
            carry_scr[:, cols] = c
            hcat = jnp.concatenate(hs, axis=0)
            if not reverse:
                hf_scr[pl.ds(row0, tc), cols] = hcat
            else:
                hsum = hf_scr[pl.ds(row0, tc), cols] + hcat
                mix_ref[:, cols] = (gy_ref[:, cols] * hsum).astype(BF16)
        if reverse:
            for hd in range(N_CH_HEADS):
                cols = slice(hd * ch_head, (hd + 1) * ch_head)
                for n in range(tc // CHUNK):
                    rows = slice(n * CHUNK, (n + 1) * CHUNK)
                    mixed = jnp.dot(sw_ref[hd], v_ref[rows, cols],
                                    preferred_element_type=F32) + sb_ref[hd]
                    mix_ref[rows, d_rg + hd * ch_head:d_rg + (hd + 1) * ch_head] = (
                        u_ref[rows, cols] * mixed).astype(BF16)

        @pl.when(j == n_chunks - 1)
        def _():
            st_ref[dirn:dirn + 1, :] = carry_scr[...]

    @pl.when(p == 0)
    def _():
        direction(False)

    @pl.when(p == 1)
    def _():
        direction(True)


def _seqmix(xrg, gy, u, v, h0, seq_len, conv_w, conv_b, gw, gb, lam, sgu_w, sgu_b):
    m, d_rg = xrg.shape
    n_seq = m // seq_len
    tc = min(SEQ_TILE, seq_len)
    n_chunks = seq_len // tc
    halo_per_tile = tc // SUBLANES
    n_halo = m // SUBLANES

    def chunk(p, j):
        return jnp.where(p == 0, j, n_chunks - 1 - j)

    def cur(s, p, j):
        return (s * n_chunks + chunk(p, j), 0)

    def prev(s, p, j):
        return (jnp.maximum((s * n_chunks + chunk(p, j)) * halo_per_tile - 1, 0), 0)

    def nxt(s, p, j):
        return (jnp.minimum((s * n_chunks + chunk(p, j) + 1) * halo_per_tile, n_halo - 1), 0)

    def out_chunk(s, p, j):
        return (s * n_chunks + jnp.where(p == 0, n_chunks - 1, n_chunks - 1 - j), 0)

    full = lambda *shape: pl.BlockSpec(shape, lambda s, p, j: (0,) * len(shape))
    per_seq = pl.BlockSpec((None, 2, d_rg), lambda s, p, j: (s, 0, 0))
    return pl.pallas_call(
        _seqmix_kernel,
        grid=(n_seq, 2, n_chunks),
        in_specs=[
            pl.BlockSpec((tc, d_rg), cur),
            pl.BlockSpec((SUBLANES, d_rg), prev),
            pl.BlockSpec((SUBLANES, d_rg), nxt),
            pl.BlockSpec((tc, d_rg), out_chunk),
            pl.BlockSpec((tc, d_rg), out_chunk),
            pl.BlockSpec((tc, d_rg), out_chunk),
            per_seq,
            full(CONV_W, d_rg),
            full(1, d_rg),
            full(*gw.shape),
            full(*gb.shape),
            full(2, d_rg),
            full(*sgu_w.shape),
            full(*sgu_b.shape),
        ],
        out_specs=[pl.BlockSpec((tc, 2 * d_rg), out_chunk), per_seq],
        out_shape=[
            jax.ShapeDtypeStruct((m, 2 * d_rg), BF16),
            jax.ShapeDtypeStruct((n_seq, 2, d_rg), F32),
        ],
        scratch_shapes=[
            pltpu.VMEM((seq_len, d_rg), F32),
            pltpu.VMEM((1, d_rg), F32),
            pltpu.VMEM((tc + 2 * SUBLANES, d_rg), F32),
        ],
        compiler_params=pltpu.CompilerParams(
            dimension_semantics=("arbitrary", "arbitrary", "arbitrary"),
            vmem_limit_bytes=VMEM_LIMIT_BYTES),
        name="sequence_mix",
    )(xrg, xrg, xrg, gy, u, v, h0, conv_w, conv_b, gw, gb, lam, sgu_w, sgu_b)


def _ffn_kernel(x_ref, mix_ref, g1_ref, sc_ref, sh_ref, g2_ref, ng_ref, fg_ref,
                wo_ref, w1_ref, w2_ref, o_ref, x1_scr, h2_scr, acc_scr, *, final_norm):
    j = pl.program_id(1)

    @pl.when(j == 0)
    def _():
        mix = jnp.dot(mix_ref[...], wo_ref[...], preferred_element_type=F32)
        x1 = x_ref[...] + g1_ref[...] * mix
        x1_scr[...] = x1
        h2 = (_rms(x1) * ng_ref[...]) * (1.0 + sc_ref[...]) + sh_ref[...]
        h2_scr[...] = h2.astype(BF16)
        acc_scr[...] = jnp.zeros_like(acc_scr)

    a = jnp.dot(h2_scr[...], w1_ref[...], preferred_element_type=F32)
    a = jnp.square(jnp.maximum(a, 0.0)).astype(BF16)
    acc_scr[...] += jnp.dot(a, w2_ref[...], preferred_element_type=F32)

    @pl.when(j == pl.num_programs(1) - 1)
    def _():
        x2 = x1_scr[...] + g2_ref[...] * acc_scr[...]
        if final_norm:
            x2 = _rms(x2) * fg_ref[...]
        o_ref[...] = x2


def _ffn(x, mix, mod, mod_base, rows_per_mod, norm_g, final_g, w_out, w_ff1, w_ff2, final_norm):
    m, d = x.shape
    d_mix = mix.shape[1]
    d_ff = w_ff1.shape[1]
    row = lambda i, j: (i, 0)
    vec = pl.BlockSpec((1, d), lambda i, j: (0, 0))
    return pl.pallas_call(
        functools.partial(_ffn_kernel, final_norm=final_norm),
        grid=(m // ROW_TILE, d_ff // FF_TILE),
        in_specs=[
            pl.BlockSpec((ROW_TILE, d), row),
            pl.BlockSpec((ROW_TILE, d_mix), row),
            _mod_spec(2, mod_base, rows_per_mod, d),
            _mod_spec(4, mod_base, rows_per_mod, d),
            _mod_spec(3, mod_base, rows_per_mod, d),
            _mod_spec(5, mod_base, rows_per_mod, d),
            vec,
            vec,
            pl.BlockSpec((d_mix, d), lambda i, j: (0, 0), pipeline_mode=pl.Buffered(1)),
            pl.BlockSpec((d, FF_TILE), lambda i, j: (0, j)),
            pl.BlockSpec((FF_TILE, d), lambda i, j: (j, 0)),
        ],
        out_specs=pl.BlockSpec((ROW_TILE, d), row),
        out_shape=jax.ShapeDtypeStruct((m, d), F32),
        scratch_shapes=[
            pltpu.VMEM((ROW_TILE, d), F32),
            pltpu.VMEM((ROW_TILE, d), BF16),
            pltpu.VMEM((ROW_TILE, d), F32),
        ],
        compiler_params=pltpu.CompilerParams(
            dimension_semantics=("arbitrary", "arbitrary"), vmem_limit_bytes=VMEM_LIMIT_BYTES),
        name="out_projection_ffn",
    )(x, mix, mod, mod, mod, mod, norm_g.reshape(1, d), final_g.reshape(1, d), w_out, w_ff1, w_ff2)


def _layer(x, mod, mod_base, rows_per_mod, seq_len, h0, p, final_g, final_norm):
    gy, xrg, u, v = _inproj(x, mod, mod_base, rows_per_mod, p["norm1_g"], p["w_in"])
    mix, st = _seqmix(xrg, gy, u, v, h0, seq_len, p["conv_w"], p["conv_b"], p["gw"], p["gb"],
                      p["lam"], p["sgu_w"], p["sgu_b"])
    y = _ffn(x, mix, mod, mod_base, rows_per_mod, p["norm2_g"], final_g,
             p["w_out"], p["w_ff1"], p["w_ff2"], final_norm)
    return y, st


def kernel(x_prompt, x_sample, c, state_rglru, c_ctx, norm1_g, w_ada, b_ada, w_in, conv_w, conv_b,
           ga_w, ga_b, gi_w, gi_b, lru_lambda, sgu_w, sgu_b, w_out, norm2_g, w_ff1, w_ff2, final_g):
    b_ctx, l_ctx, d = x_prompt.shape
    b_lat, l_lat, _ = x_sample.shape
    depth = w_in.shape[0]
    d_rg = lru_lambda.shape[-1]
    assert b_lat + 1 <= MOD_ROWS
    assert (b_ctx * l_ctx) % ROW_TILE == 0 and l_lat % ROW_TILE == 0

    cond = jnp.concatenate(
        [c, c_ctx[None, :], jnp.zeros((MOD_ROWS - b_lat - 1, d), F32)], axis=0)
    xp = x_prompt.reshape(b_ctx * l_ctx, d)
    xs = x_sample.reshape(b_lat * l_lat, d)
    h0_ctx = jnp.zeros((b_ctx, 2, d_rg), F32)
    ctx_states = []
    for l in range(depth):
        p = {
            "norm1_g": norm1_g[l], "w_in": w_in[l].astype(BF16),
            "conv_w": conv_w[l], "conv_b": conv_b[l].reshape(1, d_rg),
            "gw": jnp.concatenate([ga_w[l], gi_w[l]], axis=-1).astype(BF16),
            "gb": jnp.stack([ga_b[l], gi_b[l]], axis=1),
            "lam": lru_lambda[l],
            "sgu_w": sgu_w[l].astype(BF16), "sgu_b": sgu_b[l][:, :, None],
            "w_out": w_out[l].astype(BF16), "norm2_g": norm2_g[l],
            "w_ff1": w_ff1[l].astype(BF16), "w_ff2": w_ff2[l].astype(BF16),
        }
        mod = _modulation(cond, w_ada[l], b_ada[l])
        last = l == depth - 1
        xp, st = _layer(xp, mod, b_lat, b_ctx * l_ctx, l_ctx, h0_ctx, p, final_g, last)
        ctx_states.append(st)
        xs, _ = _layer(xs, mod, 0, l_lat, l_lat, state_rglru[:, l].astype(F32), p, final_g, last)
    y_prompt = xp.reshape(b_ctx, l_ctx, d)
    y_sample = xs.reshape(b_lat, l_lat, d)
    new_state = jnp.stack(ctx_states, axis=1).astype(x_prompt.dtype)
    return (y_prompt, y_sample, new_state)
```

```python
import functools

import jax
import jax.numpy as jnp
from jax import lax
from jax.experimental import pallas as pl
from jax.experimental.pallas import tpu as pltpu

F32 = jnp.float32
BF16 = jnp.bfloat16

N_RG_HEADS = 8
N_CH_HEADS = 8
CHUNK = 128
CONV_W = 4
CONV_LEFT = 2
RG_C = 8.0
N_MOD = 6
EPS = 1e-6

SUBLANES = 8
LANES = 128
MOD_ROWS = 8
VMEM_LIMIT_BYTES = 56 * 1024 * 1024

ROW_TILE = 512
FF_TILE = 512
SEQ_TILE = 256
MOD_COL_TILE = 1024
N_SEG = SUBLANES


def _rms(x):
    return x * lax.rsqrt(jnp.mean(x * x, axis=-1, keepdims=True) + EPS)


def _mod_kernel(c_ref, w_ref, b_ref, o_ref):
    s = jax.nn.silu(c_ref[...]).astype(BF16)
    o_ref[...] = jnp.dot(s, w_ref[...].astype(BF16), preferred_element_type=F32) + b_ref[...]


def _modulation(cond, w_ada, b_ada):
    d = cond.shape[1]
    n = w_ada.shape[1]
    out = pl.pallas_call(
        _mod_kernel,
        grid=(n // MOD_COL_TILE,),
        in_specs=[
            pl.BlockSpec((MOD_ROWS, d), lambda j: (0, 0)),
            pl.BlockSpec((d, MOD_COL_TILE), lambda j: (0, j)),
            pl.BlockSpec((1, MOD_COL_TILE), lambda j: (0, j)),
        ],
        out_specs=pl.BlockSpec((MOD_ROWS, MOD_COL_TILE), lambda j: (0, j)),
        out_shape=jax.ShapeDtypeStruct((MOD_ROWS, n), F32),
        compiler_params=pltpu.CompilerParams(
            dimension_semantics=("arbitrary",), vmem_limit_bytes=VMEM_LIMIT_BYTES),
        name="modulation",
    )(cond, w_ada, b_ada.reshape(1, n))
    return out.reshape(MOD_ROWS, N_MOD, 1, d)


def _mod_spec(which, mod_base, rows_per_mod, d):
    def index(i, *_):
        return (mod_base + (i * ROW_TILE) // rows_per_mod, which, 0, 0)
    return pl.BlockSpec((None, None, 1, d), index)


def _inproj_kernel(x_ref, g_ref, sc_ref, sh_ref, w_ref, gy_ref, xrg_ref, u_ref, v_ref):
    d_rg = gy_ref.shape[1]
    h = (_rms(x_ref[...]) * g_ref[...]) * (1.0 + sc_ref[...]) + sh_ref[...]
    hb = h.astype(BF16)

    def proj(n):
        return jnp.dot(hb, w_ref[:, n * d_rg:(n + 1) * d_rg], preferred_element_type=F32)

    gy_ref[...] = jax.nn.gelu(proj(0))
    xrg_ref[...] = proj(1)
    u_ref[...] = jax.nn.gelu(proj(2))
    v_ref[...] = jax.nn.gelu(proj(3)).astype(BF16)


def _inproj(x, mod, mod_base, rows_per_mod, norm_g, w_in):
    m, d = x.shape
    d_in = w_in.shape[1]
    d_rg = d_in // 4
    row = lambda i: (i, 0)
    return pl.pallas_call(
        _inproj_kernel,
        grid=(m // ROW_TILE,),
        in_specs=[
            pl.BlockSpec((ROW_TILE, d), row),
            pl.BlockSpec((1, d), lambda i: (0, 0)),
            _mod_spec(1, mod_base, rows_per_mod, d),
            _mod_spec(0, mod_base, rows_per_mod, d),
            pl.BlockSpec((d, d_in), lambda i: (0, 0), pipeline_mode=pl.Buffered(1)),
        ],
        out_specs=[pl.BlockSpec((ROW_TILE, d_rg), row)] * 4,
        out_shape=[
            jax.ShapeDtypeStruct((m, d_rg), F32),
            jax.ShapeDtypeStruct((m, d_rg), F32),
            jax.ShapeDtypeStruct((m, d_rg), F32),
            jax.ShapeDtypeStruct((m, d_rg), BF16),
        ],
        compiler_params=pltpu.CompilerParams(
            dimension_semantics=("arbitrary",), vmem_limit_bytes=VMEM_LIMIT_BYTES),
        name="in_projection",
    )(x, norm_g.reshape(1, d), mod, mod, w_in)


def _seqmix_kernel(x_ref, xprev_ref, xnext_ref, gy_ref, u_ref, v_ref, h0_ref,
                   cw_ref, cb_ref, gw_ref, gb_ref, lam_ref, sw_ref, sb_ref,
                   mix_ref, st_ref, hf_scr, carry_scr, xi_scr, hs_scr):
    tc, d_rg = x_ref.shape
    tl = tc // N_SEG
    head = d_rg // N_RG_HEADS
    ch_head = d_rg // N_CH_HEADS
    halo = xprev_ref.shape[0]
    lead = CONV_LEFT * N_SEG
    p = pl.program_id(1)
    j = pl.program_id(2)
    n_chunks = pl.num_programs(2)
    jj = jnp.where(p == 0, j, n_chunks - 1 - j)
    row0 = pl.multiple_of(jj * tc, tc)
    sub = lax.broadcasted_iota(jnp.int32, (N_SEG, head), 0)

    @pl.when(j == 0)
    def _():
        carry_scr[...] = h0_ref[pl.ds(p, 1), :]

    def direction(reverse):
        dirn = 1 if reverse else 0
        steps = range(tl - 1, -1, -1) if reverse else range(tl)
        segs = range(N_SEG - 1, -1, -1) if reverse else range(N_SEG)
        for hd in range(N_RG_HEADS):
            cols = slice(hd * head, (hd + 1) * head)
            xi = xi_scr.at[hd]
            for sg in range(N_SEG):
                xi[pl.ds(lead + sg, tl, stride=N_SEG), :] = x_ref[sg * tl:(sg + 1) * tl, cols]

            def step_rows(t):
                return xi[lead + t * N_SEG:lead + (t + 1) * N_SEG, :]

            for k in range(CONV_LEFT):
                row = halo - CONV_LEFT + k
                edge = jnp.where(jj == 0, 0.0, xprev_ref[row:row + 1, cols])
                prev_step = pltpu.roll(step_rows(tl - CONV_LEFT + k), 1, axis=0)
                xi[k * N_SEG:(k + 1) * N_SEG, :] = jnp.where(sub == 0, edge, prev_step)
            for k in range(CONV_W - 1 - CONV_LEFT):
                edge = jnp.where(jj == n_chunks - 1, 0.0, xnext_ref[k:k + 1, cols])
                next_step = pltpu.roll(step_rows(k), N_SEG - 1, axis=0)
                xi[lead + (tl + k) * N_SEG:lead + (tl + k + 1) * N_SEG, :] = jnp.where(
                    sub == N_SEG - 1, edge, next_step)

            xc = cb_ref[:, cols]
            for k in range(CONV_W):
                xc = xc + cw_ref[k:k + 1, cols] * xi[k * N_SEG:k * N_SEG + tc, :]
            g = jnp.dot(xc.astype(BF16), gw_ref[dirn, hd], preferred_element_type=F32)
            r = jax.nn.sigmoid(g[:, :head] + gb_ref[dirn, 0:1, cols])
            i = jax.nn.sigmoid(g[:, head:] + gb_ref[dirn, 1:2, cols])
            log_a = r * (-RG_C * jax.nn.softplus(-lam_ref[dirn:dirn + 1, cols]))
            a = jnp.exp(log_a)
            m2 = jnp.maximum(-jnp.tanh(log_a) * (a * a + 1.0), 0.0)
            mult = jnp.where(m2 > 0.0, m2 * lax.rsqrt(m2), 0.0)
            bx = mult * (i * xc)

            a3 = a.reshape(tl, N_SEG, head)
            b3 = bx.reshape(tl, N_SEG, head)
            hloc = [None] * tl
            prod = [None] * tl
            hprev = pprev = None
            for t in steps:
                if hprev is None:
                    hprev, pprev = b3[t], a3[t]
                else:
                    hprev, pprev = a3[t] * hprev + b3[t], a3[t] * pprev
                hloc[t], prod[t] = hprev, pprev
            state = carry_scr[:, cols]
            init = jnp.zeros((N_SEG, head), F32)
            for sg in segs:
                init = jnp.where(sub == sg, state, init)
                state = hprev[sg:sg + 1, :] + pprev[sg:sg + 1, :] * state
            carry_scr[:, cols] = state
            hfull = jnp.concatenate([hloc[t] + prod[t] * init for t in range(tl)], axis=0)
            if not reverse:
                hf_scr[pl.ds(row0, tc), cols] = hfull
            else:
                hs = hs_scr.at[hd]
                hs[...] = hf_scr[pl.ds(row0, tc), cols] + hfull
                for sg in range(N_SEG):
                    rows = slice(sg * tl, (sg + 1) * tl)
                    hsum = hs[pl.ds(sg, tl, stride=N_SEG), :]
                    mix_ref[rows, cols] = (gy_ref[rows, cols] * hsum).astype(BF16)
        if reverse:
            for hd in range(N_CH_HEADS):
                cols = slice(hd * ch_head, (hd + 1) * ch_head)
                for n in range(tc // CHUNK):
                    rows = slice(n * CHUNK, (n + 1) * CHUNK)
                    mixed = jnp.dot(sw_ref[hd], v_ref[rows, cols],
                                    preferred_element_type=F32) + sb_ref[hd]
                    mix_ref[rows, d_rg + hd * ch_head:d_rg + (hd + 1) * ch_head] = (
                        u_ref[rows, cols] * mixed).astype(BF16)

        @pl.when(j == n_chunks - 1)
        def _():
            st_ref[dirn:dirn + 1, :] = carry_scr[...]

    @pl.when(p == 0)
    def _():
        direction(False)

    @pl.when(p == 1)
    def _():
        direction(True)


def _seqmix(xrg, gy, u, v, h0, seq_len, conv_w, conv_b, gw, gb, lam, sgu_w, sgu_b):
    m, d_rg = xrg.shape
    n_seq = m // seq_len
    tc = min(SEQ_TILE, seq_len)
    n_chunks = seq_len // tc
    halo_per_tile = tc // SUBLANES
    n_halo = m // SUBLANES
    head = d_rg // N_RG_HEADS

    def chunk(p, j):
        return jnp.where(p == 0, j, n_chunks - 1 - j)

    def cur(s, p, j):
        return (s * n_chunks + chunk(p, j), 0)

    def prev(s, p, j):
        return (jnp.maximum((s * n_chunks + chunk(p, j)) * halo_per_tile - 1, 0), 0)

    def nxt(s, p, j):
        return (jnp.minimum((s * n_chunks + chunk(p, j) + 1) * halo_per_tile, n_halo - 1), 0)

    def out_chunk(s, p, j):
        return (s * n_chunks + jnp.where(p == 0, n_chunks - 1, n_chunks - 1 - j), 0)

    full = lambda *shape: pl.BlockSpec(shape, lambda s, p, j: (0,) * len(shape))
    per_seq = pl.BlockSpec((None, 2, d_rg), lambda s, p, j: (s, 0, 0))
    return pl.pallas_call(
        _seqmix_kernel,
        grid=(n_seq, 2, n_chunks),
        in_specs=[
            pl.BlockSpec((tc, d_rg), cur),
            pl.BlockSpec((SUBLANES, d_rg), prev),
            pl.BlockSpec((SUBLANES, d_rg), nxt),
            pl.BlockSpec((tc, d_rg), out_chunk),
            pl.BlockSpec((tc, d_rg), out_chunk),
            pl.BlockSpec((tc, d_rg), out_chunk),
            per_seq,
            full(CONV_W, d_rg),
            full(1, d_rg),
            full(*gw.shape),
            full(*gb.shape),
            full(2, d_rg),
            full(*sgu_w.shape),
            full(*sgu_b.shape),
        ],
        out_specs=[pl.BlockSpec((tc, 2 * d_rg), out_chunk), per_seq],
        out_shape=[
            jax.ShapeDtypeStruct((m, 2 * d_rg), BF16),
            jax.ShapeDtypeStruct((n_seq, 2, d_rg), F32),
        ],
        scratch_shapes=[
            pltpu.VMEM((seq_len, d_rg), F32),
            pltpu.VMEM((1, d_rg), F32),
            pltpu.VMEM((N_RG_HEADS, tc + (CONV_W - 1) * N_SEG, head), F32),
            pltpu.VMEM((N_RG_HEADS, tc, head), F32),
        ],
        compiler_params=pltpu.CompilerParams(
            dimension_semantics=("arbitrary", "arbitrary", "arbitrary"),
            vmem_limit_bytes=VMEM_LIMIT_BYTES),
        name="sequence_mix",
    )(xrg, xrg, xrg, gy, u, v, h0, conv_w, conv_b, gw, gb, lam, sgu_w, sgu_b)


def _ffn_kernel(x_ref, mix_ref, g1_ref, sc_ref, sh_ref, g2_ref, ng_ref, fg_ref,
                wo_ref, w1_ref, w2_ref, o_ref, x1_scr, h2_scr, acc_scr, *, final_norm):
    j = pl.program_id(1)

    @pl.when(j == 0)
    def _():
        mix = jnp.dot(mix_ref[...], wo_ref[...], preferred_element_type=F32)
        x1 = x_ref[...] + g1_ref[...] * mix
        x1_scr[...] = x1
        h2 = (_rms(x1) * ng_ref[...]) * (1.0 + sc_ref[...]) + sh_ref[...]
        h2_scr[...] = h2.astype(BF16)
        acc_scr[...] = jnp.zeros_like(acc_scr)

    a = jnp.dot(h2_scr[...], w1_ref[...], preferred_element_type=F32)
    a = jnp.square(jnp.maximum(a, 0.0)).astype(BF16)
    acc_scr[...] += jnp.dot(a, w2_ref[...], preferred_element_type=F32)

    @pl.when(j == pl.num_programs(1) - 1)
    def _():
        x2 = x1_scr[...] + g2_ref[...] * acc_scr[...]
        if final_norm:
            x2 = _rms(x2) * fg_ref[...]
        o_ref[...] = x2


def _ffn(x, mix, mod, mod_base, rows_per_mod, norm_g, final_g, w_out, w_ff1, w_ff2, final_norm):
    m, d = x.shape
    d_mix = mix.shape[1]
    d_ff = w_ff1.shape[1]
    row = lambda i, j: (i, 0)
    vec = pl.BlockSpec((1, d), lambda i, j: (0, 0))
    return pl.pallas_call(
        functools.partial(_ffn_kernel, final_norm=final_norm),
        grid=(m // ROW_TILE, d_ff // FF_TILE),
        in_specs=[
            pl.BlockSpec((ROW_TILE, d), row),
            pl.BlockSpec((ROW_TILE, d_mix), row),
            _mod_spec(2, mod_base, rows_per_mod, d),
            _mod_spec(4, mod_base, rows_per_mod, d),
            _mod_spec(3, mod_base, rows_per_mod, d),
            _mod_spec(5, mod_base, rows_per_mod, d),
            vec,
            vec,
            pl.BlockSpec((d_mix, d), lambda i, j: (0, 0), pipeline_mode=pl.Buffered(1)),
            pl.BlockSpec((d, FF_TILE), lambda i, j: (0, j)),
            pl.BlockSpec((FF_TILE, d), lambda i, j: (j, 0)),
        ],
        out_specs=pl.BlockSpec((ROW_TILE, d), row),
        out_shape=jax.ShapeDtypeStruct((m, d), F32),
        scratch_shapes=[
            pltpu.VMEM((ROW_TILE, d), F32),
            pltpu.VMEM((ROW_TILE, d), BF16),
            pltpu.VMEM((ROW_TILE, d), F32),
        ],
        compiler_params=pltpu.CompilerParams(
            dimension_semantics=("arbitrary", "arbitrary"), vmem_limit_bytes=VMEM_LIMIT_BYTES),
        name="out_projection_ffn",
    )(x, mix, mod, mod, mod, mod, norm_g.reshape(1, d), final_g.reshape(1, d), w_out, w_ff1, w_ff2)


def _layer(x, mod, mod_base, rows_per_mod, seq_len, h0, p, final_g, final_norm):
    gy, xrg, u, v = _inproj(x, mod, mod_base, rows_per_mod, p["norm1_g"], p["w_in"])
    mix, st = _seqmix(xrg, gy, u, v, h0, seq_len, p["conv_w"], p["conv_b"], p["gw"], p["gb"],
                      p["lam"], p["sgu_w"], p["sgu_b"])
    y = _ffn(x, mix, mod, mod_base, rows_per_mod, p["norm2_g"], final_g,
             p["w_out"], p["w_ff1"], p["w_ff2"], final_norm)
    return y, st


def kernel(x_prompt, x_sample, c, state_rglru, c_ctx, norm1_g, w_ada, b_ada, w_in, conv_w, conv_b,
           ga_w, ga_b, gi_w, gi_b, lru_lambda, sgu_w, sgu_b, w_out, norm2_g, w_ff1, w_ff2, final_g):
    b_ctx, l_ctx, d = x_prompt.shape
    b_lat, l_lat, _ = x_sample.shape
    depth = w_in.shape[0]
    d_rg = lru_lambda.shape[-1]
    assert b_lat + 1 <= MOD_ROWS
    assert (b_ctx * l_ctx) % ROW_TILE == 0 and l_lat % ROW_TILE == 0

    cond = jnp.concatenate(
        [c, c_ctx[None, :], jnp.zeros((MOD_ROWS - b_lat - 1, d), F32)], axis=0)
    xp = x_prompt.reshape(b_ctx * l_ctx, d)
    xs = x_sample.reshape(b_lat * l_lat, d)
    h0_ctx = jnp.zeros((b_ctx, 2, d_rg), F32)
    ctx_states = []
    for l in range(depth):
        p = {
            "norm1_g": norm1_g[l], "w_in": w_in[l].astype(BF16),
            "conv_w": conv_w[l], "conv_b": conv_b[l].reshape(1, d_rg),
            "gw": jnp.concatenate([ga_w[l], gi_w[l]], axis=-1).astype(BF16),
            "gb": jnp.stack([ga_b[l], gi_b[l]], axis=1),
            "lam": lru_lambda[l],
            "sgu_w": sgu_w[l].astype(BF16), "sgu_b": sgu_b[l][:, :, None],
            "w_out": w_out[l].astype(BF16), "norm2_g": norm2_g[l],
            "w_ff1": w_ff1[l].astype(BF16), "w_ff2": w_ff2[l].astype(BF16),
        }
        mod = _modulation(cond, w_ada[l], b_ada[l])
        last = l == depth - 1
        xp, st = _layer(xp, mod, b_lat, b_ctx * l_ctx, l_ctx, h0_ctx, p, final_g, last)
        ctx_states.append(st)
        xs, _ = _layer(xs, mod, 0, l_lat, l_lat, state_rglru[:, l].astype(F32), p, final_g, last)
    y_prompt = xp.reshape(b_ctx, l_ctx, d)
    y_sample = xs.reshape(b_lat, l_lat, d)
    new_state = jnp.stack(ctx_states, axis=1).astype(x_prompt.dtype)
    return (y_prompt, y_sample, new_state)
```

```python
import functools

import jax
import jax.numpy as jnp
from jax import lax
from jax.experimental import pallas as pl
from jax.experimental.pallas import tpu as pltpu

F32 = jnp.float32
BF16 = jnp.bfloat16

N_RG_HEADS = 8
N_CH_HEADS = 8
CHUNK = 128
CONV_W = 4
CONV_LEFT = 2
RG_C = 8.0
N_MOD = 6
EPS = 1e-6

SUBLANES = 8
MOD_ROWS = 8
VMEM_LIMIT_BYTES = 62 * 1024 * 1024

ROW_TILE = 512
FF_TILE_FUSED = 512
FF_TILE_ALONE = 1024
FF_PIECE = 256
MOD_COL_TILE = 1024
PROJ_PIECE = 256
N_SEG = SUBLANES
HALO = SUBLANES


def _rms(x):
    return x * lax.rsqrt(jnp.mean(x * x, axis=-1, keepdims=True) + EPS)


def _params(**kw):
    return pltpu.CompilerParams(vmem_limit_bytes=VMEM_LIMIT_BYTES, **kw)


def _mod_kernel(c_ref, w_ref, b_ref, o_ref):
    s = jax.nn.silu(c_ref[...]).astype(BF16)
    o_ref[...] = jnp.dot(s, w_ref[...].astype(BF16), preferred_element_type=F32) + b_ref[...]


def _modulation(cond, w_ada, b_ada):
    d = cond.shape[1]
    n = w_ada.shape[1]
    out = pl.pallas_call(
        _mod_kernel,
        grid=(n // MOD_COL_TILE,),
        in_specs=[
            pl.BlockSpec((MOD_ROWS, d), lambda j: (0, 0)),
            pl.BlockSpec((d, MOD_COL_TILE), lambda j: (0, j)),
            pl.BlockSpec((1, MOD_COL_TILE), lambda j: (0, j)),
        ],
        out_specs=pl.BlockSpec((MOD_ROWS, MOD_COL_TILE), lambda j: (0, j)),
        out_shape=jax.ShapeDtypeStruct((MOD_ROWS, n), F32),
        compiler_params=_params(dimension_semantics=("arbitrary",)),
        name="modulation",
    )(cond, w_ada, b_ada.reshape(1, n))
    return out.reshape(MOD_ROWS, N_MOD, 1, d)


def _mod_spec(which, grp, d):
    mod_base, rows_per_mod = grp

    def index(i, *_):
        return (mod_base + (i * ROW_TILE) // rows_per_mod, which, 0, 0)
    return pl.BlockSpec((None, None, 1, d), index)


def _inproj_body(x_ref, g_ref, sc_ref, sh_ref, w_ref, gy_ref, xrg_ref, u_ref, v_ref, between=()):
    d_rg = gy_ref.shape[1]
    h = (_rms(x_ref[...]) * g_ref[...]) * (1.0 + sc_ref[...]) + sh_ref[...]
    hb = h.astype(BF16)
    outs = (gy_ref, xrg_ref, u_ref, v_ref)
    between = list(between)
    for n in range(w_ref.shape[1] // PROJ_PIECE):
        part = jnp.dot(hb, w_ref[:, n * PROJ_PIECE:(n + 1) * PROJ_PIECE], preferred_element_type=F32)
        dest, col = divmod(n * PROJ_PIECE, d_rg)
        if dest != 1:
            part = jax.nn.gelu(part)
        outs[dest][:, col:col + PROJ_PIECE] = part.astype(outs[dest].dtype)
        if between:
            between.pop(0)()
    for item in between:
        item()


def _inproj_specs(m, d, d_in, grp):
    d_rg = d_in // 4
    row = lambda i, *_: (i, 0)
    in_specs = [
        pl.BlockSpec((ROW_TILE, d), row),
        pl.BlockSpec((1, d), lambda i, *_: (0, 0)),
        _mod_spec(1, grp, d),
        _mod_spec(0, grp, d),
        pl.BlockSpec((d, d_in), lambda i, *_: (0, 0), pipeline_mode=pl.Buffered(1)),
    ]
    out_specs = [pl.BlockSpec((ROW_TILE, d_rg), row)] * 4
    out_shape = [
        jax.ShapeDtypeStruct((m, d_rg), F32),
        jax.ShapeDtypeStruct((m, d_rg), F32),
        jax.ShapeDtypeStruct((m, d_rg), F32),
        jax.ShapeDtypeStruct((m, d_rg), BF16),
    ]
    return in_specs, out_specs, out_shape


def _scan_head(hd, reverse, edges, hf_rows, x_ref, gy_ref, cw_ref, cb_ref, gw_ref, gb_ref, lam_ref,
               mix_ref, hf_scr, carry_scr, xi_scr, hs_scr):
    tc, d_rg = x_ref.shape
    tl = tc // N_SEG
    head = d_rg // N_RG_HEADS
    lead = CONV_LEFT * N_SEG
    dirn = 1 if reverse else 0
    steps = range(tl - 1, -1, -1) if reverse else range(tl)
    segs = range(N_SEG - 1, -1, -1) if reverse else range(N_SEG)
    sub = lax.broadcasted_iota(jnp.int32, (N_SEG, head), 0)
    cols = slice(hd * head, (hd + 1) * head)
    xi = xi_scr.at[hd]
    for sg in range(N_SEG):
        xi[pl.ds(lead + sg, tl, stride=N_SEG), :] = x_ref[sg * tl:(sg + 1) * tl, cols]

    def step_rows(t):
        return xi[lead + t * N_SEG:lead + (t + 1) * N_SEG, :]

    for k in range(CONV_LEFT):
        if edges is None:
            edge = 0.0
        else:
            row = HALO - CONV_LEFT + k
            edge = jnp.where(edges[0], 0.0, edges[1][row:row + 1, cols])
        prev_step = pltpu.roll(step_rows(tl - CONV_LEFT + k), 1, axis=0)
        xi[k * N_SEG:(k + 1) * N_SEG, :] = jnp.where(sub == 0, edge, prev_step)
    for k in range(CONV_W - 1 - CONV_LEFT):
        edge = 0.0 if edges is None else jnp.where(edges[2], 0.0, edges[3][k:k + 1, cols])
        next_step = pltpu.roll(step_rows(k), N_SEG - 1, axis=0)
        xi[lead + (tl + k) * N_SEG:lead + (tl + k + 1) * N_SEG, :] = jnp.where(
            sub == N_SEG - 1, edge, next_step)

    xc = cb_ref[:, cols]
    for k in range(CONV_W):
        xc = xc + cw_ref[k:k + 1, cols] * xi[k * N_SEG:k * N_SEG + tc, :]
    g = jnp.dot(xc.astype(BF16), gw_ref[dirn, hd], preferred_element_type=F32)
    r = jax.nn.sigmoid(g[:, :head] + gb_ref[dirn, 0:1, cols])
    i = jax.nn.sigmoid(g[:, head:] + gb_ref[dirn, 1:2, cols])
    log_a = r * (-RG_C * jax.nn.softplus(-lam_ref[dirn:dirn + 1, cols]))
    a = jnp.exp(log_a)
    m2 = jnp.maximum(-jnp.tanh(log_a) * (a * a + 1.0), 0.0)
    mult = jnp.where(m2 > 0.0, m2 * lax.rsqrt(m2), 0.0)
    bx = mult * (i * xc)

    a3 = a.reshape(tl, N_SEG, head)
    b3 = bx.reshape(tl, N_SEG, head)
    hloc = [None] * tl
    prod = [None] * tl
    hprev = pprev = None
    for t in steps:
        if hprev is None:
            hprev, pprev = b3[t], a3[t]
        else:
            hprev, pprev = a3[t] * hprev + b3[t], a3[t] * pprev
        hloc[t], prod[t] = hprev, pprev
    state = carry_scr[:, cols]
    init = jnp.zeros((N_SEG, head), F32)
    for sg in segs:
        init = jnp.where(sub == sg, state, init)
        state = hprev[sg:sg + 1, :] + pprev[sg:sg + 1, :] * state
    carry_scr[:, cols] = state
    hfull = jnp.concatenate([hloc[t] + prod[t] * init for t in range(tl)], axis=0)
    if not reverse:
        hf_scr[hf_rows, cols] = hfull
    else:
        hs = hs_scr.at[hd]
        hs[...] = hf_scr[hf_rows, cols] + hfull
        for sg in range(N_SEG):
            rows = slice(sg * tl, (sg + 1) * tl)
            hsum = hs[pl.ds(sg, tl, stride=N_SEG), :]
            mix_ref[rows, cols] = (gy_ref[rows, cols] * hsum).astype(BF16)


def _sgu_head(hd, u_ref, v_ref, sw_ref, sb_ref, mix_ref):
    tc, d_ch = u_ref.shape
    ch_head = d_ch // N_CH_HEADS
    d_rg = mix_ref.shape[1] - d_ch
    cols = slice(hd * ch_head, (hd + 1) * ch_head)
    for n in range(tc // CHUNK):
        rows = slice(n * CHUNK, (n + 1) * CHUNK)
        mixed = jnp.dot(sw_ref[hd], v_ref[rows, cols], preferred_element_type=F32) + sb_ref[hd]
        mix_ref[rows, d_rg + hd * ch_head:d_rg + (hd + 1) * ch_head] = (
            u_ref[rows, cols] * mixed).astype(BF16)


def _seq_items(reverse, edges, hf_rows, x_ref, gy_ref, u_ref, v_ref,
               cw_ref, cb_ref, gw_ref, gb_ref, lam_ref, sw_ref, sb_ref,
               mix_ref, hf_scr, carry_scr, xi_scr, hs_scr):
    def item(hd):
        def run():
            _scan_head(hd, reverse, edges, hf_rows, x_ref, gy_ref, cw_ref, cb_ref, gw_ref, gb_ref,
                       lam_ref, mix_ref, hf_scr, carry_scr, xi_scr, hs_scr)
            if reverse:
                _sgu_head(hd, u_ref, v_ref, sw_ref, sb_ref, mix_ref)
        return run
    return [item(hd) for hd in range(N_RG_HEADS)]


def _seq_weight_specs(sp):
    full = lambda a: pl.BlockSpec(a.shape, lambda *_: (0,) * a.ndim, pipeline_mode=pl.Buffered(1))
    return [full(sp[k]) for k in ("conv_w", "conv_b", "gw", "gb", "lam", "sgu_w", "sgu_b")]


def _seq_weights(sp):
    return [sp[k] for k in ("conv_w", "conv_b", "gw", "gb", "lam", "sgu_w", "sgu_b")]


def _seq_scratch(seq_len, tc, d_rg):
    head = d_rg // N_RG_HEADS
    return [
        pltpu.VMEM((seq_len, d_rg), F32),
        pltpu.VMEM((1, d_rg), F32),
        pltpu.VMEM((N_RG_HEADS, tc + (CONV_W - 1) * N_SEG, head), F32),
        pltpu.VMEM((N_RG_HEADS, tc, head), F32),
    ]


def _inproj_seq_kernel(*refs):
    x_ref, gy_ref, u_ref, v_ref, h0_ref = refs[5:10]
    weights = refs[10:17]
    mix_ref, st_ref = refs[21:23]
    hf_scr, carry_scr, xi_scr, hs_scr = refs[23:27]
    tc = x_ref.shape[0]

    def set_state(dirn):
        def run():
            carry_scr[...] = h0_ref[dirn:dirn + 1, :]
        return run

    def keep_state(dirn):
        def run():
            st_ref[dirn:dirn + 1, :] = carry_scr[...]
        return run

    def then(first, item):
        def run():
            first()
            item()
        return run

    items = []
    for reverse in (False, True):
        dirn = 1 if reverse else 0
        heads = _seq_items(reverse, None, slice(0, tc), x_ref, gy_ref, u_ref, v_ref, *weights,
                           mix_ref, hf_scr, carry_scr, xi_scr, hs_scr)
        heads[0] = then(set_state(dirn), heads[0])
        heads[-1] = then(heads[-1], keep_state(dirn))
        items += heads
    _inproj_body(*refs[:5], *refs[17:21], between=items)


def _inproj_seq(x, mod, grp, norm_g, w_in, seq, sp):
    m, d = x.shape
    in_specs, out_specs, out_shape = _inproj_specs(m, d, w_in.shape[1], grp)
    args = [x, norm_g.reshape(1, d), mod, mod, w_in]
    if seq is None:
        kern = _inproj_body
        scratch = []
    else:
        xrg, gy, u, v, h0, seq_len = seq
        ms, d_rg = xrg.shape
        assert ms // seq_len == m // ROW_TILE
        blk = lambda i: (i, 0)
        per_seq = pl.BlockSpec((None, 2, d_rg), lambda i: (i, 0, 0))
        in_specs = in_specs + [pl.BlockSpec((seq_len, d_rg), blk)] * 4 + [per_seq] + _seq_weight_specs(sp)
        args = args + [xrg, gy, u, v, h0] + _seq_weights(sp)
        out_specs = out_specs + [pl.BlockSpec((seq_len, 2 * d_rg), blk), per_seq]
        out_shape = out_shape + [jax.ShapeDtypeStruct((ms, 2 * d_rg), BF16),
                                 jax.ShapeDtypeStruct((ms // seq_len, 2, d_rg), F32)]
        kern = _inproj_seq_kernel
        scratch = _seq_scratch(seq_len, seq_len, d_rg)
    return pl.pallas_call(
        kern,
        grid=(m // ROW_TILE,),
        in_specs=in_specs,
        out_specs=out_specs,
        out_shape=out_shape,
        scratch_shapes=scratch,
        compiler_params=_params(dimension_semantics=("arbitrary",)),
        name="in_projection" if seq is None else "in_projection_seqmix",
    )(*args)


def _ffn_first(x_ref, mix_ref, g1_ref, sc_ref, sh_ref, ng_ref, wo_ref, o_ref, h2_scr, acc_scr):
    mix = jnp.dot(mix_ref[...], wo_ref[...], preferred_element_type=F32)
    x1 = x_ref[...] + g1_ref[...] * mix
    o_ref[...] = x1
    h2 = (_rms(x1) * ng_ref[...]) * (1.0 + sc_ref[...]) + sh_ref[...]
    h2_scr[...] = h2.astype(BF16)
    acc_scr[...] = jnp.zeros_like(acc_scr)


def _ffn_step(w1_ref, w2_ref, h2_scr, acc_scr, between=()):
    between = list(between)
    n_pieces = w1_ref.shape[0] + 2
    per = -(-len(between) // n_pieces)
    half_d = w2_ref.shape[1] // 2

    def gap():
        for _ in range(min(per, len(between))):
            between.pop(0)()

    h2 = h2_scr[...]
    acts = []
    for n in range(w1_ref.shape[0]):
        gap()
        a = jnp.dot(h2, w1_ref[n], preferred_element_type=F32)
        acts.append(jnp.square(jnp.maximum(a, 0.0)).astype(BF16))
    act = jnp.concatenate(acts, axis=1)
    for n in range(2):
        gap()
        cols = slice(n * half_d, (n + 1) * half_d)
        acc_scr[:, cols] += jnp.dot(act, w2_ref[:, cols], preferred_element_type=F32)
    for item in between:
        item()


def _ffn_last(g2_ref, fg_ref, o_ref, acc_scr, final_norm):
    x2 = o_ref[...] + g2_ref[...] * acc_scr[...]
    if final_norm:
        x2 = _rms(x2) * fg_ref[...]
    o_ref[...] = x2


def _ffn_kernel(x_ref, mix_ref, g1_ref, sc_ref, sh_ref, g2_ref, ng_ref, fg_ref,
                wo_ref, w1_ref, w2_ref, o_ref, h2_scr, acc_scr, *, final_norm):
    j = pl.program_id(1)

    @pl.when(j == 0)
    def _():
        _ffn_first(x_ref, mix_ref, g1_ref, sc_ref, sh_ref, ng_ref, wo_ref, o_ref, h2_scr, acc_scr)

    _ffn_step(w1_ref, w2_ref, h2_scr, acc_scr)

    @pl.when(j == pl.num_programs(1) - 1)
    def _():
        _ffn_last(g2_ref, fg_ref, o_ref, acc_scr, final_norm)


def _ffn_seq_kernel(x_ref, mix_ref, g1_ref, sc_ref, sh_ref, g2_ref, ng_ref, fg_ref,
                    wo_ref, w1_ref, w2_ref,
                    sx_ref, sprev_ref, snext_ref, gy_ref, u_ref, v_ref, h0_ref,
                    cw_ref, cb_ref, gw_ref, gb_ref, lam_ref, sw_ref, sb_ref,
                    o_ref, smix_ref,
                    h2_scr, acc_scr, hf_scr, carry_scr, xi_scr, hs_scr, *, final_norm):
    i = pl.program_id(0)
    j = pl.program_id(1)
    n_chunks = pl.num_programs(1)
    tc = sx_ref.shape[0]
    backward = i % 2 == 1
    jj = jnp.where(backward, n_chunks - 1 - j, j)
    hf_rows = pl.ds(pl.multiple_of(jj * tc, tc), tc)
    edges = (jj == 0, sprev_ref, jj == n_chunks - 1, snext_ref)

    @pl.when(j == 0)
    def _():
        _ffn_first(x_ref, mix_ref, g1_ref, sc_ref, sh_ref, ng_ref, wo_ref, o_ref, h2_scr, acc_scr)
        carry_scr[...] = h0_ref[pl.ds(i % 2, 1), :]

    def both(reverse):
        items = _seq_items(reverse, edges, hf_rows, sx_ref, gy_ref, u_ref, v_ref,
                           cw_ref, cb_ref, gw_ref, gb_ref, lam_ref, sw_ref, sb_ref,
                           smix_ref, hf_scr, carry_scr, xi_scr, hs_scr)
        _ffn_step(w1_ref, w2_ref, h2_scr, acc_scr, between=items)

    @pl.when(jnp.logical_not(backward))
    def _():
        both(False)

    @pl.when(backward)
    def _():
        both(True)

    @pl.when(j == n_chunks - 1)
    def _():
        _ffn_last(g2_ref, fg_ref, o_ref, acc_scr, final_norm)


def _ffn(x, mix, mod, grp, norm_g, final_g, w_out, w_ff1, w_ff2, final_norm, ff_tile, seq, sp):
    m, d = x.shape
    d_mix = mix.shape[1]
    d_ff = w_ff2.shape[0]
    n_steps = d_ff // ff_tile
    row = lambda i, j: (i, 0)
    vec = pl.BlockSpec((1, d), lambda i, j: (0, 0))
    in_specs = [
        pl.BlockSpec((ROW_TILE, d), row),
        pl.BlockSpec((ROW_TILE, d_mix), row),
        _mod_spec(2, grp, d),
        _mod_spec(4, grp, d),
        _mod_spec(3, grp, d),
        _mod_spec(5, grp, d),
        vec,
        vec,
        pl.BlockSpec((d_mix, d), lambda i, j: (0, 0), pipeline_mode=pl.Buffered(1)),
        pl.BlockSpec((ff_tile // FF_PIECE, d, FF_PIECE), lambda i, j: (j, 0, 0)),
        pl.BlockSpec((ff_tile, d), lambda i, j: (j, 0)),
    ]
    args = [x, mix, mod, mod, mod, mod, norm_g.reshape(1, d), final_g.reshape(1, d), w_out, w_ff1, w_ff2]
    out_specs = pl.BlockSpec((ROW_TILE, d), row)
    out_shape = jax.ShapeDtypeStruct((m, d), F32)
    scratch = [
        pltpu.VMEM((ROW_TILE, d), BF16),
        pltpu.VMEM((ROW_TILE, d), F32),
    ]
    if seq is None:
        kern = functools.partial(_ffn_kernel, final_norm=final_norm)
        name = "out_projection_ffn"
    else:
        xrg, gy, u, v, h0, seq_len = seq
        ms, d_rg = xrg.shape
        n_seq = ms // seq_len
        tc = seq_len // n_steps
        assert 2 * n_seq == m // ROW_TILE and tc % (N_SEG * SUBLANES) == 0 and tc % CHUNK == 0
        per_tile = tc // HALO
        n_halo = ms // HALO

        def chunk(i, j):
            return (i // 2) * n_steps + jnp.where(i % 2 == 1, n_steps - 1 - j, j)

        def out_chunk(i, j):
            return ((i // 2) * n_steps + jnp.where(i % 2 == 1, n_steps - 1 - j, n_steps - 1), 0)

        blk = lambda rows, cols, index: pl.BlockSpec((rows, cols), index)
        in_specs = in_specs + [
            blk(tc, d_rg, lambda i, j: (chunk(i, j), 0)),
            blk(HALO, d_rg, lambda i, j: (jnp.maximum(chunk(i, j) * per_tile - 1, 0), 0)),
            blk(HALO, d_rg, lambda i, j: (jnp.minimum((chunk(i, j) + 1) * per_tile, n_halo - 1), 0)),
            blk(tc, d_rg, out_chunk),
            blk(tc, d_rg, out_chunk),
            blk(tc, d_rg, out_chunk),
            pl.BlockSpec((None, 2, d_rg), lambda i, j: (i // 2, 0, 0)),
        ] + _seq_weight_specs(sp)
        args = args + [xrg, xrg, xrg, gy, u, v, h0] + _seq_weights(sp)
        out_specs = [out_specs, blk(tc, 2 * d_rg, out_chunk)]
        out_shape = [out_shape, jax.ShapeDtypeStruct((ms, 2 * d_rg), BF16)]
        scratch = scratch + _seq_scratch(seq_len, tc, d_rg)
        kern = functools.partial(_ffn_seq_kernel, final_norm=final_norm)
        name = "out_projection_ffn_seqmix"
    return pl.pallas_call(
        kern,
        grid=(m // ROW_TILE, n_steps),
        in_specs=in_specs,
        out_specs=out_specs,
        out_shape=out_shape,
        scratch_shapes=scratch,
        compiler_params=_params(dimension_semantics=("arbitrary", "arbitrary")),
        name=name,
    )(*args)


def kernel(x_prompt, x_sample, c, state_rglru, c_ctx, norm1_g, w_ada, b_ada, w_in, conv_w, conv_b,
           ga_w, ga_b, gi_w, gi_b, lru_lambda, sgu_w, sgu_b, w_out, norm2_g, w_ff1, w_ff2, final_g):
    b_ctx, l_ctx, d = x_prompt.shape
    b_lat, l_lat, _ = x_sample.shape
    depth = w_in.shape[0]
    d_rg = lru_lambda.shape[-1]
    assert b_lat + 1 <= MOD_ROWS
    assert (b_ctx * l_ctx) % ROW_TILE == 0 and l_lat % ROW_TILE == 0

    cond = jnp.concatenate(
        [c, c_ctx[None, :], jnp.zeros((MOD_ROWS - b_lat - 1, d), F32)], axis=0)
    grp_ctx = (b_lat, b_ctx * l_ctx)
    grp_lat = (0, l_lat)
    xp = x_prompt.reshape(b_ctx * l_ctx, d)
    xs = x_sample.reshape(b_lat * l_lat, d)
    h0_ctx = jnp.zeros((b_ctx, 2, d_rg), F32)
    ctx_states = []
    for l in range(depth):
        sp = {
            "conv_w": conv_w[l], "conv_b": conv_b[l].reshape(1, d_rg),
            "gw": jnp.concatenate([ga_w[l], gi_w[l]], axis=-1).astype(BF16),
            "gb": jnp.stack([ga_b[l], gi_b[l]], axis=1),
            "lam": lru_lambda[l],
            "sgu_w": sgu_w[l].astype(BF16), "sgu_b": sgu_b[l][:, :, None],
        }
        win, wout = w_in[l].astype(BF16), w_out[l].astype(BF16)
        d_ff = w_ff1.shape[-1]
        wff1 = w_ff1[l].reshape(d, d_ff // FF_PIECE, FF_PIECE).transpose(1, 0, 2).astype(BF16)
        wff2 = w_ff2[l].astype(BF16)
        last = l == depth - 1
        mod = _modulation(cond, w_ada[l], b_ada[l])
        h0_lat = state_rglru[:, l].astype(F32)

        gy_c, xrg_c, u_c, v_c = _inproj_seq(xp, mod, grp_ctx, norm1_g[l], win, None, None)
        gy_l, xrg_l, u_l, v_l, mix_c, st = _inproj_seq(
            xs, mod, grp_lat, norm1_g[l], win, (xrg_c, gy_c, u_c, v_c, h0_ctx, l_ctx), sp)
        xp, mix_l = _ffn(xp, mix_c, mod, grp_ctx, norm2_g[l], final_g, wout, wff1, wff2, last,
                         FF_TILE_FUSED, (xrg_l, gy_l, u_l, v_l, h0_lat, l_lat), sp)
        xs = _ffn(xs, mix_l, mod, grp_lat, norm2_g[l], final_g, wout, wff1, wff2, last,
                  FF_TILE_ALONE, None, None)
        ctx_states.append(st)
    y_prompt = xp.reshape(b_ctx, l_ctx, d)
    y_sample = xs.reshape(b_lat, l_lat, d)
    new_state = jnp.stack(ctx_states, axis=1).astype(x_prompt.dtype)
    return (y_prompt, y_sample, new_state)
```

```python
import functools

import jax
import jax.numpy as jnp
from jax import lax
from jax.experimental import pallas as pl
from jax.experimental.pallas import tpu as pltpu

F32 = jnp.float32
BF16 = jnp.bfloat16

N_RG_HEADS = 8
N_CH_HEADS = 8
CHUNK = 128
CONV_W = 4
CONV_LEFT = 2
RG_C = 8.0
N_MOD = 6
EPS = 1e-6

SUBLANES = 8
MOD_ROWS = 8
VMEM_LIMIT_BYTES = 62 * 1024 * 1024

ROW_TILE = 512
FF_TILE_FUSED = 512
FF_TILE_ALONE = 1024
FF_PIECE = 256
MOD_COL_TILE = 1024
PROJ_PIECE = 256
N_SEG = SUBLANES
HALO = SUBLANES


def _rms(x):
    return x * lax.rsqrt(jnp.mean(x * x, axis=-1, keepdims=True) + EPS)


def _params(**kw):
    return pltpu.CompilerParams(vmem_limit_bytes=VMEM_LIMIT_BYTES, **kw)


def _mod_kernel(c_ref, w_ref, b_ref, o_ref):
    s = jax.nn.silu(c_ref[...]).astype(BF16)
    o_ref[...] = jnp.dot(s, w_ref[...].astype(BF16), preferred_element_type=F32) + b_ref[...]


def _modulation(cond, w_ada, b_ada):
    d = cond.shape[1]
    n = w_ada.shape[1]
    out = pl.pallas_call(
        _mod_kernel,
        grid=(n // MOD_COL_TILE,),
        in_specs=[
            pl.BlockSpec((MOD_ROWS, d), lambda j: (0, 0)),
            pl.BlockSpec((d, MOD_COL_TILE), lambda j: (0, j)),
            pl.BlockSpec((1, MOD_COL_TILE), lambda j: (0, j)),
        ],
        out_specs=pl.BlockSpec((MOD_ROWS, MOD_COL_TILE), lambda j: (0, j)),
        out_shape=jax.ShapeDtypeStruct((MOD_ROWS, n), F32),
        compiler_params=_params(dimension_semantics=("arbitrary",)),
        name="modulation",
    )(cond, w_ada, b_ada.reshape(1, n))
    return out.reshape(MOD_ROWS, N_MOD, 1, d)


def _mod_spec(which, grp, d, tile=lambda i, *_: i):
    mod_base, rows_per_mod = grp

    def index(*ids):
        return (mod_base + (tile(*ids) * ROW_TILE) // rows_per_mod, which, 0, 0)
    return pl.BlockSpec((None, None, 1, d), index)


def _inproj_body(x_ref, g_ref, sc_ref, sh_ref, w_ref, gy_ref, xrg_ref, u_ref, v_ref, between=()):
    d_rg = gy_ref.shape[1]
    h = (_rms(x_ref[...]) * g_ref[...]) * (1.0 + sc_ref[...]) + sh_ref[...]
    hb = h.astype(BF16)
    outs = (gy_ref, xrg_ref, u_ref, v_ref)
    between = list(between)
    for n in range(w_ref.shape[1] // PROJ_PIECE):
        part = jnp.dot(hb, w_ref[:, n * PROJ_PIECE:(n + 1) * PROJ_PIECE], preferred_element_type=F32)
        dest, col = divmod(n * PROJ_PIECE, d_rg)
        if dest != 1:
            part = jax.nn.gelu(part)
        outs[dest][:, col:col + PROJ_PIECE] = part.astype(outs[dest].dtype)
        if between:
            between.pop(0)()
    for item in between:
        item()


def _inproj_specs(m, d, d_in, grp):
    d_rg = d_in // 4
    row = lambda i, *_: (i, 0)
    in_specs = [
        pl.BlockSpec((ROW_TILE, d), row),
        pl.BlockSpec((1, d), lambda i, *_: (0, 0)),
        _mod_spec(1, grp, d),
        _mod_spec(0, grp, d),
        pl.BlockSpec((d, d_in), lambda i, *_: (0, 0), pipeline_mode=pl.Buffered(1)),
    ]
    out_specs = [pl.BlockSpec((ROW_TILE, d_rg), row)] * 4
    out_shape = [
        jax.ShapeDtypeStruct((m, d_rg), F32),
        jax.ShapeDtypeStruct((m, d_rg), F32),
        jax.ShapeDtypeStruct((m, d_rg), F32),
        jax.ShapeDtypeStruct((m, d_rg), BF16),
    ]
    return in_specs, out_specs, out_shape


def _scan_head(hd, reverse, edges, hf_rows, x_ref, gy_ref, cw_ref, cb_ref, gw_ref, gb_ref, lam_ref,
               mix_ref, hf_scr, carry_scr, xi_scr, hs_scr):
    tc, d_rg = x_ref.shape
    tl = tc // N_SEG
    head = d_rg // N_RG_HEADS
    lead = CONV_LEFT * N_SEG
    dirn = 1 if reverse else 0
    steps = range(tl - 1, -1, -1) if reverse else range(tl)
    segs = range(N_SEG - 1, -1, -1) if reverse else range(N_SEG)
    sub = lax.broadcasted_iota(jnp.int32, (N_SEG, head), 0)
    cols = slice(hd * head, (hd + 1) * head)
    xi = xi_scr.at[hd]
    for sg in range(N_SEG):
        xi[pl.ds(lead + sg, tl, stride=N_SEG), :] = x_ref[sg * tl:(sg + 1) * tl, cols]

    def step_rows(t):
        return xi[lead + t * N_SEG:lead + (t + 1) * N_SEG, :]

    for k in range(CONV_LEFT):
        if edges is None:
            edge = 0.0
        else:
            row = HALO - CONV_LEFT + k
            edge = jnp.where(edges[0], 0.0, edges[1][row:row + 1, cols])
        prev_step = pltpu.roll(step_rows(tl - CONV_LEFT + k), 1, axis=0)
        xi[k * N_SEG:(k + 1) * N_SEG, :] = jnp.where(sub == 0, edge, prev_step)
    for k in range(CONV_W - 1 - CONV_LEFT):
        edge = 0.0 if edges is None else jnp.where(edges[2], 0.0, edges[3][k:k + 1, cols])
        next_step = pltpu.roll(step_rows(k), N_SEG - 1, axis=0)
        xi[lead + (tl + k) * N_SEG:lead + (tl + k + 1) * N_SEG, :] = jnp.where(
            sub == N_SEG - 1, edge, next_step)

    xc = cb_ref[:, cols]
    for k in range(CONV_W):
        xc = xc + cw_ref[k:k + 1, cols] * xi[k * N_SEG:k * N_SEG + tc, :]
    g = jnp.dot(xc.astype(BF16), gw_ref[dirn, hd], preferred_element_type=F32)
    r = jax.nn.sigmoid(g[:, :head] + gb_ref[dirn, 0:1, cols])
    i = jax.nn.sigmoid(g[:, head:] + gb_ref[dirn, 1:2, cols])
    log_a = r * (-RG_C * jax.nn.softplus(-lam_ref[dirn:dirn + 1, cols]))
    a = jnp.exp(log_a)
    m2 = jnp.maximum(-jnp.tanh(log_a) * (a * a + 1.0), 0.0)
    mult = jnp.where(m2 > 0.0, m2 * lax.rsqrt(m2), 0.0)
    bx = mult * (i * xc)

    a3 = a.reshape(tl, N_SEG, head)
    b3 = bx.reshape(tl, N_SEG, head)
    hloc = [None] * tl
    prod = [None] * tl
    hprev = pprev = None
    for t in steps:
        if hprev is None:
            hprev, pprev = b3[t], a3[t]
        else:
            hprev, pprev = a3[t] * hprev + b3[t], a3[t] * pprev
        hloc[t], prod[t] = hprev, pprev
    state = carry_scr[:, cols]
    init = jnp.zeros((N_SEG, head), F32)
    for sg in segs:
        init = jnp.where(sub == sg, state, init)
        state = hprev[sg:sg + 1, :] + pprev[sg:sg + 1, :] * state
    carry_scr[:, cols] = state
    hfull = jnp.concatenate([hloc[t] + prod[t] * init for t in range(tl)], axis=0)
    if not reverse:
        hf_scr[hf_rows, cols] = hfull
    else:
        hs = hs_scr.at[hd]
        hs[...] = hf_scr[hf_rows, cols] + hfull
        for sg in range(N_SEG):
            rows = slice(sg * tl, (sg + 1) * tl)
            hsum = hs[pl.ds(sg, tl, stride=N_SEG), :]
            mix_ref[rows, cols] = (gy_ref[rows, cols] * hsum).astype(BF16)


def _sgu_head(hd, u_ref, v_ref, sw_ref, sb_ref, mix_ref):
    tc, d_ch = u_ref.shape
    ch_head = d_ch // N_CH_HEADS
    d_rg = mix_ref.shape[1] - d_ch
    cols = slice(hd * ch_head, (hd + 1) * ch_head)
    for n in range(tc // CHUNK):
        rows = slice(n * CHUNK, (n + 1) * CHUNK)
        mixed = jnp.dot(sw_ref[hd], v_ref[rows, cols], preferred_element_type=F32) + sb_ref[hd]
        mix_ref[rows, d_rg + hd * ch_head:d_rg + (hd + 1) * ch_head] = (
            u_ref[rows, cols] * mixed).astype(BF16)


def _seq_items(reverse, edges, hf_rows, x_ref, gy_ref, u_ref, v_ref,
               cw_ref, cb_ref, gw_ref, gb_ref, lam_ref, sw_ref, sb_ref,
               mix_ref, hf_scr, carry_scr, xi_scr, hs_scr):
    def item(hd):
        def run():
            _scan_head(hd, reverse, edges, hf_rows, x_ref, gy_ref, cw_ref, cb_ref, gw_ref, gb_ref,
                       lam_ref, mix_ref, hf_scr, carry_scr, xi_scr, hs_scr)
            if reverse:
                _sgu_head(hd, u_ref, v_ref, sw_ref, sb_ref, mix_ref)
        return run
    return [item(hd) for hd in range(N_RG_HEADS)]


def _seq_weight_specs(sp):
    full = lambda a: pl.BlockSpec(a.shape, lambda *_: (0,) * a.ndim, pipeline_mode=pl.Buffered(1))
    return [full(sp[k]) for k in ("conv_w", "conv_b", "gw", "gb", "lam", "sgu_w", "sgu_b")]


def _seq_weights(sp):
    return [sp[k] for k in ("conv_w", "conv_b", "gw", "gb", "lam", "sgu_w", "sgu_b")]


def _seq_scratch(seq_len, tc, d_rg):
    head = d_rg // N_RG_HEADS
    return [
        pltpu.VMEM((seq_len, d_rg), F32),
        pltpu.VMEM((1, d_rg), F32),
        pltpu.VMEM((N_RG_HEADS, tc + (CONV_W - 1) * N_SEG, head), F32),
        pltpu.VMEM((N_RG_HEADS, tc, head), F32),
    ]


def _inproj_seq_kernel(*refs):
    x_ref, gy_ref, u_ref, v_ref, h0_ref = refs[5:10]
    weights = refs[10:17]
    mix_ref, st_ref = refs[21:23]
    hf_scr, carry_scr, xi_scr, hs_scr = refs[23:27]
    tc = x_ref.shape[0]

    def set_state(dirn):
        def run():
            carry_scr[...] = h0_ref[dirn:dirn + 1, :]
        return run

    def keep_state(dirn):
        def run():
            st_ref[dirn:dirn + 1, :] = carry_scr[...]
        return run

    def then(first, item):
        def run():
            first()
            item()
        return run

    items = []
    for reverse in (False, True):
        dirn = 1 if reverse else 0
        heads = _seq_items(reverse, None, slice(0, tc), x_ref, gy_ref, u_ref, v_ref, *weights,
                           mix_ref, hf_scr, carry_scr, xi_scr, hs_scr)
        heads[0] = then(set_state(dirn), heads[0])
        heads[-1] = then(heads[-1], keep_state(dirn))
        items += heads
    _inproj_body(*refs[:5], *refs[17:21], between=items)


def _inproj_seq(x, mod, grp, norm_g, w_in, seq, sp):
    m, d = x.shape
    in_specs, out_specs, out_shape = _inproj_specs(m, d, w_in.shape[1], grp)
    args = [x, norm_g.reshape(1, d), mod, mod, w_in]
    if seq is None:
        kern = _inproj_body
        scratch = []
    else:
        xrg, gy, u, v, h0, seq_len = seq
        ms, d_rg = xrg.shape
        assert ms // seq_len == m // ROW_TILE
        blk = lambda i: (i, 0)
        per_seq = pl.BlockSpec((None, 2, d_rg), lambda i: (i, 0, 0))
        in_specs = in_specs + [pl.BlockSpec((seq_len, d_rg), blk)] * 4 + [per_seq] + _seq_weight_specs(sp)
        args = args + [xrg, gy, u, v, h0] + _seq_weights(sp)
        out_specs = out_specs + [pl.BlockSpec((seq_len, 2 * d_rg), blk), per_seq]
        out_shape = out_shape + [jax.ShapeDtypeStruct((ms, 2 * d_rg), BF16),
                                 jax.ShapeDtypeStruct((ms // seq_len, 2, d_rg), F32)]
        kern = _inproj_seq_kernel
        scratch = _seq_scratch(seq_len, seq_len, d_rg)
    return pl.pallas_call(
        kern,
        grid=(m // ROW_TILE,),
        in_specs=in_specs,
        out_specs=out_specs,
        out_shape=out_shape,
        scratch_shapes=scratch,
        compiler_params=_params(dimension_semantics=("arbitrary",)),
        name="in_projection" if seq is None else "in_projection_seqmix",
    )(*args)


def _ffn_first(x_ref, mix_ref, g1_ref, sc_ref, sh_ref, ng_ref, wo_ref, o_ref, h2_scr, acc_scr):
    mix = jnp.dot(mix_ref[...], wo_ref[...], preferred_element_type=F32)
    x1 = x_ref[...] + g1_ref[...] * mix
    o_ref[...] = x1
    h2 = (_rms(x1) * ng_ref[...]) * (1.0 + sc_ref[...]) + sh_ref[...]
    h2_scr[...] = h2.astype(BF16)
    acc_scr[...] = jnp.zeros_like(acc_scr)


def _ffn_step(w1_ref, w2_ref, h2_scr, acc_scr, between=(), assign=False):
    between = list(between)
    n_w1 = w1_ref.shape[1] // FF_PIECE
    n_pieces = n_w1 + 2
    per = -(-len(between) // n_pieces)
    half_d = w2_ref.shape[1] // 2

    def gap():
        for _ in range(min(per, len(between))):
            between.pop(0)()

    h2 = h2_scr[...]
    acts = []
    for n in range(n_w1):
        gap()
        a = jnp.dot(h2, w1_ref[:, n * FF_PIECE:(n + 1) * FF_PIECE], preferred_element_type=F32)
        acts.append(jnp.square(jnp.maximum(a, 0.0)).astype(BF16))
    act = jnp.concatenate(acts, axis=1)
    for n in range(2):
        gap()
        cols = slice(n * half_d, (n + 1) * half_d)
        part = jnp.dot(act, w2_ref[:, cols], preferred_element_type=F32)
        acc_scr[:, cols] = part if assign else acc_scr[:, cols] + part
    for item in between:
        item()


def _ffn_last(g2_ref, fg_ref, o_ref, acc_scr, final_norm):
    x2 = o_ref[...] + g2_ref[...] * acc_scr[...]
    if final_norm:
        x2 = _rms(x2) * fg_ref[...]
    o_ref[...] = x2


def _ffn_pipe_kernel(x_ref, mix_ref, g1_ref, sc_ref, sh_ref, g2_ref, ng_ref, fg_ref,
                     wo_ref, w1_ref, w2_ref, o_ref, h2_scr, x1_scr, acc_scr, *, n_tiles, final_norm):
    t = pl.program_id(0)
    steps = (pl.num_programs(0) - 1) // n_tiles
    i = t // steps
    j = t % steps
    h2_cur = h2_scr.at[i % 2]
    h2_next = h2_scr.at[(i + 1) % 2]

    def first(h2_dst):
        mix = jnp.dot(mix_ref[...], wo_ref[...], preferred_element_type=F32)
        x1 = x_ref[...] + g1_ref[...] * mix
        x1_scr[...] = x1
        h2 = (_rms(x1) * ng_ref[...]) * (1.0 + sc_ref[...]) + sh_ref[...]
        h2_dst[...] = h2.astype(BF16)

    def last():
        _ffn_last(g2_ref, fg_ref, o_ref, acc_scr, final_norm)

    def keep_x1():
        o_ref[...] = x1_scr[...]

    @pl.when(t == 0)
    def _():
        first(h2_cur)
        _ffn_step(w1_ref, w2_ref, h2_cur, acc_scr, assign=True)

    @pl.when(jnp.logical_and(j == 0, jnp.logical_and(i > 0, i < n_tiles)))
    def _():
        last()
        _ffn_step(w1_ref, w2_ref, h2_cur, acc_scr, assign=True)

    @pl.when(t == n_tiles * steps)
    def _():
        last()

    @pl.when(j == 1)
    def _():
        _ffn_step(w1_ref, w2_ref, h2_cur, acc_scr, between=[keep_x1])

    @pl.when(jnp.logical_and(j > 1, j < steps - 1))
    def _():
        _ffn_step(w1_ref, w2_ref, h2_cur, acc_scr)

    @pl.when(jnp.logical_and(j == steps - 1, i < n_tiles))
    def _():
        first(h2_next)
        _ffn_step(w1_ref, w2_ref, h2_cur, acc_scr)


def _ffn_seq_kernel(x_ref, mix_ref, g1_ref, sc_ref, sh_ref, g2_ref, ng_ref, fg_ref,
                    wo_ref, w1_ref, w2_ref,
                    sx_ref, sprev_ref, snext_ref, gy_ref, u_ref, v_ref, h0_ref,
                    cw_ref, cb_ref, gw_ref, gb_ref, lam_ref, sw_ref, sb_ref,
                    o_ref, smix_ref,
                    h2_scr, acc_scr, hf_scr, carry_scr, xi_scr, hs_scr, *, final_norm):
    i = pl.program_id(0)
    j = pl.program_id(1)
    n_chunks = pl.num_programs(1)
    tc = sx_ref.shape[0]
    backward = i % 2 == 1
    jj = jnp.where(backward, n_chunks - 1 - j, j)
    hf_rows = pl.ds(pl.multiple_of(jj * tc, tc), tc)
    edges = (jj == 0, sprev_ref, jj == n_chunks - 1, snext_ref)

    @pl.when(j == 0)
    def _():
        _ffn_first(x_ref, mix_ref, g1_ref, sc_ref, sh_ref, ng_ref, wo_ref, o_ref, h2_scr, acc_scr)
        carry_scr[...] = h0_ref[pl.ds(i % 2, 1), :]

    def both(reverse):
        items = _seq_items(reverse, edges, hf_rows, sx_ref, gy_ref, u_ref, v_ref,
                           cw_ref, cb_ref, gw_ref, gb_ref, lam_ref, sw_ref, sb_ref,
                           smix_ref, hf_scr, carry_scr, xi_scr, hs_scr)
        _ffn_step(w1_ref, w2_ref, h2_scr, acc_scr, between=items)

    @pl.when(jnp.logical_not(backward))
    def _():
        both(False)

    @pl.when(backward)
    def _():
        both(True)

    @pl.when(j == n_chunks - 1)
    def _():
        _ffn_last(g2_ref, fg_ref, o_ref, acc_scr, final_norm)


def _ffn_pipelined(x, mix, mod, grp, norm_g, final_g, w_out, w_ff1, w_ff2, final_norm, ff_tile):
    m, d = x.shape
    d_mix = mix.shape[1]
    steps = w_ff1.shape[1] // ff_tile
    n = m // ROW_TILE
    head_tile = lambda t: jnp.minimum((t + 1) // steps, n - 1)
    tail_tile = lambda t: jnp.clip((t - 1) // steps, 0, n - 1)
    ff_step = lambda t: jnp.where(t == n * steps, steps - 1, t % steps)
    vec = pl.BlockSpec((1, d), lambda t: (0, 0))
    return pl.pallas_call(
        functools.partial(_ffn_pipe_kernel, n_tiles=n, final_norm=final_norm),
        grid=(n * steps + 1,),
        in_specs=[
            pl.BlockSpec((ROW_TILE, d), lambda t: (head_tile(t), 0)),
            pl.BlockSpec((ROW_TILE, d_mix), lambda t: (head_tile(t), 0)),
            _mod_spec(2, grp, d, head_tile),
            _mod_spec(4, grp, d, head_tile),
            _mod_spec(3, grp, d, head_tile),
            _mod_spec(5, grp, d, tail_tile),
            vec,
            vec,
            pl.BlockSpec((d_mix, d), lambda t: (0, 0), pipeline_mode=pl.Buffered(1)),
            pl.BlockSpec((d, ff_tile), lambda t: (0, ff_step(t))),
            pl.BlockSpec((ff_tile, d), lambda t: (ff_step(t), 0)),
        ],
        out_specs=pl.BlockSpec((ROW_TILE, d), lambda t: (tail_tile(t), 0)),
        out_shape=jax.ShapeDtypeStruct((m, d), F32),
        scratch_shapes=[
            pltpu.VMEM((2, ROW_TILE, d), BF16),
            pltpu.VMEM((ROW_TILE, d), F32),
            pltpu.VMEM((ROW_TILE, d), F32),
        ],
        compiler_params=_params(dimension_semantics=("arbitrary",)),
        name="out_projection_ffn",
    )(x, mix, mod, mod, mod, mod, norm_g.reshape(1, d), final_g.reshape(1, d), w_out, w_ff1, w_ff2)


def _ffn_seq(x, mix, mod, grp, norm_g, final_g, w_out, w_ff1, w_ff2, final_norm, ff_tile, seq, sp):
    m, d = x.shape
    d_mix = mix.shape[1]
    n_steps = w_ff1.shape[1] // ff_tile
    row = lambda i, j: (i, 0)
    vec = pl.BlockSpec((1, d), lambda i, j: (0, 0))
    xrg, gy, u, v, h0, seq_len = seq
    ms, d_rg = xrg.shape
    n_seq = ms // seq_len
    tc = seq_len // n_steps
    assert 2 * n_seq == m // ROW_TILE and tc % (N_SEG * SUBLANES) == 0 and tc % CHUNK == 0
    per_tile = tc // HALO
    n_halo = ms // HALO

    def chunk(i, j):
        return (i // 2) * n_steps + jnp.where(i % 2 == 1, n_steps - 1 - j, j)

    def out_chunk(i, j):
        return ((i // 2) * n_steps + jnp.where(i % 2 == 1, n_steps - 1 - j, n_steps - 1), 0)

    blk = lambda rows, cols, index: pl.BlockSpec((rows, cols), index)
    return pl.pallas_call(
        functools.partial(_ffn_seq_kernel, final_norm=final_norm),
        grid=(m // ROW_TILE, n_steps),
        in_specs=[
            pl.BlockSpec((ROW_TILE, d), row),
            pl.BlockSpec((ROW_TILE, d_mix), row),
            _mod_spec(2, grp, d),
            _mod_spec(4, grp, d),
            _mod_spec(3, grp, d),
            _mod_spec(5, grp, d),
            vec,
            vec,
            pl.BlockSpec((d_mix, d), lambda i, j: (0, 0), pipeline_mode=pl.Buffered(1)),
            pl.BlockSpec((d, ff_tile), lambda i, j: (0, j)),
            pl.BlockSpec((ff_tile, d), lambda i, j: (j, 0)),
            blk(tc, d_rg, lambda i, j: (chunk(i, j), 0)),
            blk(HALO, d_rg, lambda i, j: (jnp.maximum(chunk(i, j) * per_tile - 1, 0), 0)),
            blk(HALO, d_rg, lambda i, j: (jnp.minimum((chunk(i, j) + 1) * per_tile, n_halo - 1), 0)),
            blk(tc, d_rg, out_chunk),
            blk(tc, d_rg, out_chunk),
            blk(tc, d_rg, out_chunk),
            pl.BlockSpec((None, 2, d_rg), lambda i, j: (i // 2, 0, 0)),
        ] + _seq_weight_specs(sp),
        out_specs=[pl.BlockSpec((ROW_TILE, d), row), blk(tc, 2 * d_rg, out_chunk)],
        out_shape=[jax.ShapeDtypeStruct((m, d), F32), jax.ShapeDtypeStruct((ms, 2 * d_rg), BF16)],
        scratch_shapes=[
            pltpu.VMEM((ROW_TILE, d), BF16),
            pltpu.VMEM((ROW_TILE, d), F32),
        ] + _seq_scratch(seq_len, tc, d_rg),
        compiler_params=_params(dimension_semantics=("arbitrary", "arbitrary")),
        name="out_projection_ffn_seqmix",
    )(x, mix, mod, mod, mod, mod, norm_g.reshape(1, d), final_g.reshape(1, d), w_out, w_ff1, w_ff2,
      xrg, xrg, xrg, gy, u, v, h0, *_seq_weights(sp))


def kernel(x_prompt, x_sample, c, state_rglru, c_ctx, norm1_g, w_ada, b_ada, w_in, conv_w, conv_b,
           ga_w, ga_b, gi_w, gi_b, lru_lambda, sgu_w, sgu_b, w_out, norm2_g, w_ff1, w_ff2, final_g):
    b_ctx, l_ctx, d = x_prompt.shape
    b_lat, l_lat, _ = x_sample.shape
    depth = w_in.shape[0]
    d_rg = lru_lambda.shape[-1]
    assert b_lat + 1 <= MOD_ROWS
    assert (b_ctx * l_ctx) % ROW_TILE == 0 and l_lat % ROW_TILE == 0

    cond = jnp.concatenate(
        [c, c_ctx[None, :], jnp.zeros((MOD_ROWS - b_lat - 1, d), F32)], axis=0)
    grp_ctx = (b_lat, b_ctx * l_ctx)
    grp_lat = (0, l_lat)
    xp = x_prompt.reshape(b_ctx * l_ctx, d)
    xs = x_sample.reshape(b_lat * l_lat, d)
    h0_ctx = jnp.zeros((b_ctx, 2, d_rg), F32)
    ctx_states = []
    for l in range(depth):
        sp = {
            "conv_w": conv_w[l], "conv_b": conv_b[l].reshape(1, d_rg),
            "gw": jnp.concatenate([ga_w[l], gi_w[l]], axis=-1).astype(BF16),
            "gb": jnp.stack([ga_b[l], gi_b[l]], axis=1),
            "lam": lru_lambda[l],
            "sgu_w": sgu_w[l].astype(BF16), "sgu_b": sgu_b[l][:, :, None],
        }
        win, wout = w_in[l].astype(BF16), w_out[l].astype(BF16)
        wff1, wff2 = w_ff1[l].astype(BF16), w_ff2[l].astype(BF16)
        last = l == depth - 1
        mod = _modulation(cond, w_ada[l], b_ada[l])
        h0_lat = state_rglru[:, l].astype(F32)

        gy_c, xrg_c, u_c, v_c = _inproj_seq(xp, mod, grp_ctx, norm1_g[l], win, None, None)
        gy_l, xrg_l, u_l, v_l, mix_c, st = _inproj_seq(
            xs, mod, grp_lat, norm1_g[l], win, (xrg_c, gy_c, u_c, v_c, h0_ctx, l_ctx), sp)
        xp, mix_l = _ffn_seq(xp, mix_c, mod, grp_ctx, norm2_g[l], final_g, wout, wff1, wff2, last,
                             FF_TILE_FUSED, (xrg_l, gy_l, u_l, v_l, h0_lat, l_lat), sp)
        xs = _ffn_pipelined(xs, mix_l, mod, grp_lat, norm2_g[l], final_g, wout, wff1, wff2, last,
                            FF_TILE_ALONE)
        ctx_states.append(st)
    y_prompt = xp.reshape(b_ctx, l_ctx, d)
    y_sample = xs.reshape(b_lat, l_lat, d)
    new_state = jnp.stack(ctx_states, axis=1).astype(x_prompt.dtype)
    return (y_prompt, y_sample, new_state)
```

```python
import functools

import jax
import jax.numpy as jnp
from jax import lax
from jax.experimental import pallas as pl
from jax.experimental.pallas import tpu as pltpu

F32 = jnp.float32
BF16 = jnp.bfloat16

N_RG_HEADS = 8
N_CH_HEADS = 8
CHUNK = 128
CONV_W = 4
CONV_LEFT = 2
RG_C = 8.0
N_MOD = 6
EPS = 1e-6

SUBLANES = 8
MOD_ROWS = 8
VMEM_LIMIT_BYTES = 62 * 1024 * 1024

ROW_TILE = 512
FF_TILE_FUSED = 512
FF_TILE_ALONE = 1024
FF_PIECE = 256
W1_TILE = 512
CAST_TILES = 2
MOD_COL_TILE = 1024
PROJ_PIECE = 256
N_SEG = SUBLANES
HALO = SUBLANES


def _rms(x):
    return x * lax.rsqrt(jnp.mean(x * x, axis=-1, keepdims=True) + EPS)


def _params(**kw):
    return pltpu.CompilerParams(vmem_limit_bytes=VMEM_LIMIT_BYTES, **kw)


def _cast_tiles_kernel(w_ref, o_ref):
    for n in range(o_ref.shape[0]):
        o_ref[n] = w_ref[:, n * W1_TILE:(n + 1) * W1_TILE].astype(o_ref.dtype)


def _cast_col_tiles(w):
    d, n = w.shape
    return pl.pallas_call(
        _cast_tiles_kernel,
        grid=(n // (CAST_TILES * W1_TILE),),
        in_specs=[pl.BlockSpec((d, CAST_TILES * W1_TILE), lambda j: (0, j))],
        out_specs=pl.BlockSpec((CAST_TILES, d, W1_TILE), lambda j: (j, 0, 0)),
        out_shape=jax.ShapeDtypeStruct((n // W1_TILE, d, W1_TILE), BF16),
        compiler_params=_params(dimension_semantics=("arbitrary",)),
        name="cast_col_tiles",
    )(w)


def _mod_kernel(c_ref, w_ref, b_ref, o_ref):
    s = jax.nn.silu(c_ref[...]).astype(BF16)
    o_ref[...] = jnp.dot(s, w_ref[...].astype(BF16), preferred_element_type=F32) + b_ref[...]


def _modulation(cond, w_ada, b_ada):
    d = cond.shape[1]
    n = w_ada.shape[1]
    out = pl.pallas_call(
        _mod_kernel,
        grid=(n // MOD_COL_TILE,),
        in_specs=[
            pl.BlockSpec((MOD_ROWS, d), lambda j: (0, 0)),
            pl.BlockSpec((d, MOD_COL_TILE), lambda j: (0, j)),
            pl.BlockSpec((1, MOD_COL_TILE), lambda j: (0, j)),
        ],
        out_specs=pl.BlockSpec((MOD_ROWS, MOD_COL_TILE), lambda j: (0, j)),
        out_shape=jax.ShapeDtypeStruct((MOD_ROWS, n), F32),
        compiler_params=_params(dimension_semantics=("arbitrary",)),
        name="modulation",
    )(cond, w_ada, b_ada.reshape(1, n))
    return out.reshape(MOD_ROWS, N_MOD, 1, d)


def _mod_spec(which, grp, d, tile=lambda i, *_: i):
    mod_base, rows_per_mod = grp

    def index(*ids):
        return (mod_base + (tile(*ids) * ROW_TILE) // rows_per_mod, which, 0, 0)
    return pl.BlockSpec((None, None, 1, d), index)


def _inproj_body(x_ref, g_ref, sc_ref, sh_ref, w_ref, gy_ref, xrg_ref, u_ref, v_ref, between=()):
    d_rg = gy_ref.shape[1]
    h = (_rms(x_ref[...]) * g_ref[...]) * (1.0 + sc_ref[...]) + sh_ref[...]
    hb = h.astype(BF16)
    outs = (gy_ref, xrg_ref, u_ref, v_ref)
    between = list(between)
    for n in range(w_ref.shape[1] // PROJ_PIECE):
        part = jnp.dot(hb, w_ref[:, n * PROJ_PIECE:(n + 1) * PROJ_PIECE], preferred_element_type=F32)
        dest, col = divmod(n * PROJ_PIECE, d_rg)
        if dest != 1:
            part = jax.nn.gelu(part)
        outs[dest][:, col:col + PROJ_PIECE] = part.astype(outs[dest].dtype)
        if between:
            between.pop(0)()
    for item in between:
        item()


def _inproj_specs(m, d, d_in, grp):
    d_rg = d_in // 4
    row = lambda i, *_: (i, 0)
    in_specs = [
        pl.BlockSpec((ROW_TILE, d), row),
        pl.BlockSpec((1, d), lambda i, *_: (0, 0)),
        _mod_spec(1, grp, d),
        _mod_spec(0, grp, d),
        pl.BlockSpec((d, d_in), lambda i, *_: (0, 0), pipeline_mode=pl.Buffered(1)),
    ]
    out_specs = [pl.BlockSpec((ROW_TILE, d_rg), row)] * 4
    out_shape = [
        jax.ShapeDtypeStruct((m, d_rg), F32),
        jax.ShapeDtypeStruct((m, d_rg), F32),
        jax.ShapeDtypeStruct((m, d_rg), F32),
        jax.ShapeDtypeStruct((m, d_rg), BF16),
    ]
    return in_specs, out_specs, out_shape


def _scan_head(hd, reverse, edges, hf_rows, x_ref, gy_ref, cw_ref, cb_ref, gw_ref, gb_ref, lam_ref,
               mix_ref, hf_scr, carry_scr, xi_scr, hs_scr):
    tc, d_rg = x_ref.shape
    tl = tc // N_SEG
    head = d_rg // N_RG_HEADS
    lead = CONV_LEFT * N_SEG
    dirn = 1 if reverse else 0
    steps = range(tl - 1, -1, -1) if reverse else range(tl)
    segs = range(N_SEG - 1, -1, -1) if reverse else range(N_SEG)
    sub = lax.broadcasted_iota(jnp.int32, (N_SEG, head), 0)
    cols = slice(hd * head, (hd + 1) * head)
    xi = xi_scr.at[hd]
    for sg in range(N_SEG):
        xi[pl.ds(lead + sg, tl, stride=N_SEG), :] = x_ref[sg * tl:(sg + 1) * tl, cols]

    def step_rows(t):
        return xi[lead + t * N_SEG:lead + (t + 1) * N_SEG, :]

    for k in range(CONV_LEFT):
        if edges is None:
            edge = 0.0
        else:
            row = HALO - CONV_LEFT + k
            edge = jnp.where(edges[0], 0.0, edges[1][row:row + 1, cols])
        prev_step = pltpu.roll(step_rows(tl - CONV_LEFT + k), 1, axis=0)
        xi[k * N_SEG:(k + 1) * N_SEG, :] = jnp.where(sub == 0, edge, prev_step)
    for k in range(CONV_W - 1 - CONV_LEFT):
        edge = 0.0 if edges is None else jnp.where(edges[2], 0.0, edges[3][k:k + 1, cols])
        next_step = pltpu.roll(step_rows(k), N_SEG - 1, axis=0)
        xi[lead + (tl + k) * N_SEG:lead + (tl + k + 1) * N_SEG, :] = jnp.where(
            sub == N_SEG - 1, edge, next_step)

    xc = cb_ref[:, cols]
    for k in range(CONV_W):
        xc = xc + cw_ref[k:k + 1, cols] * xi[k * N_SEG:k * N_SEG + tc, :]
    g = jnp.dot(xc.astype(BF16), gw_ref[dirn, hd], preferred_element_type=F32)
    r = jax.nn.sigmoid(g[:, :head] + gb_ref[dirn, 0:1, cols])
    i = jax.nn.sigmoid(g[:, head:] + gb_ref[dirn, 1:2, cols])
    log_a = r * (-RG_C * jax.nn.softplus(-lam_ref[dirn:dirn + 1, cols]))
    a = jnp.exp(log_a)
    m2 = jnp.maximum(-jnp.tanh(log_a) * (a * a + 1.0), 0.0)
    mult = jnp.where(m2 > 0.0, m2 * lax.rsqrt(m2), 0.0)
    bx = mult * (i * xc)

    a3 = a.reshape(tl, N_SEG, head)
    b3 = bx.reshape(tl, N_SEG, head)
    hloc = [None] * tl
    prod = [None] * tl
    hprev = pprev = None
    for t in steps:
        if hprev is None:
            hprev, pprev = b3[t], a3[t]
        else:
            hprev, pprev = a3[t] * hprev + b3[t], a3[t] * pprev
        hloc[t], prod[t] = hprev, pprev
    state = carry_scr[:, cols]
    init = jnp.zeros((N_SEG, head), F32)
    for sg in segs:
        init = jnp.where(sub == sg, state, init)
        state = hprev[sg:sg + 1, :] + pprev[sg:sg + 1, :] * state
    carry_scr[:, cols] = state
    hfull = jnp.concatenate([hloc[t] + prod[t] * init for t in range(tl)], axis=0)
    if not reverse:
        hf_scr[hf_rows, cols] = hfull
    else:
        hs = hs_scr.at[hd]
        hs[...] = hf_scr[hf_rows, cols] + hfull
        for sg in range(N_SEG):
            rows = slice(sg * tl, (sg + 1) * tl)
            hsum = hs[pl.ds(sg, tl, stride=N_SEG), :]
            mix_ref[rows, cols] = (gy_ref[rows, cols] * hsum).astype(BF16)


def _sgu_head(hd, u_ref, v_ref, sw_ref, sb_ref, mix_ref):
    tc, d_ch = u_ref.shape
    ch_head = d_ch // N_CH_HEADS
    d_rg = mix_ref.shape[1] - d_ch
    cols = slice(hd * ch_head, (hd + 1) * ch_head)
    for n in range(tc // CHUNK):
        rows = slice(n * CHUNK, (n + 1) * CHUNK)
        mixed = jnp.dot(sw_ref[hd], v_ref[rows, cols], preferred_element_type=F32) + sb_ref[hd]
        mix_ref[rows, d_rg + hd * ch_head:d_rg + (hd + 1) * ch_head] = (
            u_ref[rows, cols] * mixed).astype(BF16)


def _seq_items(reverse, edges, hf_rows, x_ref, gy_ref, u_ref, v_ref,
               cw_ref, cb_ref, gw_ref, gb_ref, lam_ref, sw_ref, sb_ref,
               mix_ref, hf_scr, carry_scr, xi_scr, hs_scr):
    def item(hd):
        def run():
            _scan_head(hd, reverse, edges, hf_rows, x_ref, gy_ref, cw_ref, cb_ref, gw_ref, gb_ref,
                       lam_ref, mix_ref, hf_scr, carry_scr, xi_scr, hs_scr)
            if reverse:
                _sgu_head(hd, u_ref, v_ref, sw_ref, sb_ref, mix_ref)
        return run
    return [item(hd) for hd in range(N_RG_HEADS)]


def _seq_weight_specs(sp):
    full = lambda a: pl.BlockSpec(a.shape, lambda *_: (0,) * a.ndim, pipeline_mode=pl.Buffered(1))
    return [full(sp[k]) for k in ("conv_w", "conv_b", "gw", "gb", "lam", "sgu_w", "sgu_b")]


def _seq_weights(sp):
    return [sp[k] for k in ("conv_w", "conv_b", "gw", "gb", "lam", "sgu_w", "sgu_b")]


def _seq_scratch(seq_len, tc, d_rg):
    head = d_rg // N_RG_HEADS
    return [
        pltpu.VMEM((seq_len, d_rg), F32),
        pltpu.VMEM((1, d_rg), F32),
        pltpu.VMEM((N_RG_HEADS, tc + (CONV_W - 1) * N_SEG, head), F32),
        pltpu.VMEM((N_RG_HEADS, tc, head), F32),
    ]


def _inproj_seq_kernel(*refs):
    x_ref, gy_ref, u_ref, v_ref, h0_ref = refs[5:10]
    weights = refs[10:17]
    mix_ref, st_ref = refs[21:23]
    hf_scr, carry_scr, xi_scr, hs_scr = refs[23:27]
    tc = x_ref.shape[0]

    def set_state(dirn):
        def run():
            carry_scr[...] = h0_ref[dirn:dirn + 1, :]
        return run

    def keep_state(dirn):
        def run():
            st_ref[dirn:dirn + 1, :] = carry_scr[...]
        return run

    def then(first, item):
        def run():
            first()
            item()
        return run

    items = []
    for reverse in (False, True):
        dirn = 1 if reverse else 0
        heads = _seq_items(reverse, None, slice(0, tc), x_ref, gy_ref, u_ref, v_ref, *weights,
                           mix_ref, hf_scr, carry_scr, xi_scr, hs_scr)
        heads[0] = then(set_state(dirn), heads[0])
        heads[-1] = then(heads[-1], keep_state(dirn))
        items += heads
    _inproj_body(*refs[:5], *refs[17:21], between=items)


def _inproj_seq(x, mod, grp, norm_g, w_in, seq, sp):
    m, d = x.shape
    in_specs, out_specs, out_shape = _inproj_specs(m, d, w_in.shape[1], grp)
    args = [x, norm_g.reshape(1, d), mod, mod, w_in]
    if seq is None:
        kern = _inproj_body
        scratch = []
    else:
        xrg, gy, u, v, h0, seq_len = seq
        ms, d_rg = xrg.shape
        assert ms // seq_len == m // ROW_TILE
        blk = lambda i: (i, 0)
        per_seq = pl.BlockSpec((None, 2, d_rg), lambda i: (i, 0, 0))
        in_specs = in_specs + [pl.BlockSpec((seq_len, d_rg), blk)] * 4 + [per_seq] + _seq_weight_specs(sp)
        args = args + [xrg, gy, u, v, h0] + _seq_weights(sp)
        out_specs = out_specs + [pl.BlockSpec((seq_len, 2 * d_rg), blk), per_seq]
        out_shape = out_shape + [jax.ShapeDtypeStruct((ms, 2 * d_rg), BF16),
                                 jax.ShapeDtypeStruct((ms // seq_len, 2, d_rg), F32)]
        kern = _inproj_seq_kernel
        scratch = _seq_scratch(seq_len, seq_len, d_rg)
    return pl.pallas_call(
        kern,
        grid=(m // ROW_TILE,),
        in_specs=in_specs,
        out_specs=out_specs,
        out_shape=out_shape,
        scratch_shapes=scratch,
        compiler_params=_params(dimension_semantics=("arbitrary",)),
        name="in_projection" if seq is None else "in_projection_seqmix",
    )(*args)


def _ffn_first(x_ref, mix_ref, g1_ref, sc_ref, sh_ref, ng_ref, wo_ref, o_ref, h2_scr, acc_scr):
    mix = jnp.dot(mix_ref[...], wo_ref[...], preferred_element_type=F32)
    x1 = x_ref[...] + g1_ref[...] * mix
    o_ref[...] = x1
    h2 = (_rms(x1) * ng_ref[...]) * (1.0 + sc_ref[...]) + sh_ref[...]
    h2_scr[...] = h2.astype(BF16)
    acc_scr[...] = jnp.zeros_like(acc_scr)


def _ffn_step(w1_ref, w2_ref, h2_scr, acc_scr, between=(), assign=False):
    between = list(between)
    n_w1 = w1_ref.shape[0] * W1_TILE // FF_PIECE
    n_pieces = n_w1 + 2
    per = -(-len(between) // n_pieces)
    half_d = w2_ref.shape[1] // 2

    def gap():
        for _ in range(min(per, len(between))):
            between.pop(0)()

    h2 = h2_scr[...]
    acts = []
    for n in range(n_w1):
        gap()
        tile, col = divmod(n * FF_PIECE, W1_TILE)
        a = jnp.dot(h2, w1_ref[tile, :, col:col + FF_PIECE], preferred_element_type=F32)
        acts.append(jnp.square(jnp.maximum(a, 0.0)).astype(BF16))
    act = jnp.concatenate(acts, axis=1)
    for n in range(2):
        gap()
        cols = slice(n * half_d, (n + 1) * half_d)
        part = jnp.dot(act, w2_ref[:, cols], preferred_element_type=F32)
        acc_scr[:, cols] = part if assign else acc_scr[:, cols] + part
    for item in between:
        item()


def _ffn_last(g2_ref, fg_ref, o_ref, acc_scr, final_norm):
    x2 = o_ref[...] + g2_ref[...] * acc_scr[...]
    if final_norm:
        x2 = _rms(x2) * fg_ref[...]
    o_ref[...] = x2


def _ffn_pipe_kernel(x_ref, mix_ref, g1_ref, sc_ref, sh_ref, g2_ref, ng_ref, fg_ref,
                     wo_ref, w1_ref, w2_ref, o_ref, h2_scr, x1_scr, acc_scr, *, n_tiles, final_norm):
    t = pl.program_id(0)
    steps = (pl.num_programs(0) - 1) // n_tiles
    i = t // steps
    j = t % steps
    h2_cur = h2_scr.at[i % 2]
    h2_next = h2_scr.at[(i + 1) % 2]

    def first(h2_dst):
        mix = jnp.dot(mix_ref[...], wo_ref[...], preferred_element_type=F32)
        x1 = x_ref[...] + g1_ref[...] * mix
        x1_scr[...] = x1
        h2 = (_rms(x1) * ng_ref[...]) * (1.0 + sc_ref[...]) + sh_ref[...]
        h2_dst[...] = h2.astype(BF16)

    def last():
        _ffn_last(g2_ref, fg_ref, o_ref, acc_scr, final_norm)

    def keep_x1():
        o_ref[...] = x1_scr[...]

    @pl.when(t == 0)
    def _():
        first(h2_cur)
        _ffn_step(w1_ref, w2_ref, h2_cur, acc_scr, assign=True)

    @pl.when(jnp.logical_and(j == 0, jnp.logical_and(i > 0, i < n_tiles)))
    def _():
        last()
        _ffn_step(w1_ref, w2_ref, h2_cur, acc_scr, assign=True)

    @pl.when(t == n_tiles * steps)
    def _():
        last()

    @pl.when(j == 1)
    def _():
        _ffn_step(w1_ref, w2_ref, h2_cur, acc_scr, between=[keep_x1])

    @pl.when(jnp.logical_and(j > 1, j < steps - 1))
    def _():
        _ffn_step(w1_ref, w2_ref, h2_cur, acc_scr)

    @pl.when(jnp.logical_and(j == steps - 1, i < n_tiles))
    def _():
        first(h2_next)
        _ffn_step(w1_ref, w2_ref, h2_cur, acc_scr)


def _ffn_seq_kernel(x_ref, mix_ref, g1_ref, sc_ref, sh_ref, g2_ref, ng_ref, fg_ref,
                    wo_ref, w1_ref, w2_ref,
                    sx_ref, sprev_ref, snext_ref, gy_ref, u_ref, v_ref, h0_ref,
                    cw_ref, cb_ref, gw_ref, gb_ref, lam_ref, sw_ref, sb_ref,
                    o_ref, smix_ref,
                    h2_scr, acc_scr, hf_scr, carry_scr, xi_scr, hs_scr, *, final_norm):
    i = pl.program_id(0)
    j = pl.program_id(1)
    n_chunks = pl.num_programs(1)
    tc = sx_ref.shape[0]
    backward = i % 2 == 1
    jj = jnp.where(backward, n_chunks - 1 - j, j)
    hf_rows = pl.ds(pl.multiple_of(jj * tc, tc), tc)
    edges = (jj == 0, sprev_ref, jj == n_chunks - 1, snext_ref)

    @pl.when(j == 0)
    def _():
        _ffn_first(x_ref, mix_ref, g1_ref, sc_ref, sh_ref, ng_ref, wo_ref, o_ref, h2_scr, acc_scr)
        carry_scr[...] = h0_ref[pl.ds(i % 2, 1), :]

    def both(reverse):
        items = _seq_items(reverse, edges, hf_rows, sx_ref, gy_ref, u_ref, v_ref,
                           cw_ref, cb_ref, gw_ref, gb_ref, lam_ref, sw_ref, sb_ref,
                           smix_ref, hf_scr, carry_scr, xi_scr, hs_scr)
        _ffn_step(w1_ref, w2_ref, h2_scr, acc_scr, between=items)

    @pl.when(jnp.logical_not(backward))
    def _():
        both(False)

    @pl.when(backward)
    def _():
        both(True)

    @pl.when(j == n_chunks - 1)
    def _():
        _ffn_last(g2_ref, fg_ref, o_ref, acc_scr, final_norm)


def _ffn_pipelined(x, mix, mod, grp, norm_g, final_g, w_out, w_ff1, w_ff2, final_norm, ff_tile):
    m, d = x.shape
    d_mix = mix.shape[1]
    steps = w_ff2.shape[0] // ff_tile
    n = m // ROW_TILE
    head_tile = lambda t: jnp.minimum((t + 1) // steps, n - 1)
    tail_tile = lambda t: jnp.clip((t - 1) // steps, 0, n - 1)
    ff_step = lambda t: jnp.where(t == n * steps, steps - 1, t % steps)
    vec = pl.BlockSpec((1, d), lambda t: (0, 0))
    return pl.pallas_call(
        functools.partial(_ffn_pipe_kernel, n_tiles=n, final_norm=final_norm),
        grid=(n * steps + 1,),
        in_specs=[
            pl.BlockSpec((ROW_TILE, d), lambda t: (head_tile(t), 0)),
            pl.BlockSpec((ROW_TILE, d_mix), lambda t: (head_tile(t), 0)),
            _mod_spec(2, grp, d, head_tile),
            _mod_spec(4, grp, d, head_tile),
            _mod_spec(3, grp, d, head_tile),
            _mod_spec(5, grp, d, tail_tile),
            vec,
            vec,
            pl.BlockSpec((d_mix, d), lambda t: (0, 0), pipeline_mode=pl.Buffered(1)),
            pl.BlockSpec((ff_tile // W1_TILE, d, W1_TILE), lambda t: (ff_step(t), 0, 0)),
            pl.BlockSpec((ff_tile, d), lambda t: (ff_step(t), 0)),
        ],
        out_specs=pl.BlockSpec((ROW_TILE, d), lambda t: (tail_tile(t), 0)),
        out_shape=jax.ShapeDtypeStruct((m, d), F32),
        scratch_shapes=[
            pltpu.VMEM((2, ROW_TILE, d), BF16),
            pltpu.VMEM((ROW_TILE, d), F32),
            pltpu.VMEM((ROW_TILE, d), F32),
        ],
        compiler_params=_params(dimension_semantics=("arbitrary",)),
        name="out_projection_ffn",
    )(x, mix, mod, mod, mod, mod, norm_g.reshape(1, d), final_g.reshape(1, d), w_out, w_ff1, w_ff2)


def _ffn_seq(x, mix, mod, grp, norm_g, final_g, w_out, w_ff1, w_ff2, final_norm, ff_tile, seq, sp):
    m, d = x.shape
    d_mix = mix.shape[1]
    n_steps = w_ff2.shape[0] // ff_tile
    row = lambda i, j: (i, 0)
    vec = pl.BlockSpec((1, d), lambda i, j: (0, 0))
    xrg, gy, u, v, h0, seq_len = seq
    ms, d_rg = xrg.shape
    n_seq = ms // seq_len
    tc = seq_len // n_steps
    assert 2 * n_seq == m // ROW_TILE and tc % (N_SEG * SUBLANES) == 0 and tc % CHUNK == 0
    per_tile = tc // HALO
    n_halo = ms // HALO

    def chunk(i, j):
        return (i // 2) * n_steps + jnp.where(i % 2 == 1, n_steps - 1 - j, j)

    def out_chunk(i, j):
        return ((i // 2) * n_steps + jnp.where(i % 2 == 1, n_steps - 1 - j, n_steps - 1), 0)

    blk = lambda rows, cols, index: pl.BlockSpec((rows, cols), index)
    return pl.pallas_call(
        functools.partial(_ffn_seq_kernel, final_norm=final_norm),
        grid=(m // ROW_TILE, n_steps),
        in_specs=[
            pl.BlockSpec((ROW_TILE, d), row),
            pl.BlockSpec((ROW_TILE, d_mix), row),
            _mod_spec(2, grp, d),
            _mod_spec(4, grp, d),
            _mod_spec(3, grp, d),
            _mod_spec(5, grp, d),
            vec,
            vec,
            pl.BlockSpec((d_mix, d), lambda i, j: (0, 0), pipeline_mode=pl.Buffered(1)),
            pl.BlockSpec((ff_tile // W1_TILE, d, W1_TILE), lambda i, j: (j, 0, 0)),
            pl.BlockSpec((ff_tile, d), lambda i, j: (j, 0)),
            blk(tc, d_rg, lambda i, j: (chunk(i, j), 0)),
            blk(HALO, d_rg, lambda i, j: (jnp.maximum(chunk(i, j) * per_tile - 1, 0), 0)),
            blk(HALO, d_rg, lambda i, j: (jnp.minimum((chunk(i, j) + 1) * per_tile, n_halo - 1), 0)),
            blk(tc, d_rg, out_chunk),
            blk(tc, d_rg, out_chunk),
            blk(tc, d_rg, out_chunk),
            pl.BlockSpec((None, 2, d_rg), lambda i, j: (i // 2, 0, 0)),
        ] + _seq_weight_specs(sp),
        out_specs=[pl.BlockSpec((ROW_TILE, d), row), blk(tc, 2 * d_rg, out_chunk)],
        out_shape=[jax.ShapeDtypeStruct((m, d), F32), jax.ShapeDtypeStruct((ms, 2 * d_rg), BF16)],
        scratch_shapes=[
            pltpu.VMEM((ROW_TILE, d), BF16),
            pltpu.VMEM((ROW_TILE, d), F32),
        ] + _seq_scratch(seq_len, tc, d_rg),
        compiler_params=_params(dimension_semantics=("arbitrary", "arbitrary")),
        name="out_projection_ffn_seqmix",
    )(x, mix, mod, mod, mod, mod, norm_g.reshape(1, d), final_g.reshape(1, d), w_out, w_ff1, w_ff2,
      xrg, xrg, xrg, gy, u, v, h0, *_seq_weights(sp))


def kernel(x_prompt, x_sample, c, state_rglru, c_ctx, norm1_g, w_ada, b_ada, w_in, conv_w, conv_b,
           ga_w, ga_b, gi_w, gi_b, lru_lambda, sgu_w, sgu_b, w_out, norm2_g, w_ff1, w_ff2, final_g):
    b_ctx, l_ctx, d = x_prompt.shape
    b_lat, l_lat, _ = x_sample.shape
    depth = w_in.shape[0]
    d_rg = lru_lambda.shape[-1]
    assert b_lat + 1 <= MOD_ROWS
    assert (b_ctx * l_ctx) % ROW_TILE == 0 and l_lat % ROW_TILE == 0

    cond = jnp.concatenate(
        [c, c_ctx[None, :], jnp.zeros((MOD_ROWS - b_lat - 1, d), F32)], axis=0)
    grp_ctx = (b_lat, b_ctx * l_ctx)
    grp_lat = (0, l_lat)
    xp = x_prompt.reshape(b_ctx * l_ctx, d)
    xs = x_sample.reshape(b_lat * l_lat, d)
    h0_ctx = jnp.zeros((b_ctx, 2, d_rg), F32)
    ctx_states = []
    for l in range(depth):
        sp = {
            "conv_w": conv_w[l], "conv_b": conv_b[l].reshape(1, d_rg),
            "gw": jnp.concatenate([ga_w[l], gi_w[l]], axis=-1).astype(BF16),
            "gb": jnp.stack([ga_b[l], gi_b[l]], axis=1),
            "lam": lru_lambda[l],
            "sgu_w": sgu_w[l].astype(BF16), "sgu_b": sgu_b[l][:, :, None],
        }
        win, wout = w_in[l].astype(BF16), w_out[l].astype(BF16)
        wff1, wff2 = _cast_col_tiles(w_ff1[l]), w_ff2[l].astype(BF16)
        last = l == depth - 1
        mod = _modulation(cond, w_ada[l], b_ada[l])
        h0_lat = state_rglru[:, l].astype(F32)

        gy_c, xrg_c, u_c, v_c = _inproj_seq(xp, mod, grp_ctx, norm1_g[l], win, None, None)
        gy_l, xrg_l, u_l, v_l, mix_c, st = _inproj_seq(
            xs, mod, grp_lat, norm1_g[l], win, (xrg_c, gy_c, u_c, v_c, h0_ctx, l_ctx), sp)
        xp, mix_l = _ffn_seq(xp, mix_c, mod, grp_ctx, norm2_g[l], final_g, wout, wff1, wff2, last,
                             FF_TILE_FUSED, (xrg_l, gy_l, u_l, v_l, h0_lat, l_lat), sp)
        xs = _ffn_pipelined(xs, mix_l, mod, grp_lat, norm2_g[l], final_g, wout, wff1, wff2, last,
                            FF_TILE_ALONE)
        ctx_states.append(st)
    y_prompt = xp.reshape(b_ctx, l_ctx, d)
    y_sample = xs.reshape(b_lat, l_lat, d)
    new_state = jnp.stack(ctx_states, axis=1).astype(x_prompt.dtype)
    return (y_prompt, y_sample, new_state)
```

```python
import functools

import jax
import jax.numpy as jnp
from jax import lax
from jax.experimental import pallas as pl
from jax.experimental.pallas import tpu as pltpu

F32 = jnp.float32
BF16 = jnp.bfloat16

N_RG_HEADS = 8
N_CH_HEADS = 8
CHUNK = 128
CONV_W = 4
CONV_LEFT = 2
RG_C = 8.0
N_MOD = 6
EPS = 1e-6

SUBLANES = 8
BF16_ROWS = 16
MOD_ROWS = 8
VMEM_LIMIT_BYTES = 62 * 1024 * 1024

ROW_TILE = 512
IN_ROW_TILE = 256
CAST_ITEM_BYTES = 1 << 19
FF_TILE_FUSED = 512
FF_TILE_ALONE = 1024
FF_PIECE = 256
MOD_COL_TILE = 1024
PROJ_PIECE = 256
N_SEG = SUBLANES
HALO = SUBLANES


def _rms(x):
    return x * lax.rsqrt(jnp.mean(x * x, axis=-1, keepdims=True) + EPS)


def _params(**kw):
    return pltpu.CompilerParams(vmem_limit_bytes=VMEM_LIMIT_BYTES, **kw)


def _mod_kernel(c_ref, w_ref, b_ref, o_ref):
    s = jax.nn.silu(c_ref[...]).astype(BF16)
    o_ref[...] = jnp.dot(s, w_ref[...].astype(BF16), preferred_element_type=F32) + b_ref[...]


def _modulation(cond, w_ada, b_ada):
    d = cond.shape[1]
    n = w_ada.shape[1]
    out = pl.pallas_call(
        _mod_kernel,
        grid=(n // MOD_COL_TILE,),
        in_specs=[
            pl.BlockSpec((MOD_ROWS, d), lambda j: (0, 0)),
            pl.BlockSpec((d, MOD_COL_TILE), lambda j: (0, j)),
            pl.BlockSpec((1, MOD_COL_TILE), lambda j: (0, j)),
        ],
        out_specs=pl.BlockSpec((MOD_ROWS, MOD_COL_TILE), lambda j: (0, j)),
        out_shape=jax.ShapeDtypeStruct((MOD_ROWS, n), F32),
        compiler_params=_params(dimension_semantics=("arbitrary",)),
        name="modulation",
    )(cond, w_ada, b_ada.reshape(1, n))
    return out.reshape(MOD_ROWS, N_MOD, 1, d)


def _mod_spec(which, grp, d, tile=lambda i, *_: i, row_tile=ROW_TILE):
    mod_base, rows_per_mod = grp

    def index(*ids):
        return (mod_base + (tile(*ids) * row_tile) // rows_per_mod, which, 0, 0)
    return pl.BlockSpec((None, None, 1, d), index)


def _spread(items, n_slots):
    slots = [[] for _ in range(n_slots)]
    for k, item in enumerate(items):
        slots[k * n_slots // len(items)].append(item)
    return slots


def _merge(a, b):
    keyed = [((k + 0.5) / len(a), 0, k, item) for k, item in enumerate(a)]
    keyed += [((k + 0.5) / len(b), 1, k, item) for k, item in enumerate(b)]
    return [entry[-1] for entry in sorted(keyed, key=lambda entry: entry[:3])]


def _cast_items(pairs):
    items = []
    for src, dst in pairs:
        rows = max(CAST_ITEM_BYTES // (4 * src.shape[1]), BF16_ROWS)
        for r in range(0, src.shape[0], rows):
            def run(src=src, dst=dst, part=slice(r, r + rows)):
                dst[part, :] = src[part, :].astype(BF16)
            items.append(run)
    return items


def _inproj_body(x_ref, g_ref, sc_ref, sh_ref, w_ref, gy_ref, xrg_ref, u_ref, v_ref, between=()):
    d_rg = gy_ref.shape[1]
    h = (_rms(x_ref[...]) * g_ref[...]) * (1.0 + sc_ref[...]) + sh_ref[...]
    hb = h.astype(BF16)
    outs = (gy_ref, xrg_ref, u_ref, v_ref)
    n_pieces = w_ref.shape[1] // PROJ_PIECE
    work = _spread(list(between), n_pieces) if between else [[]] * n_pieces
    for n in range(n_pieces):
        part = jnp.dot(hb, w_ref[:, n * PROJ_PIECE:(n + 1) * PROJ_PIECE], preferred_element_type=F32)
        dest, col = divmod(n * PROJ_PIECE, d_rg)
        if dest != 1:
            part = jax.nn.gelu(part)
        outs[dest][:, col:col + PROJ_PIECE] = part.astype(outs[dest].dtype)
        for item in work[n]:
            item()


def _inproj_specs(m, d, d_in, grp):
    d_rg = d_in // 4
    row = lambda i: (i, 0)
    in_specs = [
        pl.BlockSpec((IN_ROW_TILE, d), row),
        pl.BlockSpec((1, d), lambda i: (0, 0)),
        _mod_spec(1, grp, d, row_tile=IN_ROW_TILE),
        _mod_spec(0, grp, d, row_tile=IN_ROW_TILE),
        pl.BlockSpec((d, d_in), lambda i: (0, 0), pipeline_mode=pl.Buffered(1)),
    ]
    out_specs = [pl.BlockSpec((IN_ROW_TILE, d_rg), row)] * 4
    out_shape = [
        jax.ShapeDtypeStruct((m, d_rg), F32),
        jax.ShapeDtypeStruct((m, d_rg), F32),
        jax.ShapeDtypeStruct((m, d_rg), F32),
        jax.ShapeDtypeStruct((m, d_rg), BF16),
    ]
    return in_specs, out_specs, out_shape


def _scan_head(hd, reverse, edges, hf_rows, x_ref, gy_ref, cw_ref, cb_ref, gw_ref, gb_ref, lam_ref,
               mix_ref, hf_scr, carry_scr, xi_scr, hs_scr):
    tc, d_rg = x_ref.shape
    tl = tc // N_SEG
    head = d_rg // N_RG_HEADS
    lead = CONV_LEFT * N_SEG
    dirn = 1 if reverse else 0
    steps = range(tl - 1, -1, -1) if reverse else range(tl)
    segs = range(N_SEG - 1, -1, -1) if reverse else range(N_SEG)
    sub = lax.broadcasted_iota(jnp.int32, (N_SEG, head), 0)
    cols = slice(hd * head, (hd + 1) * head)
    xi = xi_scr.at[hd]
    for sg in range(N_SEG):
        xi[pl.ds(lead + sg, tl, stride=N_SEG), :] = x_ref[sg * tl:(sg + 1) * tl, cols]

    def step_rows(t):
        return xi[lead + t * N_SEG:lead + (t + 1) * N_SEG, :]

    for k in range(CONV_LEFT):
        if edges is None:
            edge = 0.0
        else:
            row = HALO - CONV_LEFT + k
            edge = jnp.where(edges[0], 0.0, edges[1][row:row + 1, cols])
        prev_step = pltpu.roll(step_rows(tl - CONV_LEFT + k), 1, axis=0)
        xi[k * N_SEG:(k + 1) * N_SEG, :] = jnp.where(sub == 0, edge, prev_step)
    for k in range(CONV_W - 1 - CONV_LEFT):
        edge = 0.0 if edges is None else jnp.where(edges[2], 0.0, edges[3][k:k + 1, cols])
        next_step = pltpu.roll(step_rows(k), N_SEG - 1, axis=0)
        xi[lead + (tl + k) * N_SEG:lead + (tl + k + 1) * N_SEG, :] = jnp.where(
            sub == N_SEG - 1, edge, next_step)

    xc = cb_ref[:, cols]
    for k in range(CONV_W):
        xc = xc + cw_ref[k:k + 1, cols] * xi[k * N_SEG:k * N_SEG + tc, :]
    g = jnp.dot(xc.astype(BF16), gw_ref[dirn, hd], preferred_element_type=F32)
    r = jax.nn.sigmoid(g[:, :head] + gb_ref[dirn, 0:1, cols])
    i = jax.nn.sigmoid(g[:, head:] + gb_ref[dirn, 1:2, cols])
    log_a = r * (-RG_C * jax.nn.softplus(-lam_ref[dirn:dirn + 1, cols]))
    a = jnp.exp(log_a)
    m2 = jnp.maximum(-jnp.tanh(log_a) * (a * a + 1.0), 0.0)
    mult = jnp.where(m2 > 0.0, m2 * lax.rsqrt(m2), 0.0)
    bx = mult * (i * xc)

    a3 = a.reshape(tl, N_SEG, head)
    b3 = bx.reshape(tl, N_SEG, head)
    hloc = [None] * tl
    prod = [None] * tl
    hprev = pprev = None
    for t in steps:
        if hprev is None:
            hprev, pprev = b3[t], a3[t]
        else:
            hprev, pprev = a3[t] * hprev + b3[t], a3[t] * pprev
        hloc[t], prod[t] = hprev, pprev
    state = carry_scr[:, cols]
    init = jnp.zeros((N_SEG, head), F32)
    for sg in segs:
        init = jnp.where(sub == sg, state, init)
        state = hprev[sg:sg + 1, :] + pprev[sg:sg + 1, :] * state
    carry_scr[:, cols] = state
    hfull = jnp.concatenate([hloc[t] + prod[t] * init for t in range(tl)], axis=0)
    if not reverse:
        hf_scr[hf_rows, cols] = hfull
    else:
        hs = hs_scr.at[hd]
        hs[...] = hf_scr[hf_rows, cols] + hfull
        for sg in range(N_SEG):
            rows = slice(sg * tl, (sg + 1) * tl)
            hsum = hs[pl.ds(sg, tl, stride=N_SEG), :]
            mix_ref[rows, cols] = (gy_ref[rows, cols] * hsum).astype(BF16)


def _sgu_head(hd, u_ref, v_ref, sw_ref, sb_ref, mix_ref):
    tc, d_ch = u_ref.shape
    ch_head = d_ch // N_CH_HEADS
    d_rg = mix_ref.shape[1] - d_ch
    cols = slice(hd * ch_head, (hd + 1) * ch_head)
    for n in range(tc // CHUNK):
        rows = slice(n * CHUNK, (n + 1) * CHUNK)
        mixed = jnp.dot(sw_ref[hd], v_ref[rows, cols], preferred_element_type=F32) + sb_ref[hd]
        mix_ref[rows, d_rg + hd * ch_head:d_rg + (hd + 1) * ch_head] = (
            u_ref[rows, cols] * mixed).astype(BF16)


def _seq_items(reverse, edges, hf_rows, x_ref, gy_ref, u_ref, v_ref,
               cw_ref, cb_ref, gw_ref, gb_ref, lam_ref, sw_ref, sb_ref,
               mix_ref, hf_scr, carry_scr, xi_scr, hs_scr):
    def item(hd):
        def run():
            _scan_head(hd, reverse, edges, hf_rows, x_ref, gy_ref, cw_ref, cb_ref, gw_ref, gb_ref,
                       lam_ref, mix_ref, hf_scr, carry_scr, xi_scr, hs_scr)
            if reverse:
                _sgu_head(hd, u_ref, v_ref, sw_ref, sb_ref, mix_ref)
        return run
    return [item(hd) for hd in range(N_RG_HEADS)]


def _seq_weight_specs(sp):
    full = lambda a: pl.BlockSpec(a.shape, lambda *_: (0,) * a.ndim, pipeline_mode=pl.Buffered(1))
    return [full(sp[k]) for k in ("conv_w", "conv_b", "gw", "gb", "lam", "sgu_w", "sgu_b")]


def _seq_weights(sp):
    return [sp[k] for k in ("conv_w", "conv_b", "gw", "gb", "lam", "sgu_w", "sgu_b")]


def _seq_scratch(seq_len, tc, d_rg):
    head = d_rg // N_RG_HEADS
    return [
        pltpu.VMEM((seq_len, d_rg), F32),
        pltpu.VMEM((1, d_rg), F32),
        pltpu.VMEM((N_RG_HEADS, tc + (CONV_W - 1) * N_SEG, head), F32),
        pltpu.VMEM((N_RG_HEADS, tc, head), F32),
    ]


N_INPROJ_IN = 5
N_INPROJ_OUT = 4
N_SEQ_IN = 12
N_SEQ_OUT = 2
N_SEQ_SCRATCH = 4


def _inproj_kernel(*refs, n_cast, with_seq):
    s = pl.program_id(0)
    refs = list(refs)
    take = lambda k: [refs.pop(0) for _ in range(k)]
    proj_in = take(N_INPROJ_IN)
    seq_in = take(N_SEQ_IN) if with_seq else None
    cast_src = take(n_cast)
    proj_out = take(N_INPROJ_OUT)
    seq_out = take(N_SEQ_OUT) if with_seq else None
    cast_dst = take(n_cast)
    casts = _cast_items(list(zip(cast_src, cast_dst)))
    if not with_seq:
        _inproj_body(*proj_in, *proj_out, between=casts)
        return

    x_ref, gy_ref, u_ref, v_ref, h0_ref = seq_in[:5]
    weights = seq_in[5:]
    mix_ref, st_ref = seq_out
    hf_scr, carry_scr, xi_scr, hs_scr = take(N_SEQ_SCRATCH)
    tc = x_ref.shape[0]

    def direction(reverse):
        dirn = 1 if reverse else 0
        heads = _seq_items(reverse, None, slice(0, tc), x_ref, gy_ref, u_ref, v_ref, *weights,
                           mix_ref, hf_scr, carry_scr, xi_scr, hs_scr)

        def first():
            carry_scr[...] = h0_ref[dirn:dirn + 1, :]
            heads[0]()

        def last():
            heads[-1]()
            st_ref[dirn:dirn + 1, :] = carry_scr[...]

        items = [first] + heads[1:-1] + [last]
        _inproj_body(*proj_in, *proj_out, between=_merge(items, casts) if casts else items)

    @pl.when(s % 2 == 0)
    def _():
        direction(False)

    @pl.when(s % 2 == 1)
    def _():
        direction(True)


def _inproj_seq(x, mod, grp, norm_g, w_in, seq, sp, to_cast):
    m, d = x.shape
    n_steps = m // IN_ROW_TILE
    in_specs, out_specs, out_shape = _inproj_specs(m, d, w_in.shape[1], grp)
    args = [x, norm_g.reshape(1, d), mod, mod, w_in]
    scratch = []
    if seq is not None:
        xrg, gy, u, v, h0, seq_len = seq
        ms, d_rg = xrg.shape
        assert 2 * (ms // seq_len) == n_steps
        blk = lambda i: (i // 2, 0)
        per_seq = pl.BlockSpec((None, 2, d_rg), lambda i: (i // 2, 0, 0))
        in_specs = in_specs + [pl.BlockSpec((seq_len, d_rg), blk)] * 4 + [per_seq] + _seq_weight_specs(sp)
        args = args + [xrg, gy, u, v, h0] + _seq_weights(sp)
        out_specs = out_specs + [pl.BlockSpec((seq_len, 2 * d_rg), blk), per_seq]
        out_shape = out_shape + [jax.ShapeDtypeStruct((ms, 2 * d_rg), BF16),
                                 jax.ShapeDtypeStruct((ms // seq_len, 2, d_rg), F32)]
        scratch = _seq_scratch(seq_len, seq_len, d_rg)
    for w in to_cast:
        rows, cols = w.shape[0] // n_steps, w.shape[1]
        assert rows * n_steps == w.shape[0] and rows % BF16_ROWS == 0
        in_specs = in_specs + [pl.BlockSpec((rows, cols), lambda i: (i, 0))]
        args = args + [w]
    out_specs = out_specs + [pl.BlockSpec((w.shape[0] // n_steps, w.shape[1]), lambda i: (i, 0))
                             for w in to_cast]
    out_shape = out_shape + [jax.ShapeDtypeStruct(w.shape, BF16) for w in to_cast]
    return pl.pallas_call(
        functools.partial(_inproj_kernel, n_cast=len(to_cast), with_seq=seq is not None),
        grid=(n_steps,),
        in_specs=in_specs,
        out_specs=out_specs,
        out_shape=out_shape,
        scratch_shapes=scratch,
        compiler_params=_params(dimension_semantics=("arbitrary",)),
        name="in_projection" if seq is None else "in_projection_seqmix",
    )(*args)


def _ffn_first(x_ref, mix_ref, g1_ref, sc_ref, sh_ref, ng_ref, wo_ref, o_ref, h2_scr, acc_scr):
    mix = jnp.dot(mix_ref[...], wo_ref[...], preferred_element_type=F32)
    x1 = x_ref[...] + g1_ref[...] * mix
    o_ref[...] = x1
    h2 = (_rms(x1) * ng_ref[...]) * (1.0 + sc_ref[...]) + sh_ref[...]
    h2_scr[...] = h2.astype(BF16)
    acc_scr[...] = jnp.zeros_like(acc_scr)


def _ffn_step(w1_ref, w2_ref, h2_scr, acc_scr, between=(), assign=False):
    between = list(between)
    n_w1 = w1_ref.shape[1] // FF_PIECE
    n_pieces = n_w1 + 2
    per = -(-len(between) // n_pieces)
    half_d = w2_ref.shape[1] // 2

    def gap():
        for _ in range(min(per, len(between))):
            between.pop(0)()

    h2 = h2_scr[...]
    acts = []
    for n in range(n_w1):
        gap()
        a = jnp.dot(h2, w1_ref[:, n * FF_PIECE:(n + 1) * FF_PIECE], preferred_element_type=F32)
        acts.append(jnp.square(jnp.maximum(a, 0.0)).astype(BF16))
    act = jnp.concatenate(acts, axis=1)
    for n in range(2):
        gap()
        cols = slice(n * half_d, (n + 1) * half_d)
        part = jnp.dot(act, w2_ref[:, cols], preferred_element_type=F32)
        acc_scr[:, cols] = part if assign else acc_scr[:, cols] + part
    for item in between:
        item()


def _ffn_last(g2_ref, fg_ref, o_ref, acc_scr, final_norm):
    x2 = o_ref[...] + g2_ref[...] * acc_scr[...]
    if final_norm:
        x2 = _rms(x2) * fg_ref[...]
    o_ref[...] = x2


def _ffn_pipe_kernel(x_ref, mix_ref, g1_ref, sc_ref, sh_ref, g2_ref, ng_ref, fg_ref,
                     wo_ref, w1_ref, w2_ref, o_ref, h2_scr, x1_scr, acc_scr, *, n_tiles, final_norm):
    t = pl.program_id(0)
    steps = (pl.num_programs(0) - 1) // n_tiles
    i = t // steps
    j = t % steps
    h2_cur = h2_scr.at[i % 2]
    h2_next = h2_scr.at[(i + 1) % 2]

    def first(h2_dst):
        mix = jnp.dot(mix_ref[...], wo_ref[...], preferred_element_type=F32)
        x1 = x_ref[...] + g1_ref[...] * mix
        x1_scr[...] = x1
        h2 = (_rms(x1) * ng_ref[...]) * (1.0 + sc_ref[...]) + sh_ref[...]
        h2_dst[...] = h2.astype(BF16)

    def last():
        _ffn_last(g2_ref, fg_ref, o_ref, acc_scr, final_norm)

    def keep_x1():
        o_ref[...] = x1_scr[...]

    @pl.when(t == 0)
    def _():
        first(h2_cur)
        _ffn_step(w1_ref, w2_ref, h2_cur, acc_scr, assign=True)

    @pl.when(jnp.logical_and(j == 0, jnp.logical_and(i > 0, i < n_tiles)))
    def _():
        last()
        _ffn_step(w1_ref, w2_ref, h2_cur, acc_scr, assign=True)

    @pl.when(t == n_tiles * steps)
    def _():
        last()

    @pl.when(j == 1)
    def _():
        _ffn_step(w1_ref, w2_ref, h2_cur, acc_scr, between=[keep_x1])

    @pl.when(jnp.logical_and(j > 1, j < steps - 1))
    def _():
        _ffn_step(w1_ref, w2_ref, h2_cur, acc_scr)

    @pl.when(jnp.logical_and(j == steps - 1, i < n_tiles))
    def _():
        first(h2_next)
        _ffn_step(w1_ref, w2_ref, h2_cur, acc_scr)


def _ffn_seq_kernel(x_ref, mix_ref, g1_ref, sc_ref, sh_ref, g2_ref, ng_ref, fg_ref,
                    wo_ref, w1_ref, w2_ref,
                    sx_ref, sprev_ref, snext_ref, gy_ref, u_ref, v_ref, h0_ref,
                    cw_ref, cb_ref, gw_ref, gb_ref, lam_ref, sw_ref, sb_ref,
                    o_ref, smix_ref,
                    h2_scr, acc_scr, hf_scr, carry_scr, xi_scr, hs_scr, *, final_norm):
    i = pl.program_id(0)
    j = pl.program_id(1)
    n_chunks = pl.num_programs(1)
    tc = sx_ref.shape[0]
    backward = i % 2 == 1
    jj = jnp.where(backward, n_chunks - 1 - j, j)
    hf_rows = pl.ds(pl.multiple_of(jj * tc, tc), tc)
    edges = (jj == 0, sprev_ref, jj == n_chunks - 1, snext_ref)

    @pl.when(j == 0)
    def _():
        _ffn_first(x_ref, mix_ref, g1_ref, sc_ref, sh_ref, ng_ref, wo_ref, o_ref, h2_scr, acc_scr)
        carry_scr[...] = h0_ref[pl.ds(i % 2, 1), :]

    def both(reverse):
        items = _seq_items(reverse, edges, hf_rows, sx_ref, gy_ref, u_ref, v_ref,
                           cw_ref, cb_ref, gw_ref, gb_ref, lam_ref, sw_ref, sb_ref,
                           smix_ref, hf_scr, carry_scr, xi_scr, hs_scr)
        _ffn_step(w1_ref, w2_ref, h2_scr, acc_scr, between=items)

    @pl.when(jnp.logical_not(backward))
    def _():
        both(False)

    @pl.when(backward)
    def _():
        both(True)

    @pl.when(j == n_chunks - 1)
    def _():
        _ffn_last(g2_ref, fg_ref, o_ref, acc_scr, final_norm)


def _ffn_pipelined(x, mix, mod, grp, norm_g, final_g, w_out, w_ff1, w_ff2, final_norm, ff_tile):
    m, d = x.shape
    d_mix = mix.shape[1]
    steps = w_ff1.shape[1] // ff_tile
    n = m // ROW_TILE
    head_tile = lambda t: jnp.minimum((t + 1) // steps, n - 1)
    tail_tile = lambda t: jnp.clip((t - 1) // steps, 0, n - 1)
    ff_step = lambda t: jnp.where(t == n * steps, steps - 1, t % steps)
    vec = pl.BlockSpec((1, d), lambda t: (0, 0))
    return pl.pallas_call(
        functools.partial(_ffn_pipe_kernel, n_tiles=n, final_norm=final_norm),
        grid=(n * steps + 1,),
        in_specs=[
            pl.BlockSpec((ROW_TILE, d), lambda t: (head_tile(t), 0)),
            pl.BlockSpec((ROW_TILE, d_mix), lambda t: (head_tile(t), 0)),
            _mod_spec(2, grp, d, head_tile),
            _mod_spec(4, grp, d, head_tile),
            _mod_spec(3, grp, d, head_tile),
            _mod_spec(5, grp, d, tail_tile),
            vec,
            vec,
            pl.BlockSpec((d_mix, d), lambda t: (0, 0), pipeline_mode=pl.Buffered(1)),
            pl.BlockSpec((d, ff_tile), lambda t: (0, ff_step(t))),
            pl.BlockSpec((ff_tile, d), lambda t: (ff_step(t), 0)),
        ],
        out_specs=pl.BlockSpec((ROW_TILE, d), lambda t: (tail_tile(t), 0)),
        out_shape=jax.ShapeDtypeStruct((m, d), F32),
        scratch_shapes=[
            pltpu.VMEM((2, ROW_TILE, d), BF16),
            pltpu.VMEM((ROW_TILE, d), F32),
            pltpu.VMEM((ROW_TILE, d), F32),
        ],
        compiler_params=_params(dimension_semantics=("arbitrary",)),
        name="out_projection_ffn",
    )(x, mix, mod, mod, mod, mod, norm_g.reshape(1, d), final_g.reshape(1, d), w_out, w_ff1, w_ff2)


def _ffn_seq(x, mix, mod, grp, norm_g, final_g, w_out, w_ff1, w_ff2, final_norm, ff_tile, seq, sp):
    m, d = x.shape
    d_mix = mix.shape[1]
    n_steps = w_ff1.shape[1] // ff_tile
    row = lambda i, j: (i, 0)
    vec = pl.BlockSpec((1, d), lambda i, j: (0, 0))
    xrg, gy, u, v, h0, seq_len = seq
    ms, d_rg = xrg.shape
    n_seq = ms // seq_len
    tc = seq_len // n_steps
    assert 2 * n_seq == m // ROW_TILE and tc % (N_SEG * SUBLANES) == 0 and tc % CHUNK == 0
    per_tile = tc // HALO
    n_halo = ms // HALO

    def chunk(i, j):
        return (i // 2) * n_steps + jnp.where(i % 2 == 1, n_steps - 1 - j, j)

    def out_chunk(i, j):
        return ((i // 2) * n_steps + jnp.where(i % 2 == 1, n_steps - 1 - j, n_steps - 1), 0)

    blk = lambda rows, cols, index: pl.BlockSpec((rows, cols), index)
    return pl.pallas_call(
        functools.partial(_ffn_seq_kernel, final_norm=final_norm),
        grid=(m // ROW_TILE, n_steps),
        in_specs=[
            pl.BlockSpec((ROW_TILE, d), row),
            pl.BlockSpec((ROW_TILE, d_mix), row),
            _mod_spec(2, grp, d),
            _mod_spec(4, grp, d),
            _mod_spec(3, grp, d),
            _mod_spec(5, grp, d),
            vec,
            vec,
            pl.BlockSpec((d_mix, d), lambda i, j: (0, 0), pipeline_mode=pl.Buffered(1)),
            pl.BlockSpec((d, ff_tile), lambda i, j: (0, j)),
            pl.BlockSpec((ff_tile, d), lambda i, j: (j, 0)),
            blk(tc, d_rg, lambda i, j: (chunk(i, j), 0)),
            blk(HALO, d_rg, lambda i, j: (jnp.maximum(chunk(i, j) * per_tile - 1, 0), 0)),
            blk(HALO, d_rg, lambda i, j: (jnp.minimum((chunk(i, j) + 1) * per_tile, n_halo - 1), 0)),
            blk(tc, d_rg, out_chunk),
            blk(tc, d_rg, out_chunk),
            blk(tc, d_rg, out_chunk),
            pl.BlockSpec((None, 2, d_rg), lambda i, j: (i // 2, 0, 0)),
        ] + _seq_weight_specs(sp),
        out_specs=[pl.BlockSpec((ROW_TILE, d), row), blk(tc, 2 * d_rg, out_chunk)],
        out_shape=[jax.ShapeDtypeStruct((m, d), F32), jax.ShapeDtypeStruct((ms, 2 * d_rg), BF16)],
        scratch_shapes=[
            pltpu.VMEM((ROW_TILE, d), BF16),
            pltpu.VMEM((ROW_TILE, d), F32),
        ] + _seq_scratch(seq_len, tc, d_rg),
        compiler_params=_params(dimension_semantics=("arbitrary", "arbitrary")),
        name="out_projection_ffn_seqmix",
    )(x, mix, mod, mod, mod, mod, norm_g.reshape(1, d), final_g.reshape(1, d), w_out, w_ff1, w_ff2,
      xrg, xrg, xrg, gy, u, v, h0, *_seq_weights(sp))


def kernel(x_prompt, x_sample, c, state_rglru, c_ctx, norm1_g, w_ada, b_ada, w_in, conv_w, conv_b,
           ga_w, ga_b, gi_w, gi_b, lru_lambda, sgu_w, sgu_b, w_out, norm2_g, w_ff1, w_ff2, final_g):
    b_ctx, l_ctx, d = x_prompt.shape
    b_lat, l_lat, _ = x_sample.shape
    depth = w_in.shape[0]
    d_rg = lru_lambda.shape[-1]
    assert b_lat + 1 <= MOD_ROWS
    assert (b_ctx * l_ctx) % ROW_TILE == 0 and l_lat % ROW_TILE == 0

    cond = jnp.concatenate(
        [c, c_ctx[None, :], jnp.zeros((MOD_ROWS - b_lat - 1, d), F32)], axis=0)
    grp_ctx = (b_lat, b_ctx * l_ctx)
    grp_lat = (0, l_lat)
    xp = x_prompt.reshape(b_ctx * l_ctx, d)
    xs = x_sample.reshape(b_lat * l_lat, d)
    h0_ctx = jnp.zeros((b_ctx, 2, d_rg), F32)
    ctx_states = []
    for l in range(depth):
        sp = {
            "conv_w": conv_w[l], "conv_b": conv_b[l].reshape(1, d_rg),
            "gw": jnp.concatenate([ga_w[l], gi_w[l]], axis=-1).astype(BF16),
            "gb": jnp.stack([ga_b[l], gi_b[l]], axis=1),
            "lam": lru_lambda[l],
            "sgu_w": sgu_w[l].astype(BF16), "sgu_b": sgu_b[l][:, :, None],
        }
        win = w_in[l].astype(BF16)
        last = l == depth - 1
        mod = _modulation(cond, w_ada[l], b_ada[l])
        h0_lat = state_rglru[:, l].astype(F32)

        gy_c, xrg_c, u_c, v_c, wff2, wout = _inproj_seq(
            xp, mod, grp_ctx, norm1_g[l], win, None, None, [w_ff2[l], w_out[l]])
        gy_l, xrg_l, u_l, v_l, mix_c, st, wff1 = _inproj_seq(
            xs, mod, grp_lat, norm1_g[l], win, (xrg_c, gy_c, u_c, v_c, h0_ctx, l_ctx), sp, [w_ff1[l]])
        xp, mix_l = _ffn_seq(xp, mix_c, mod, grp_ctx, norm2_g[l], final_g, wout, wff1, wff2, last,
                             FF_TILE_FUSED, (xrg_l, gy_l, u_l, v_l, h0_lat, l_lat), sp)
        xs = _ffn_pipelined(xs, mix_l, mod, grp_lat, norm2_g[l], final_g, wout, wff1, wff2, last,
                            FF_TILE_ALONE)
        ctx_states.append(st)
    y_prompt = xp.reshape(b_ctx, l_ctx, d)
    y_sample = xs.reshape(b_lat, l_lat, d)
    new_state = jnp.stack(ctx_states, axis=1).astype(x_prompt.dtype)
    return (y_prompt, y_sample, new_state)
```

```python
import functools

import jax
import jax.numpy as jnp
from jax import lax
from jax.experimental import pallas as pl
from jax.experimental.pallas import tpu as pltpu

F32 = jnp.float32
BF16 = jnp.bfloat16

N_RG_HEADS = 8
N_CH_HEADS = 8
CHUNK = 128
CONV_W = 4
CONV_LEFT = 2
RG_C = 8.0
N_MOD = 6
EPS = 1e-6

SUBLANES = 8
BF16_ROWS = 16
MOD_ROWS = 8
VMEM_LIMIT_BYTES = 62 * 1024 * 1024

ROW_TILE = 512
IN_ROW_TILE = 256
CAST_ITEM_BYTES = 1 << 19
FF_TILE_FUSED = 512
FF_TILE_ALONE = 1024
FF_PIECE = 256
W2_PARTS = 4
MOD_COL_TILE = 1024
PROJ_PIECE = 256
N_SEG = SUBLANES
HALO = SUBLANES


def _rms(x):
    return x * lax.rsqrt(jnp.mean(x * x, axis=-1, keepdims=True) + EPS)


def _params(**kw):
    return pltpu.CompilerParams(vmem_limit_bytes=VMEM_LIMIT_BYTES, **kw)


def _mod_kernel(c_ref, w_ref, b_ref, o_ref):
    s = jax.nn.silu(c_ref[...]).astype(BF16)
    o_ref[...] = jnp.dot(s, w_ref[...].astype(BF16), preferred_element_type=F32) + b_ref[...]


def _modulation(cond, w_ada, b_ada):
    d = cond.shape[1]
    n = w_ada.shape[1]
    out = pl.pallas_call(
        _mod_kernel,
        grid=(n // MOD_COL_TILE,),
        in_specs=[
            pl.BlockSpec((MOD_ROWS, d), lambda j: (0, 0)),
            pl.BlockSpec((d, MOD_COL_TILE), lambda j: (0, j)),
            pl.BlockSpec((1, MOD_COL_TILE), lambda j: (0, j)),
        ],
        out_specs=pl.BlockSpec((MOD_ROWS, MOD_COL_TILE), lambda j: (0, j)),
        out_shape=jax.ShapeDtypeStruct((MOD_ROWS, n), F32),
        compiler_params=_params(dimension_semantics=("arbitrary",)),
        name="modulation",
    )(cond, w_ada, b_ada.reshape(1, n))
    return out.reshape(MOD_ROWS, N_MOD, 1, d)


def _mod_spec(which, grp, d, tile=lambda i, *_: i, row_tile=ROW_TILE):
    mod_base, rows_per_mod = grp

    def index(*ids):
        return (mod_base + (tile(*ids) * row_tile) // rows_per_mod, which, 0, 0)
    return pl.BlockSpec((None, None, 1, d), index)


def _spread(items, n_slots):
    slots = [[] for _ in range(n_slots)]
    for k, item in enumerate(items):
        slots[k * n_slots // len(items)].append(item)
    return slots


def _merge(a, b):
    keyed = [((k + 0.5) / len(a), 0, k, item) for k, item in enumerate(a)]
    keyed += [((k + 0.5) / len(b), 1, k, item) for k, item in enumerate(b)]
    return [entry[-1] for entry in sorted(keyed, key=lambda entry: entry[:3])]


def _cast_items(pairs):
    items = []
    for src, dst in pairs:
        rows = max(CAST_ITEM_BYTES // (4 * src.shape[1]), BF16_ROWS)
        for r in range(0, src.shape[0], rows):
            def run(src=src, dst=dst, part=slice(r, r + rows)):
                dst[part, :] = src[part, :].astype(BF16)
            items.append(run)
    return items


def _inproj_body(x_ref, g_ref, sc_ref, sh_ref, w_ref, gy_ref, xrg_ref, u_ref, v_ref, between=()):
    d_rg = gy_ref.shape[1]
    h = (_rms(x_ref[...]) * g_ref[...]) * (1.0 + sc_ref[...]) + sh_ref[...]
    hb = h.astype(BF16)
    outs = (gy_ref, xrg_ref, u_ref, v_ref)
    n_pieces = w_ref.shape[1] // PROJ_PIECE
    work = _spread(list(between), n_pieces) if between else [[]] * n_pieces
    for n in range(n_pieces):
        part = jnp.dot(hb, w_ref[:, n * PROJ_PIECE:(n + 1) * PROJ_PIECE], preferred_element_type=F32)
        dest, col = divmod(n * PROJ_PIECE, d_rg)
        if dest != 1:
            part = jax.nn.gelu(part)
        outs[dest][:, col:col + PROJ_PIECE] = part.astype(outs[dest].dtype)
        for item in work[n]:
            item()


def _inproj_specs(m, d, d_in, grp):
    d_rg = d_in // 4
    row = lambda i: (i, 0)
    in_specs = [
        pl.BlockSpec((IN_ROW_TILE, d), row),
        pl.BlockSpec((1, d), lambda i: (0, 0)),
        _mod_spec(1, grp, d, row_tile=IN_ROW_TILE),
        _mod_spec(0, grp, d, row_tile=IN_ROW_TILE),
        pl.BlockSpec((d, d_in), lambda i: (0, 0), pipeline_mode=pl.Buffered(1)),
    ]
    out_specs = [pl.BlockSpec((IN_ROW_TILE, d_rg), row)] * 4
    out_shape = [
        jax.ShapeDtypeStruct((m, d_rg), F32),
        jax.ShapeDtypeStruct((m, d_rg), F32),
        jax.ShapeDtypeStruct((m, d_rg), F32),
        jax.ShapeDtypeStruct((m, d_rg), BF16),
    ]
    return in_specs, out_specs, out_shape


def _scan_head(hd, reverse, edges, hf_rows, x_ref, gy_ref, cw_ref, cb_ref, gw_ref, gb_ref, lam_ref,
               mix_ref, hf_scr, carry_scr, xi_scr, hs_scr):
    tc, d_rg = x_ref.shape
    tl = tc // N_SEG
    head = d_rg // N_RG_HEADS
    lead = CONV_LEFT * N_SEG
    dirn = 1 if reverse else 0
    steps = range(tl - 1, -1, -1) if reverse else range(tl)
    segs = range(N_SEG - 1, -1, -1) if reverse else range(N_SEG)
    sub = lax.broadcasted_iota(jnp.int32, (N_SEG, head), 0)
    cols = slice(hd * head, (hd + 1) * head)
    xi = xi_scr.at[hd]
    for sg in range(N_SEG):
        xi[pl.ds(lead + sg, tl, stride=N_SEG), :] = x_ref[sg * tl:(sg + 1) * tl, cols]

    def step_rows(t):
        return xi[lead + t * N_SEG:lead + (t + 1) * N_SEG, :]

    for k in range(CONV_LEFT):
        if edges is None:
            edge = 0.0
        else:
            row = HALO - CONV_LEFT + k
            edge = jnp.where(edges[0], 0.0, edges[1][row:row + 1, cols])
        prev_step = pltpu.roll(step_rows(tl - CONV_LEFT + k), 1, axis=0)
        xi[k * N_SEG:(k + 1) * N_SEG, :] = jnp.where(sub == 0, edge, prev_step)
    for k in range(CONV_W - 1 - CONV_LEFT):
        edge = 0.0 if edges is None else jnp.where(edges[2], 0.0, edges[3][k:k + 1, cols])
        next_step = pltpu.roll(step_rows(k), N_SEG - 1, axis=0)
        xi[lead + (tl + k) * N_SEG:lead + (tl + k + 1) * N_SEG, :] = jnp.where(
            sub == N_SEG - 1, edge, next_step)

    xc = cb_ref[:, cols]
    for k in range(CONV_W):
        xc = xc + cw_ref[k:k + 1, cols] * xi[k * N_SEG:k * N_SEG + tc, :]
    g = jnp.dot(xc.astype(BF16), gw_ref[dirn, hd], preferred_element_type=F32)
    r = jax.nn.sigmoid(g[:, :head] + gb_ref[dirn, 0:1, cols])
    i = jax.nn.sigmoid(g[:, head:] + gb_ref[dirn, 1:2, cols])
    log_a = r * (-RG_C * jax.nn.softplus(-lam_ref[dirn:dirn + 1, cols]))
    a = jnp.exp(log_a)
    m2 = jnp.maximum(-jnp.tanh(log_a) * (a * a + 1.0), 0.0)
    mult = jnp.where(m2 > 0.0, m2 * lax.rsqrt(m2), 0.0)
    bx = mult * (i * xc)

    a3 = a.reshape(tl, N_SEG, head)
    b3 = bx.reshape(tl, N_SEG, head)
    hloc = [None] * tl
    prod = [None] * tl
    hprev = pprev = None
    for t in steps:
        if hprev is None:
            hprev, pprev = b3[t], a3[t]
        else:
            hprev, pprev = a3[t] * hprev + b3[t], a3[t] * pprev
        hloc[t], prod[t] = hprev, pprev
    state = carry_scr[:, cols]
    init = jnp.zeros((N_SEG, head), F32)
    for sg in segs:
        init = jnp.where(sub == sg, state, init)
        state = hprev[sg:sg + 1, :] + pprev[sg:sg + 1, :] * state
    carry_scr[:, cols] = state
    hfull = jnp.concatenate([hloc[t] + prod[t] * init for t in range(tl)], axis=0)
    if not reverse:
        hf_scr[hf_rows, cols] = hfull
    else:
        hs = hs_scr.at[hd]
        hs[...] = hf_scr[hf_rows, cols] + hfull
        for sg in range(N_SEG):
            rows = slice(sg * tl, (sg + 1) * tl)
            hsum = hs[pl.ds(sg, tl, stride=N_SEG), :]
            mix_ref[rows, cols] = (gy_ref[rows, cols] * hsum).astype(BF16)


def _sgu_head(hd, u_ref, v_ref, sw_ref, sb_ref, mix_ref):
    tc, d_ch = u_ref.shape
    ch_head = d_ch // N_CH_HEADS
    d_rg = mix_ref.shape[1] - d_ch
    cols = slice(hd * ch_head, (hd + 1) * ch_head)
    for n in range(tc // CHUNK):
        rows = slice(n * CHUNK, (n + 1) * CHUNK)
        mixed = jnp.dot(sw_ref[hd], v_ref[rows, cols], preferred_element_type=F32) + sb_ref[hd]
        mix_ref[rows, d_rg + hd * ch_head:d_rg + (hd + 1) * ch_head] = (
            u_ref[rows, cols] * mixed).astype(BF16)


def _seq_items(reverse, edges, hf_rows, x_ref, gy_ref, u_ref, v_ref,
               cw_ref, cb_ref, gw_ref, gb_ref, lam_ref, sw_ref, sb_ref,
               mix_ref, hf_scr, carry_scr, xi_scr, hs_scr):
    def item(hd):
        def run():
            _scan_head(hd, reverse, edges, hf_rows, x_ref, gy_ref, cw_ref, cb_ref, gw_ref, gb_ref,
                       lam_ref, mix_ref, hf_scr, carry_scr, xi_scr, hs_scr)
            if reverse:
                _sgu_head(hd, u_ref, v_ref, sw_ref, sb_ref, mix_ref)
        return run
    return [item(hd) for hd in range(N_RG_HEADS)]


def _seq_weight_specs(sp):
    full = lambda a: pl.BlockSpec(a.shape, lambda *_: (0,) * a.ndim, pipeline_mode=pl.Buffered(1))
    return [full(sp[k]) for k in ("conv_w", "conv_b", "gw", "gb", "lam", "sgu_w", "sgu_b")]


def _seq_weights(sp):
    return [sp[k] for k in ("conv_w", "conv_b", "gw", "gb", "lam", "sgu_w", "sgu_b")]


def _seq_scratch(seq_len, tc, d_rg):
    head = d_rg // N_RG_HEADS
    return [
        pltpu.VMEM((seq_len, d_rg), F32),
        pltpu.VMEM((1, d_rg), F32),
        pltpu.VMEM((N_RG_HEADS, tc + (CONV_W - 1) * N_SEG, head), F32),
        pltpu.VMEM((N_RG_HEADS, tc, head), F32),
    ]


N_INPROJ_IN = 5
N_INPROJ_OUT = 4
N_SEQ_IN = 12
N_SEQ_OUT = 2
N_SEQ_SCRATCH = 4


def _inproj_kernel(*refs, n_cast, with_seq):
    s = pl.program_id(0)
    refs = list(refs)
    take = lambda k: [refs.pop(0) for _ in range(k)]
    proj_in = take(N_INPROJ_IN)
    seq_in = take(N_SEQ_IN) if with_seq else None
    cast_src = take(n_cast)
    proj_out = take(N_INPROJ_OUT)
    seq_out = take(N_SEQ_OUT) if with_seq else None
    cast_dst = take(n_cast)
    casts = _cast_items(list(zip(cast_src, cast_dst)))
    if not with_seq:
        _inproj_body(*proj_in, *proj_out, between=casts)
        return

    x_ref, gy_ref, u_ref, v_ref, h0_ref = seq_in[:5]
    weights = seq_in[5:]
    mix_ref, st_ref = seq_out
    hf_scr, carry_scr, xi_scr, hs_scr = take(N_SEQ_SCRATCH)
    tc = x_ref.shape[0]

    def direction(reverse):
        dirn = 1 if reverse else 0
        heads = _seq_items(reverse, None, slice(0, tc), x_ref, gy_ref, u_ref, v_ref, *weights,
                           mix_ref, hf_scr, carry_scr, xi_scr, hs_scr)

        def first():
            carry_scr[...] = h0_ref[dirn:dirn + 1, :]
            heads[0]()

        def last():
            heads[-1]()
            st_ref[dirn:dirn + 1, :] = carry_scr[...]

        items = [first] + heads[1:-1] + [last]
        _inproj_body(*proj_in, *proj_out, between=_merge(items, casts) if casts else items)

    @pl.when(s % 2 == 0)
    def _():
        direction(False)

    @pl.when(s % 2 == 1)
    def _():
        direction(True)


def _inproj_seq(x, mod, grp, norm_g, w_in, seq, sp, to_cast):
    m, d = x.shape
    n_steps = m // IN_ROW_TILE
    in_specs, out_specs, out_shape = _inproj_specs(m, d, w_in.shape[1], grp)
    args = [x, norm_g.reshape(1, d), mod, mod, w_in]
    scratch = []
    if seq is not None:
        xrg, gy, u, v, h0, seq_len = seq
        ms, d_rg = xrg.shape
        assert 2 * (ms // seq_len) == n_steps
        blk = lambda i: (i // 2, 0)
        per_seq = pl.BlockSpec((None, 2, d_rg), lambda i: (i // 2, 0, 0))
        in_specs = in_specs + [pl.BlockSpec((seq_len, d_rg), blk)] * 4 + [per_seq] + _seq_weight_specs(sp)
        args = args + [xrg, gy, u, v, h0] + _seq_weights(sp)
        out_specs = out_specs + [pl.BlockSpec((seq_len, 2 * d_rg), blk), per_seq]
        out_shape = out_shape + [jax.ShapeDtypeStruct((ms, 2 * d_rg), BF16),
                                 jax.ShapeDtypeStruct((ms // seq_len, 2, d_rg), F32)]
        scratch = _seq_scratch(seq_len, seq_len, d_rg)
    for w in to_cast:
        rows, cols = w.shape[0] // n_steps, w.shape[1]
        assert rows * n_steps == w.shape[0] and rows % BF16_ROWS == 0
        in_specs = in_specs + [pl.BlockSpec((rows, cols), lambda i: (i, 0))]
        args = args + [w]
    out_specs = out_specs + [pl.BlockSpec((w.shape[0] // n_steps, w.shape[1]), lambda i: (i, 0))
                             for w in to_cast]
    out_shape = out_shape + [jax.ShapeDtypeStruct(w.shape, BF16) for w in to_cast]
    return pl.pallas_call(
        functools.partial(_inproj_kernel, n_cast=len(to_cast), with_seq=seq is not None),
        grid=(n_steps,),
        in_specs=in_specs,
        out_specs=out_specs,
        out_shape=out_shape,
        scratch_shapes=scratch,
        compiler_params=_params(dimension_semantics=("arbitrary",)),
        name="in_projection" if seq is None else "in_projection_seqmix",
    )(*args)


def _ffn_first(x_ref, mix_ref, g1_ref, sc_ref, sh_ref, ng_ref, wo_ref, o_ref, h2_scr, acc_scr):
    mix = jnp.dot(mix_ref[...], wo_ref[...], preferred_element_type=F32)
    x1 = x_ref[...] + g1_ref[...] * mix
    o_ref[...] = x1
    h2 = (_rms(x1) * ng_ref[...]) * (1.0 + sc_ref[...]) + sh_ref[...]
    h2_scr[...] = h2.astype(BF16)
    acc_scr[...] = jnp.zeros_like(acc_scr)


def _ffn_step(w1_ref, w2_ref, h2_scr, acc_scr, between=(), assign=False):
    n_w1 = w1_ref.shape[1] // FF_PIECE
    n_pieces = n_w1 + W2_PARTS
    work = _spread(list(between), n_pieces) if between else [[]] * n_pieces
    part_d = w2_ref.shape[1] // W2_PARTS

    h2 = h2_scr[...]
    acts = []
    for n in range(n_w1):
        for item in work[n]:
            item()
        a = jnp.dot(h2, w1_ref[:, n * FF_PIECE:(n + 1) * FF_PIECE], preferred_element_type=F32)
        acts.append(jnp.square(jnp.maximum(a, 0.0)).astype(BF16))
    act = jnp.concatenate(acts, axis=1)
    for n in range(W2_PARTS):
        for item in work[n_w1 + n]:
            item()
        cols = slice(n * part_d, (n + 1) * part_d)
        part = jnp.dot(act, w2_ref[:, cols], preferred_element_type=F32)
        acc_scr[:, cols] = part if assign else acc_scr[:, cols] + part


def _ffn_last(g2_ref, fg_ref, o_ref, acc_scr, final_norm):
    x2 = o_ref[...] + g2_ref[...] * acc_scr[...]
    if final_norm:
        x2 = _rms(x2) * fg_ref[...]
    o_ref[...] = x2


def _ffn_pipe_kernel(x_ref, mix_ref, g1_ref, sc_ref, sh_ref, g2_ref, ng_ref, fg_ref,
                     wo_ref, w1_ref, w2_ref, o_ref, h2_scr, h2n_scr, x1_scr, acc_scr,
                     *, n_tiles, final_norm):
    t = pl.program_id(0)
    steps = (pl.num_programs(0) - 1) // n_tiles
    i = t // steps
    j = t % steps

    def first():
        mix = jnp.dot(mix_ref[...], wo_ref[...], preferred_element_type=F32)
        x1 = x_ref[...] + g1_ref[...] * mix
        x1_scr[...] = x1
        h2 = (_rms(x1) * ng_ref[...]) * (1.0 + sc_ref[...]) + sh_ref[...]
        h2n_scr[...] = h2.astype(BF16)

    def last():
        _ffn_last(g2_ref, fg_ref, o_ref, acc_scr, final_norm)

    def keep_h2():
        h2_scr[...] = h2n_scr[...]

    def keep_x1():
        o_ref[...] = x1_scr[...]

    @pl.when(t == 0)
    def _():
        first()
        _ffn_step(w1_ref, w2_ref, h2n_scr, acc_scr, between=[keep_h2], assign=True)

    @pl.when(jnp.logical_and(j == 0, jnp.logical_and(i > 0, i < n_tiles)))
    def _():
        last()
        _ffn_step(w1_ref, w2_ref, h2n_scr, acc_scr, between=[keep_h2], assign=True)

    @pl.when(t == n_tiles * steps)
    def _():
        last()

    @pl.when(j == 1)
    def _():
        _ffn_step(w1_ref, w2_ref, h2_scr, acc_scr, between=[keep_x1])

    @pl.when(jnp.logical_and(j > 1, j < steps - 1))
    def _():
        _ffn_step(w1_ref, w2_ref, h2_scr, acc_scr)

    @pl.when(jnp.logical_and(j == steps - 1, i < n_tiles))
    def _():
        first()
        _ffn_step(w1_ref, w2_ref, h2_scr, acc_scr)


def _ffn_seq_kernel(x_ref, mix_ref, g1_ref, sc_ref, sh_ref, g2_ref, ng_ref, fg_ref,
                    wo_ref, w1_ref, w2_ref,
                    sx_ref, sprev_ref, snext_ref, gy_ref, u_ref, v_ref, h0_ref,
                    cw_ref, cb_ref, gw_ref, gb_ref, lam_ref, sw_ref, sb_ref,
                    o_ref, smix_ref,
                    h2_scr, acc_scr, hf_scr, carry_scr, xi_scr, hs_scr, *, final_norm):
    i = pl.program_id(0)
    j = pl.program_id(1)
    n_chunks = pl.num_programs(1)
    tc = sx_ref.shape[0]
    backward = i % 2 == 1
    jj = jnp.where(backward, n_chunks - 1 - j, j)
    hf_rows = pl.ds(pl.multiple_of(jj * tc, tc), tc)
    edges = (jj == 0, sprev_ref, jj == n_chunks - 1, snext_ref)

    @pl.when(j == 0)
    def _():
        _ffn_first(x_ref, mix_ref, g1_ref, sc_ref, sh_ref, ng_ref, wo_ref, o_ref, h2_scr, acc_scr)
        carry_scr[...] = h0_ref[pl.ds(i % 2, 1), :]

    def both(reverse):
        items = _seq_items(reverse, edges, hf_rows, sx_ref, gy_ref, u_ref, v_ref,
                           cw_ref, cb_ref, gw_ref, gb_ref, lam_ref, sw_ref, sb_ref,
                           smix_ref, hf_scr, carry_scr, xi_scr, hs_scr)
        _ffn_step(w1_ref, w2_ref, h2_scr, acc_scr, between=items)

    @pl.when(jnp.logical_not(backward))
    def _():
        both(False)

    @pl.when(backward)
    def _():
        both(True)

    @pl.when(j == n_chunks - 1)
    def _():
        _ffn_last(g2_ref, fg_ref, o_ref, acc_scr, final_norm)


def _ffn_pipelined(x, mix, mod, grp, norm_g, final_g, w_out, w_ff1, w_ff2, final_norm, ff_tile):
    m, d = x.shape
    d_mix = mix.shape[1]
    steps = w_ff1.shape[1] // ff_tile
    n = m // ROW_TILE
    head_tile = lambda t: jnp.minimum((t + 1) // steps, n - 1)
    tail_tile = lambda t: jnp.clip((t - 1) // steps, 0, n - 1)
    ff_step = lambda t: jnp.where(t == n * steps, steps - 1, t % steps)
    vec = pl.BlockSpec((1, d), lambda t: (0, 0))
    return pl.pallas_call(
        functools.partial(_ffn_pipe_kernel, n_tiles=n, final_norm=final_norm),
        grid=(n * steps + 1,),
        in_specs=[
            pl.BlockSpec((ROW_TILE, d), lambda t: (head_tile(t), 0)),
            pl.BlockSpec((ROW_TILE, d_mix), lambda t: (head_tile(t), 0)),
            _mod_spec(2, grp, d, head_tile),
            _mod_spec(4, grp, d, head_tile),
            _mod_spec(3, grp, d, head_tile),
            _mod_spec(5, grp, d, tail_tile),
            vec,
            vec,
            pl.BlockSpec((d_mix, d), lambda t: (0, 0), pipeline_mode=pl.Buffered(1)),
            pl.BlockSpec((d, ff_tile), lambda t: (0, ff_step(t))),
            pl.BlockSpec((ff_tile, d), lambda t: (ff_step(t), 0)),
        ],
        out_specs=pl.BlockSpec((ROW_TILE, d), lambda t: (tail_tile(t), 0)),
        out_shape=jax.ShapeDtypeStruct((m, d), F32),
        scratch_shapes=[
            pltpu.VMEM((ROW_TILE, d), BF16),
            pltpu.VMEM((ROW_TILE, d), BF16),
            pltpu.VMEM((ROW_TILE, d), F32),
            pltpu.VMEM((ROW_TILE, d), F32),
        ],
        compiler_params=_params(dimension_semantics=("arbitrary",)),
        name="out_projection_ffn",
    )(x, mix, mod, mod, mod, mod, norm_g.reshape(1, d), final_g.reshape(1, d), w_out, w_ff1, w_ff2)


def _ffn_seq(x, mix, mod, grp, norm_g, final_g, w_out, w_ff1, w_ff2, final_norm, ff_tile, seq, sp):
    m, d = x.shape
    d_mix = mix.shape[1]
    n_steps = w_ff1.shape[1] // ff_tile
    row = lambda i, j: (i, 0)
    vec = pl.BlockSpec((1, d), lambda i, j: (0, 0))
    xrg, gy, u, v, h0, seq_len = seq
    ms, d_rg = xrg.shape
    n_seq = ms // seq_len
    tc = seq_len // n_steps
    assert 2 * n_seq == m // ROW_TILE and tc % (N_SEG * SUBLANES) == 0 and tc % CHUNK == 0
    per_tile = tc // HALO
    n_halo = ms // HALO

    def chunk(i, j):
        return (i // 2) * n_steps + jnp.where(i % 2 == 1, n_steps - 1 - j, j)

    def out_chunk(i, j):
        return ((i // 2) * n_steps + jnp.where(i % 2 == 1, n_steps - 1 - j, n_steps - 1), 0)

    blk = lambda rows, cols, index: pl.BlockSpec((rows, cols), index)
    return pl.pallas_call(
        functools.partial(_ffn_seq_kernel, final_norm=final_norm),
        grid=(m // ROW_TILE, n_steps),
        in_specs=[
            pl.BlockSpec((ROW_TILE, d), row),
            pl.BlockSpec((ROW_TILE, d_mix), row),
            _mod_spec(2, grp, d),
            _mod_spec(4, grp, d),
            _mod_spec(3, grp, d),
            _mod_spec(5, grp, d),
            vec,
            vec,
            pl.BlockSpec((d_mix, d), lambda i, j: (0, 0), pipeline_mode=pl.Buffered(1)),
            pl.BlockSpec((d, ff_tile), lambda i, j: (0, j)),
            pl.BlockSpec((ff_tile, d), lambda i, j: (j, 0)),
            blk(tc, d_rg, lambda i, j: (chunk(i, j), 0)),
            blk(HALO, d_rg, lambda i, j: (jnp.maximum(chunk(i, j) * per_tile - 1, 0), 0)),
            blk(HALO, d_rg, lambda i, j: (jnp.minimum((chunk(i, j) + 1) * per_tile, n_halo - 1), 0)),
            blk(tc, d_rg, out_chunk),
            blk(tc, d_rg, out_chunk),
            blk(tc, d_rg, out_chunk),
            pl.BlockSpec((None, 2, d_rg), lambda i, j: (i // 2, 0, 0)),
        ] + _seq_weight_specs(sp),
        out_specs=[pl.BlockSpec((ROW_TILE, d), row), blk(tc, 2 * d_rg, out_chunk)],
        out_shape=[jax.ShapeDtypeStruct((m, d), F32), jax.ShapeDtypeStruct((ms, 2 * d_rg), BF16)],
        scratch_shapes=[
            pltpu.VMEM((ROW_TILE, d), BF16),
            pltpu.VMEM((ROW_TILE, d), F32),
        ] + _seq_scratch(seq_len, tc, d_rg),
        compiler_params=_params(dimension_semantics=("arbitrary", "arbitrary")),
        name="out_projection_ffn_seqmix",
    )(x, mix, mod, mod, mod, mod, norm_g.reshape(1, d), final_g.reshape(1, d), w_out, w_ff1, w_ff2,
      xrg, xrg, xrg, gy, u, v, h0, *_seq_weights(sp))


def kernel(x_prompt, x_sample, c, state_rglru, c_ctx, norm1_g, w_ada, b_ada, w_in, conv_w, conv_b,
           ga_w, ga_b, gi_w, gi_b, lru_lambda, sgu_w, sgu_b, w_out, norm2_g, w_ff1, w_ff2, final_g):
    b_ctx, l_ctx, d = x_prompt.shape
    b_lat, l_lat, _ = x_sample.shape
    depth = w_in.shape[0]
    d_rg = lru_lambda.shape[-1]
    assert b_lat + 1 <= MOD_ROWS
    assert (b_ctx * l_ctx) % ROW_TILE == 0 and l_lat % ROW_TILE == 0

    cond = jnp.concatenate(
        [c, c_ctx[None, :], jnp.zeros((MOD_ROWS - b_lat - 1, d), F32)], axis=0)
    grp_ctx = (b_lat, b_ctx * l_ctx)
    grp_lat = (0, l_lat)
    xp = x_prompt.reshape(b_ctx * l_ctx, d)
    xs = x_sample.reshape(b_lat * l_lat, d)
    h0_ctx = jnp.zeros((b_ctx, 2, d_rg), F32)
    ctx_states = []
    for l in range(depth):
        sp = {
            "conv_w": conv_w[l], "conv_b": conv_b[l].reshape(1, d_rg),
            "gw": jnp.concatenate([ga_w[l], gi_w[l]], axis=-1).astype(BF16),
            "gb": jnp.stack([ga_b[l], gi_b[l]], axis=1),
            "lam": lru_lambda[l],
            "sgu_w": sgu_w[l].astype(BF16), "sgu_b": sgu_b[l][:, :, None],
        }
        win = w_in[l].astype(BF16)
        last = l == depth - 1
        mod = _modulation(cond, w_ada[l], b_ada[l])
        h0_lat = state_rglru[:, l].astype(F32)

        gy_c, xrg_c, u_c, v_c, wff2, wout = _inproj_seq(
            xp, mod, grp_ctx, norm1_g[l], win, None, None, [w_ff2[l], w_out[l]])
        gy_l, xrg_l, u_l, v_l, mix_c, st, wff1 = _inproj_seq(
            xs, mod, grp_lat, norm1_g[l], win, (xrg_c, gy_c, u_c, v_c, h0_ctx, l_ctx), sp, [w_ff1[l]])
        xp, mix_l = _ffn_seq(xp, mix_c, mod, grp_ctx, norm2_g[l], final_g, wout, wff1, wff2, last,
                             FF_TILE_FUSED, (xrg_l, gy_l, u_l, v_l, h0_lat, l_lat), sp)
        xs = _ffn_pipelined(xs, mix_l, mod, grp_lat, norm2_g[l], final_g, wout, wff1, wff2, last,
                            FF_TILE_ALONE)
        ctx_states.append(st)
    y_prompt = xp.reshape(b_ctx, l_ctx, d)
    y_sample = xs.reshape(b_lat, l_lat, d)
    new_state = jnp.stack(ctx_states, axis=1).astype(x_prompt.dtype)
    return (y_prompt, y_sample, new_state)
```

```python
import functools

import jax
import jax.numpy as jnp
from jax import lax
from jax.experimental import pallas as pl
from jax.experimental.pallas import tpu as pltpu

F32 = jnp.float32
BF16 = jnp.bfloat16

N_RG_HEADS = 8
N_CH_HEADS = 8
CHUNK = 128
CONV_W = 4
CONV_LEFT = 2
RG_C = 8.0
N_MOD = 6
EPS = 1e-6

SUBLANES = 8
BF16_ROWS = 16
MOD_ROWS = 8
VMEM_LIMIT_BYTES = 62 * 1024 * 1024

ROW_TILE = 512
IN_ROW_TILE = 256
CAST_ITEM_BYTES = 1 << 19
FF_TILE_FUSED = 512
FF_TILE_ALONE = 1024
FF_PIECE = 256
W2_PARTS = 4
MOD_COL_TILE = 2048
PROJ_PIECE = 256
N_SEG = SUBLANES
HALO = SUBLANES


def _rms(x):
    return x * lax.rsqrt(jnp.mean(x * x, axis=-1, keepdims=True) + EPS)


def _params(**kw):
    return pltpu.CompilerParams(vmem_limit_bytes=VMEM_LIMIT_BYTES, **kw)


def _spread(items, n_slots):
    slots = [[] for _ in range(n_slots)]
    for k, item in enumerate(items):
        slots[k * n_slots // len(items)].append(item)
    return slots


def _merge(a, b):
    keyed = [((k + 0.5) / len(a), 0, k, item) for k, item in enumerate(a)]
    keyed += [((k + 0.5) / len(b), 1, k, item) for k, item in enumerate(b)]
    return [entry[-1] for entry in sorted(keyed, key=lambda entry: entry[:3])]


def _mod_kernel(c_ref, w_ref, b_ref, o_ref):
    s = jax.nn.silu(c_ref[...]).astype(BF16)
    o_ref[...] = jnp.dot(s, w_ref[...].astype(BF16), preferred_element_type=F32) + b_ref[...]


def _modulation(cond, w_ada, b_ada):
    d = cond.shape[1]
    n = w_ada.shape[1]
    out = pl.pallas_call(
        _mod_kernel,
        grid=(n // MOD_COL_TILE,),
        in_specs=[
            pl.BlockSpec((MOD_ROWS, d), lambda j: (0, 0)),
            pl.BlockSpec((d, MOD_COL_TILE), lambda j: (0, j)),
            pl.BlockSpec((1, MOD_COL_TILE), lambda j: (0, j)),
        ],
        out_specs=pl.BlockSpec((MOD_ROWS, MOD_COL_TILE), lambda j: (0, j)),
        out_shape=jax.ShapeDtypeStruct((MOD_ROWS, n), F32),
        compiler_params=_params(dimension_semantics=("arbitrary",)),
        name="modulation",
    )(cond, w_ada, b_ada.reshape(1, n))
    return out.reshape(MOD_ROWS, N_MOD, 1, d)


def _mod_spec(which, grp, d, row_tile=ROW_TILE):
    mod_base, rows_per_mod = grp

    def index(i, *_):
        return (mod_base + (i * row_tile) // rows_per_mod, which, 0, 0)
    return pl.BlockSpec((None, None, 1, d), index)


def _cast_items(pairs):
    items = []
    for src, dst in pairs:
        rows = max(CAST_ITEM_BYTES // (4 * src.shape[1]), BF16_ROWS)
        for r in range(0, src.shape[0], rows):
            def run(src=src, dst=dst, part=slice(r, r + rows)):
                dst[part, :] = src[part, :].astype(BF16)
            items.append(run)
    return items


def _inproj_body(x_ref, g_ref, sc_ref, sh_ref, w_ref, gy_ref, xrg_ref, u_ref, v_ref, between=()):
    d_rg = gy_ref.shape[1]
    h = (_rms(x_ref[...]) * g_ref[...]) * (1.0 + sc_ref[...]) + sh_ref[...]
    hb = h.astype(BF16)
    outs = (gy_ref, xrg_ref, u_ref, v_ref)
    n_pieces = w_ref.shape[1] // PROJ_PIECE
    work = _spread(list(between), n_pieces) if between else [[]] * n_pieces
    for n in range(n_pieces):
        part = jnp.dot(hb, w_ref[:, n * PROJ_PIECE:(n + 1) * PROJ_PIECE], preferred_element_type=F32)
        dest, col = divmod(n * PROJ_PIECE, d_rg)
        if dest != 1:
            part = jax.nn.gelu(part)
        outs[dest][:, col:col + PROJ_PIECE] = part.astype(outs[dest].dtype)
        for item in work[n]:
            item()


def _inproj_specs(m, d, d_in, grp):
    d_rg = d_in // 4
    row = lambda i: (i, 0)
    in_specs = [
        pl.BlockSpec((IN_ROW_TILE, d), row),
        pl.BlockSpec((1, d), lambda i: (0, 0)),
        _mod_spec(1, grp, d, IN_ROW_TILE),
        _mod_spec(0, grp, d, IN_ROW_TILE),
        pl.BlockSpec((d, d_in), lambda i: (0, 0), pipeline_mode=pl.Buffered(1)),
    ]
    out_specs = [pl.BlockSpec((IN_ROW_TILE, d_rg), row)] * 4
    out_shape = [
        jax.ShapeDtypeStruct((m, d_rg), F32),
        jax.ShapeDtypeStruct((m, d_rg), F32),
        jax.ShapeDtypeStruct((m, d_rg), F32),
        jax.ShapeDtypeStruct((m, d_rg), BF16),
    ]
    return in_specs, out_specs, out_shape


def _scan_head(hd, reverse, edges, hf_rows, x_ref, gy_ref, cw_ref, cb_ref, gw_ref, gb_ref, lam_ref,
               mix_ref, hf_scr, carry_scr, xi_scr, hs_scr):
    tc, d_rg = x_ref.shape
    tl = tc // N_SEG
    head = d_rg // N_RG_HEADS
    lead = CONV_LEFT * N_SEG
    dirn = 1 if reverse else 0
    steps = range(tl - 1, -1, -1) if reverse else range(tl)
    segs = range(N_SEG - 1, -1, -1) if reverse else range(N_SEG)
    cols = slice(hd * head, (hd + 1) * head)
    xi = xi_scr.at[hd]
    held = {}

    def step_rows(t):
        return xi[lead + t * N_SEG:lead + (t + 1) * N_SEG, :]

    def conv_and_gate_matmul():
        sub = lax.broadcasted_iota(jnp.int32, (N_SEG, head), 0)
        for sg in range(N_SEG):
            xi[pl.ds(lead + sg, tl, stride=N_SEG), :] = x_ref[sg * tl:(sg + 1) * tl, cols]
        for k in range(CONV_LEFT):
            if edges is None:
                edge = 0.0
            else:
                row = HALO - CONV_LEFT + k
                edge = jnp.where(edges[0], 0.0, edges[1][row:row + 1, cols])
            prev_step = pltpu.roll(step_rows(tl - CONV_LEFT + k), 1, axis=0)
            xi[k * N_SEG:(k + 1) * N_SEG, :] = jnp.where(sub == 0, edge, prev_step)
        for k in range(CONV_W - 1 - CONV_LEFT):
            edge = 0.0 if edges is None else jnp.where(edges[2], 0.0, edges[3][k:k + 1, cols])
            next_step = pltpu.roll(step_rows(k), N_SEG - 1, axis=0)
            xi[lead + (tl + k) * N_SEG:lead + (tl + k + 1) * N_SEG, :] = jnp.where(
                sub == N_SEG - 1, edge, next_step)
        xc = cb_ref[:, cols]
        for k in range(CONV_W):
            xc = xc + cw_ref[k:k + 1, cols] * xi[k * N_SEG:k * N_SEG + tc, :]
        held["xc"] = xc
        held["g"] = jnp.dot(xc.astype(BF16), gw_ref[dirn, hd], preferred_element_type=F32)

    def gates_and_scan():
        sub = lax.broadcasted_iota(jnp.int32, (N_SEG, head), 0)
        xc, g = held["xc"], held["g"]
        r = jax.nn.sigmoid(g[:, :head] + gb_ref[dirn, 0:1, cols])
        i = jax.nn.sigmoid(g[:, head:] + gb_ref[dirn, 1:2, cols])
        log_a = r * (-RG_C * jax.nn.softplus(-lam_ref[dirn:dirn + 1, cols]))
        a = jnp.exp(log_a)
        m2 = jnp.maximum(-jnp.tanh(log_a) * (a * a + 1.0), 0.0)
        mult = jnp.where(m2 > 0.0, m2 * lax.rsqrt(m2), 0.0)
        bx = mult * (i * xc)

        a3 = a.reshape(tl, N_SEG, head)
        b3 = bx.reshape(tl, N_SEG, head)
        hloc = [None] * tl
        prod = [None] * tl
        hprev = pprev = None
        for t in steps:
            if hprev is None:
                hprev, pprev = b3[t], a3[t]
            else:
                hprev, pprev = a3[t] * hprev + b3[t], a3[t] * pprev
            hloc[t], prod[t] = hprev, pprev
        state = carry_scr[:, cols]
        init = jnp.zeros((N_SEG, head), F32)
        for sg in segs:
            init = jnp.where(sub == sg, state, init)
            state = hprev[sg:sg + 1, :] + pprev[sg:sg + 1, :] * state
        carry_scr[:, cols] = state
        hfull = jnp.concatenate([hloc[t] + prod[t] * init for t in range(tl)], axis=0)
        if not reverse:
            hf_scr[hf_rows, cols] = hfull
        else:
            hs = hs_scr.at[hd]
            hs[...] = hf_scr[hf_rows, cols] + hfull
            for sg in range(N_SEG):
                rows = slice(sg * tl, (sg + 1) * tl)
                hsum = hs[pl.ds(sg, tl, stride=N_SEG), :]
                mix_ref[rows, cols] = (gy_ref[rows, cols] * hsum).astype(BF16)

    return [conv_and_gate_matmul, gates_and_scan]


def _sgu_head(hd, u_ref, v_ref, sw_ref, sb_ref, mix_ref):
    tc, d_ch = u_ref.shape
    ch_head = d_ch // N_CH_HEADS
    d_rg = mix_ref.shape[1] - d_ch
    cols = slice(hd * ch_head, (hd + 1) * ch_head)
    for n in range(tc // CHUNK):
        rows = slice(n * CHUNK, (n + 1) * CHUNK)
        mixed = jnp.dot(sw_ref[hd], v_ref[rows, cols], preferred_element_type=F32) + sb_ref[hd]
        mix_ref[rows, d_rg + hd * ch_head:d_rg + (hd + 1) * ch_head] = (
            u_ref[rows, cols] * mixed).astype(BF16)


def _seq_items(reverse, edges, hf_rows, x_ref, gy_ref, u_ref, v_ref,
               cw_ref, cb_ref, gw_ref, gb_ref, lam_ref, sw_ref, sb_ref,
               mix_ref, hf_scr, carry_scr, xi_scr, hs_scr):
    items = []
    for hd in range(N_RG_HEADS):
        items += _scan_head(hd, reverse, edges, hf_rows, x_ref, gy_ref, cw_ref, cb_ref, gw_ref, gb_ref,
                            lam_ref, mix_ref, hf_scr, carry_scr, xi_scr, hs_scr)
        if reverse:
            items.append(functools.partial(_sgu_head, hd, u_ref, v_ref, sw_ref, sb_ref, mix_ref))
    return items


def _seq_weight_specs(sp):
    full = lambda a: pl.BlockSpec(a.shape, lambda *_: (0,) * a.ndim, pipeline_mode=pl.Buffered(1))
    return [full(sp[k]) for k in ("conv_w", "conv_b", "gw", "gb", "lam", "sgu_w", "sgu_b")]


def _seq_weights(sp):
    return [sp[k] for k in ("conv_w", "conv_b", "gw", "gb", "lam", "sgu_w", "sgu_b")]


def _seq_scratch(seq_len, tc, d_rg):
    head = d_rg // N_RG_HEADS
    return [
        pltpu.VMEM((seq_len, d_rg), F32),
        pltpu.VMEM((1, d_rg), F32),
        pltpu.VMEM((N_RG_HEADS, tc + (CONV_W - 1) * N_SEG, head), F32),
        pltpu.VMEM((N_RG_HEADS, tc, head), F32),
    ]


N_INPROJ_IN = 5
N_INPROJ_OUT = 4
N_SEQ_IN = 12
N_SEQ_OUT = 2
N_SEQ_SCRATCH = 4


def _inproj_kernel(*refs, n_cast, with_seq):
    s = pl.program_id(0)
    refs = list(refs)
    take = lambda k: [refs.pop(0) for _ in range(k)]
    proj_in = take(N_INPROJ_IN)
    seq_in = take(N_SEQ_IN) if with_seq else None
    cast_src = take(n_cast)
    proj_out = take(N_INPROJ_OUT)
    seq_out = take(N_SEQ_OUT) if with_seq else None
    cast_dst = take(n_cast)
    casts = _cast_items(list(zip(cast_src, cast_dst)))
    if not with_seq:
        _inproj_body(*proj_in, *proj_out, between=casts)
        return

    x_ref, gy_ref, u_ref, v_ref, h0_ref = seq_in[:5]
    weights = seq_in[5:]
    mix_ref, st_ref = seq_out
    hf_scr, carry_scr, xi_scr, hs_scr = take(N_SEQ_SCRATCH)
    tc = x_ref.shape[0]

    def direction(reverse):
        dirn = 1 if reverse else 0
        work = _seq_items(reverse, None, slice(0, tc), x_ref, gy_ref, u_ref, v_ref, *weights,
                          mix_ref, hf_scr, carry_scr, xi_scr, hs_scr)

        def first():
            carry_scr[...] = h0_ref[dirn:dirn + 1, :]
            work[0]()

        def last():
            work[-1]()
            st_ref[dirn:dirn + 1, :] = carry_scr[...]

        items = [first] + work[1:-1] + [last]
        _inproj_body(*proj_in, *proj_out, between=_merge(items, casts) if casts else items)

    @pl.when(s % 2 == 0)
    def _():
        direction(False)

    @pl.when(s % 2 == 1)
    def _():
        direction(True)


def _inproj_seq(x, mod, grp, norm_g, w_in, seq, sp, to_cast):
    m, d = x.shape
    n_steps = m // IN_ROW_TILE
    in_specs, out_specs, out_shape = _inproj_specs(m, d, w_in.shape[1], grp)
    args = [x, norm_g.reshape(1, d), mod, mod, w_in]
    scratch = []
    if seq is not None:
        xrg, gy, u, v, h0, seq_len = seq
        ms, d_rg = xrg.shape
        assert 2 * (ms // seq_len) == n_steps
        blk = lambda i: (i // 2, 0)
        per_seq = pl.BlockSpec((None, 2, d_rg), lambda i: (i // 2, 0, 0))
        in_specs = in_specs + [pl.BlockSpec((seq_len, d_rg), blk)] * 4 + [per_seq] + _seq_weight_specs(sp)
        args = args + [xrg, gy, u, v, h0] + _seq_weights(sp)
        out_specs = out_specs + [pl.BlockSpec((seq_len, 2 * d_rg), blk), per_seq]
        out_shape = out_shape + [jax.ShapeDtypeStruct((ms, 2 * d_rg), BF16),
                                 jax.ShapeDtypeStruct((ms // seq_len, 2, d_rg), F32)]
        scratch = _seq_scratch(seq_len, seq_len, d_rg)
    for w in to_cast:
        rows, cols = w.shape[0] // n_steps, w.shape[1]
        assert rows * n_steps == w.shape[0] and rows % BF16_ROWS == 0
        in_specs = in_specs + [pl.BlockSpec((rows, cols), lambda i: (i, 0))]
        args = args + [w]
    out_specs = out_specs + [pl.BlockSpec((w.shape[0] // n_steps, w.shape[1]), lambda i: (i, 0))
                             for w in to_cast]
    out_shape = out_shape + [jax.ShapeDtypeStruct(w.shape, BF16) for w in to_cast]
    return pl.pallas_call(
        functools.partial(_inproj_kernel, n_cast=len(to_cast), with_seq=seq is not None),
        grid=(n_steps,),
        in_specs=in_specs,
        out_specs=out_specs,
        out_shape=out_shape,
        scratch_shapes=scratch,
        compiler_params=_params(dimension_semantics=("arbitrary",)),
        name="in_projection" if seq is None else "in_projection_seqmix",
    )(*args)


def _ffn_first(x_ref, mix_ref, g1_ref, sc_ref, sh_ref, ng_ref, wo_ref, o_ref, h2_scr, acc_scr):
    mix = jnp.dot(mix_ref[...], wo_ref[...], preferred_element_type=F32)
    x1 = x_ref[...] + g1_ref[...] * mix
    o_ref[...] = x1
    h2 = (_rms(x1) * ng_ref[...]) * (1.0 + sc_ref[...]) + sh_ref[...]
    h2_scr[...] = h2.astype(BF16)
    acc_scr[...] = jnp.zeros_like(acc_scr)


def _ffn_step(w1_ref, w2_ref, h2_scr, acc_scr, between=()):
    n_w1 = w1_ref.shape[1] // FF_PIECE
    w2_parts = w2_ref.shape[1] // FF_PIECE if between else W2_PARTS
    n_pieces = n_w1 + w2_parts
    work = _spread(list(between), n_pieces) if between else [[]] * n_pieces
    part_d = w2_ref.shape[1] // w2_parts

    h2 = h2_scr[...]
    acts = []
    for n in range(n_w1):
        for item in work[n]:
            item()
        a = jnp.dot(h2, w1_ref[:, n * FF_PIECE:(n + 1) * FF_PIECE], preferred_element_type=F32)
        acts.append(jnp.square(jnp.maximum(a, 0.0)).astype(BF16))
    act = jnp.concatenate(acts, axis=1)
    for n in range(w2_parts):
        for item in work[n_w1 + n]:
            item()
        cols = slice(n * part_d, (n + 1) * part_d)
        acc_scr[:, cols] += jnp.dot(act, w2_ref[:, cols], preferred_element_type=F32)


def _ffn_last(g2_ref, fg_ref, o_ref, acc_scr, final_norm):
    x2 = o_ref[...] + g2_ref[...] * acc_scr[...]
    if final_norm:
        x2 = _rms(x2) * fg_ref[...]
    o_ref[...] = x2


def _ffn_kernel(x_ref, mix_ref, g1_ref, sc_ref, sh_ref, g2_ref, ng_ref, fg_ref,
                wo_ref, w1_ref, w2_ref, o_ref, h2_scr, acc_scr, *, final_norm):
    j = pl.program_id(1)

    @pl.when(j == 0)
    def _():
        _ffn_first(x_ref, mix_ref, g1_ref, sc_ref, sh_ref, ng_ref, wo_ref, o_ref, h2_scr, acc_scr)

    _ffn_step(w1_ref, w2_ref, h2_scr, acc_scr)

    @pl.when(j == pl.num_programs(1) - 1)
    def _():
        _ffn_last(g2_ref, fg_ref, o_ref, acc_scr, final_norm)


def _ffn_seq_kernel(x_ref, mix_ref, g1_ref, sc_ref, sh_ref, g2_ref, ng_ref, fg_ref,
                    wo_ref, w1_ref, w2_ref,
                    sx_ref, sprev_ref, snext_ref, gy_ref, u_ref, v_ref, h0_ref,
                    cw_ref, cb_ref, gw_ref, gb_ref, lam_ref, sw_ref, sb_ref,
                    o_ref, smix_ref,
                    h2_scr, acc_scr, hf_scr, carry_scr, xi_scr, hs_scr, *, final_norm):
    i = pl.program_id(0)
    j = pl.program_id(1)
    n_chunks = pl.num_programs(1)
    tc = sx_ref.shape[0]
    backward = i % 2 == 1
    jj = jnp.where(backward, n_chunks - 1 - j, j)
    hf_rows = pl.ds(pl.multiple_of(jj * tc, tc), tc)
    edges = (jj == 0, sprev_ref, jj == n_chunks - 1, snext_ref)

    @pl.when(j == 0)
    def _():
        _ffn_first(x_ref, mix_ref, g1_ref, sc_ref, sh_ref, ng_ref, wo_ref, o_ref, h2_scr, acc_scr)
        carry_scr[...] = h0_ref[pl.ds(i % 2, 1), :]

    def both(reverse):
        items = _seq_items(reverse, edges, hf_rows, sx_ref, gy_ref, u_ref, v_ref,
                           cw_ref, cb_ref, gw_ref, gb_ref, lam_ref, sw_ref, sb_ref,
                           smix_ref, hf_scr, carry_scr, xi_scr, hs_scr)
        _ffn_step(w1_ref, w2_ref, h2_scr, acc_scr, between=items)

    @pl.when(jnp.logical_not(backward))
    def _():
        both(False)

    @pl.when(backward)
    def _():
        both(True)

    @pl.when(j == n_chunks - 1)
    def _():
        _ffn_last(g2_ref, fg_ref, o_ref, acc_scr, final_norm)


def _ffn(x, mix, mod, grp, norm_g, final_g, w_out, w_ff1, w_ff2, final_norm, ff_tile, seq, sp):
    m, d = x.shape
    d_mix = mix.shape[1]
    n_steps = w_ff1.shape[1] // ff_tile
    row = lambda i, j: (i, 0)
    vec = pl.BlockSpec((1, d), lambda i, j: (0, 0))
    in_specs = [
        pl.BlockSpec((ROW_TILE, d), row),
        pl.BlockSpec((ROW_TILE, d_mix), row),
        _mod_spec(2, grp, d),
        _mod_spec(4, grp, d),
        _mod_spec(3, grp, d),
        _mod_spec(5, grp, d),
        vec,
        vec,
        pl.BlockSpec((d_mix, d), lambda i, j: (0, 0), pipeline_mode=pl.Buffered(1)),
        pl.BlockSpec((d, ff_tile), lambda i, j: (0, j)),
        pl.BlockSpec((ff_tile, d), lambda i, j: (j, 0)),
    ]
    args = [x, mix, mod, mod, mod, mod, norm_g.reshape(1, d), final_g.reshape(1, d), w_out, w_ff1, w_ff2]
    out_specs = [pl.BlockSpec((ROW_TILE, d), row)]
    out_shape = [jax.ShapeDtypeStruct((m, d), F32)]
    scratch = [
        pltpu.VMEM((ROW_TILE, d), BF16),
        pltpu.VMEM((ROW_TILE, d), F32),
    ]
    if seq is None:
        kern, name = _ffn_kernel, "out_projection_ffn"
    else:
        kern, name = _ffn_seq_kernel, "out_projection_ffn_seqmix"
        xrg, gy, u, v, h0, seq_len = seq
        ms, d_rg = xrg.shape
        n_seq = ms // seq_len
        tc = seq_len // n_steps
        assert 2 * n_seq == m // ROW_TILE and tc % (N_SEG * SUBLANES) == 0 and tc % CHUNK == 0
        per_tile = tc // HALO
        n_halo = ms // HALO

        def chunk(i, j):
            return (i // 2) * n_steps + jnp.where(i % 2 == 1, n_steps - 1 - j, j)

        def out_chunk(i, j):
            return ((i // 2) * n_steps + jnp.where(i % 2 == 1, n_steps - 1 - j, n_steps - 1), 0)

        blk = lambda rows, cols, index: pl.BlockSpec((rows, cols), index)
        in_specs = in_specs + [
            blk(tc, d_rg, lambda i, j: (chunk(i, j), 0)),
            blk(HALO, d_rg, lambda i, j: (jnp.maximum(chunk(i, j) * per_tile - 1, 0), 0)),
            blk(HALO, d_rg, lambda i, j: (jnp.minimum((chunk(i, j) + 1) * per_tile, n_halo - 1), 0)),
            blk(tc, d_rg, out_chunk),
            blk(tc, d_rg, out_chunk),
            blk(tc, d_rg, out_chunk),
            pl.BlockSpec((None, 2, d_rg), lambda i, j: (i // 2, 0, 0)),
        ] + _seq_weight_specs(sp)
        args = args + [xrg, xrg, xrg, gy, u, v, h0] + _seq_weights(sp)
        out_specs = out_specs + [blk(tc, 2 * d_rg, out_chunk)]
        out_shape = out_shape + [jax.ShapeDtypeStruct((ms, 2 * d_rg), BF16)]
        scratch = scratch + _seq_scratch(seq_len, tc, d_rg)
    return pl.pallas_call(
        functools.partial(kern, final_norm=final_norm),
        grid=(m // ROW_TILE, n_steps),
        in_specs=in_specs,
        out_specs=out_specs,
        out_shape=out_shape,
        scratch_shapes=scratch,
        compiler_params=_params(dimension_semantics=("arbitrary", "arbitrary")),
        name=name,
    )(*args)


def kernel(x_prompt, x_sample, c, state_rglru, c_ctx, norm1_g, w_ada, b_ada, w_in, conv_w, conv_b,
           ga_w, ga_b, gi_w, gi_b, lru_lambda, sgu_w, sgu_b, w_out, norm2_g, w_ff1, w_ff2, final_g):
    b_ctx, l_ctx, d = x_prompt.shape
    b_lat, l_lat, _ = x_sample.shape
    depth = w_in.shape[0]
    d_rg = lru_lambda.shape[-1]
    assert b_lat + 1 <= MOD_ROWS
    assert (b_ctx * l_ctx) % ROW_TILE == 0 and l_lat % ROW_TILE == 0

    cond = jnp.concatenate(
        [c, c_ctx[None, :], jnp.zeros((MOD_ROWS - b_lat - 1, d), F32)], axis=0)
    grp_ctx = (b_lat, b_ctx * l_ctx)
    grp_lat = (0, l_lat)
    xp = x_prompt.reshape(b_ctx * l_ctx, d)
    xs = x_sample.reshape(b_lat * l_lat, d)
    h0_ctx = jnp.zeros((b_ctx, 2, d_rg), F32)
    ctx_states = []
    for l in range(depth):
        sp = {
            "conv_w": conv_w[l], "conv_b": conv_b[l].reshape(1, d_rg),
            "gw": jnp.concatenate([ga_w[l], gi_w[l]], axis=-1).astype(BF16),
            "gb": jnp.stack([ga_b[l], gi_b[l]], axis=1),
            "lam": lru_lambda[l],
            "sgu_w": sgu_w[l].astype(BF16), "sgu_b": sgu_b[l][:, :, None],
        }
        win = w_in[l].astype(BF16)
        last = l == depth - 1
        mod = _modulation(cond, w_ada[l], b_ada[l])
        h0_lat = state_rglru[:, l].astype(F32)

        gy_c, xrg_c, u_c, v_c, wff2, wout = _inproj_seq(
            xp, mod, grp_ctx, norm1_g[l], win, None, None, [w_ff2[l], w_out[l]])
        gy_l, xrg_l, u_l, v_l, mix_c, st, wff1 = _inproj_seq(
            xs, mod, grp_lat, norm1_g[l], win, (xrg_c, gy_c, u_c, v_c, h0_ctx, l_ctx), sp, [w_ff1[l]])
        xp, mix_l = _ffn(xp, mix_c, mod, grp_ctx, norm2_g[l], final_g, wout, wff1, wff2, last,
                         FF_TILE_FUSED, (xrg_l, gy_l, u_l, v_l, h0_lat, l_lat), sp)
        xs, = _ffn(xs, mix_l, mod, grp_lat, norm2_g[l], final_g, wout, wff1, wff2, last,
                   FF_TILE_ALONE, None, None)
        ctx_states.append(st)
    y_prompt = xp.reshape(b_ctx, l_ctx, d)
    y_sample = xs.reshape(b_lat, l_lat, d)
    new_state = jnp.stack(ctx_states, axis=1).astype(x_prompt.dtype)
    return (y_prompt, y_sample, new_state)
```

```python
import functools

import jax
import jax.numpy as jnp
from jax import lax
from jax.experimental import pallas as pl
from jax.experimental.pallas import tpu as pltpu

F32 = jnp.float32
BF16 = jnp.bfloat16

N_RG_HEADS = 8
N_CH_HEADS = 8
CHUNK = 128
CONV_W = 4
CONV_LEFT = 2
RG_C = 8.0
N_MOD = 6
EPS = 1e-6

SUBLANES = 8
BF16_ROWS = 16
MOD_ROWS = 8
VMEM_LIMIT_BYTES = 62 * 1024 * 1024

ROW_TILE = 512
IN_ROW_TILE = 256
CAST_ITEM_BYTES = 1 << 19
FF_TILE_FUSED = 512
FF_TILE_ALONE = 1024
FF_PIECE = 256
W2_PARTS = 4
MOD_COL_TILE = 1024
PROJ_PIECE = 256
N_SEG = SUBLANES
HALO = SUBLANES


def _rms(x):
    return x * lax.rsqrt(jnp.mean(x * x, axis=-1, keepdims=True) + EPS)


def _params(**kw):
    return pltpu.CompilerParams(vmem_limit_bytes=VMEM_LIMIT_BYTES, **kw)


def _spread(items, n_slots):
    slots = [[] for _ in range(n_slots)]
    for k, item in enumerate(items):
        slots[k * n_slots // len(items)].append(item)
    return slots


def _merge(a, b):
    keyed = [((k + 0.5) / len(a), 0, k, item) for k, item in enumerate(a)]
    keyed += [((k + 0.5) / len(b), 1, k, item) for k, item in enumerate(b)]
    return [entry[-1] for entry in sorted(keyed, key=lambda entry: entry[:3])]


def _mod_kernel(c_ref, w_ref, b_ref, o_ref):
    s = jax.nn.silu(c_ref[...]).astype(BF16)
    o_ref[...] = jnp.dot(s, w_ref[...].astype(BF16), preferred_element_type=F32) + b_ref[...]


def _modulation(cond, w_ada, b_ada):
    d = cond.shape[1]
    n = w_ada.shape[1]
    out = pl.pallas_call(
        _mod_kernel,
        grid=(n // MOD_COL_TILE,),
        in_specs=[
            pl.BlockSpec((MOD_ROWS, d), lambda j: (0, 0)),
            pl.BlockSpec((d, MOD_COL_TILE), lambda j: (0, j)),
            pl.BlockSpec((1, MOD_COL_TILE), lambda j: (0, j)),
        ],
        out_specs=pl.BlockSpec((MOD_ROWS, MOD_COL_TILE), lambda j: (0, j)),
        out_shape=jax.ShapeDtypeStruct((MOD_ROWS, n), F32),
        compiler_params=_params(dimension_semantics=("arbitrary",)),
        name="modulation",
    )(cond, w_ada, b_ada.reshape(1, n))
    return out.reshape(MOD_ROWS, N_MOD, 1, d)


def _mod_spec(which, grp, d, row_tile=ROW_TILE):
    mod_base, rows_per_mod = grp

    def index(i, *_):
        return (mod_base + (i * row_tile) // rows_per_mod, which, 0, 0)
    return pl.BlockSpec((None, None, 1, d), index)


def _cast_items(pairs):
    items = []
    for src, dst in pairs:
        rows = max(CAST_ITEM_BYTES // (4 * src.shape[1]), BF16_ROWS)
        for r in range(0, src.shape[0], rows):
            def run(src=src, dst=dst, part=slice(r, r + rows)):
                dst[part, :] = src[part, :].astype(BF16)
            items.append(run)
    return items


def _inproj_body(x_ref, g_ref, sc_ref, sh_ref, w_ref, gy_ref, xrg_ref, u_ref, v_ref, between=()):
    d_rg = gy_ref.shape[1]
    h = (_rms(x_ref[...]) * g_ref[...]) * (1.0 + sc_ref[...]) + sh_ref[...]
    hb = h.astype(BF16)
    outs = (gy_ref, xrg_ref, u_ref, v_ref)
    n_pieces = w_ref.shape[1] // PROJ_PIECE
    work = _spread(list(between), n_pieces) if between else [[]] * n_pieces
    for n in range(n_pieces):
        part = jnp.dot(hb, w_ref[:, n * PROJ_PIECE:(n + 1) * PROJ_PIECE], preferred_element_type=F32)
        dest, col = divmod(n * PROJ_PIECE, d_rg)
        if dest != 1:
            part = jax.nn.gelu(part)
        outs[dest][:, col:col + PROJ_PIECE] = part.astype(outs[dest].dtype)
        for item in work[n]:
            item()


def _inproj_specs(m, d, d_in, grp):
    d_rg = d_in // 4
    row = lambda i: (i, 0)
    in_specs = [
        pl.BlockSpec((IN_ROW_TILE, d), row),
        pl.BlockSpec((1, d), lambda i: (0, 0)),
        _mod_spec(1, grp, d, IN_ROW_TILE),
        _mod_spec(0, grp, d, IN_ROW_TILE),
        pl.BlockSpec((d, d_in), lambda i: (0, 0), pipeline_mode=pl.Buffered(1)),
    ]
    out_specs = [pl.BlockSpec((IN_ROW_TILE, d_rg), row)] * 4
    out_shape = [
        jax.ShapeDtypeStruct((m, d_rg), F32),
        jax.ShapeDtypeStruct((m, d_rg), F32),
        jax.ShapeDtypeStruct((m, d_rg), F32),
        jax.ShapeDtypeStruct((m, d_rg), BF16),
    ]
    return in_specs, out_specs, out_shape


def _scan_head(hd, reverse, edges, hf_rows, x_ref, gy_ref, cw_ref, cb_ref, gw_ref, gb_ref, lam_ref,
               mix_ref, hf_scr, carry_scr, xi_scr, hs_scr):
    tc, d_rg = x_ref.shape
    tl = tc // N_SEG
    head = d_rg // N_RG_HEADS
    lead = CONV_LEFT * N_SEG
    dirn = 1 if reverse else 0
    steps = range(tl - 1, -1, -1) if reverse else range(tl)
    segs = range(N_SEG - 1, -1, -1) if reverse else range(N_SEG)
    cols = slice(hd * head, (hd + 1) * head)
    xi = xi_scr.at[hd]
    held = {}

    def step_rows(t):
        return xi[lead + t * N_SEG:lead + (t + 1) * N_SEG, :]

    def conv_and_gate_matmul():
        sub = lax.broadcasted_iota(jnp.int32, (N_SEG, head), 0)
        for sg in range(N_SEG):
            xi[pl.ds(lead + sg, tl, stride=N_SEG), :] = x_ref[sg * tl:(sg + 1) * tl, cols]
        for k in range(CONV_LEFT):
            if edges is None:
                edge = 0.0
            else:
                row = HALO - CONV_LEFT + k
                edge = jnp.where(edges[0], 0.0, edges[1][row:row + 1, cols])
            prev_step = pltpu.roll(step_rows(tl - CONV_LEFT + k), 1, axis=0)
            xi[k * N_SEG:(k + 1) * N_SEG, :] = jnp.where(sub == 0, edge, prev_step)
        for k in range(CONV_W - 1 - CONV_LEFT):
            edge = 0.0 if edges is None else jnp.where(edges[2], 0.0, edges[3][k:k + 1, cols])
            next_step = pltpu.roll(step_rows(k), N_SEG - 1, axis=0)
            xi[lead + (tl + k) * N_SEG:lead + (tl + k + 1) * N_SEG, :] = jnp.where(
                sub == N_SEG - 1, edge, next_step)
        xc = cb_ref[:, cols]
        for k in range(CONV_W):
            xc = xc + cw_ref[k:k + 1, cols] * xi[k * N_SEG:k * N_SEG + tc, :]
        held["xc"] = xc
        held["g"] = jnp.dot(xc.astype(BF16), gw_ref[dirn, hd], preferred_element_type=F32)

    def gates_and_scan():
        sub = lax.broadcasted_iota(jnp.int32, (N_SEG, head), 0)
        xc, g = held["xc"], held["g"]
        r = jax.nn.sigmoid(g[:, :head] + gb_ref[dirn, 0:1, cols])
        i = jax.nn.sigmoid(g[:, head:] + gb_ref[dirn, 1:2, cols])
        log_a = r * (-RG_C * jax.nn.softplus(-lam_ref[dirn:dirn + 1, cols]))
        a = jnp.exp(log_a)
        m2 = jnp.maximum(-jnp.tanh(log_a) * (a * a + 1.0), 0.0)
        mult = jnp.where(m2 > 0.0, m2 * lax.rsqrt(m2), 0.0)
        bx = mult * (i * xc)

        a3 = a.reshape(tl, N_SEG, head)
        b3 = bx.reshape(tl, N_SEG, head)
        hloc = [None] * tl
        prod = [None] * tl
        hprev = pprev = None
        for t in steps:
            if hprev is None:
                hprev, pprev = b3[t], a3[t]
            else:
                hprev, pprev = a3[t] * hprev + b3[t], a3[t] * pprev
            hloc[t], prod[t] = hprev, pprev
        state = carry_scr[:, cols]
        init = jnp.zeros((N_SEG, head), F32)
        for sg in segs:
            init = jnp.where(sub == sg, state, init)
            state = hprev[sg:sg + 1, :] + pprev[sg:sg + 1, :] * state
        carry_scr[:, cols] = state
        hfull = jnp.concatenate([hloc[t] + prod[t] * init for t in range(tl)], axis=0)
        if not reverse:
            hf_scr[hf_rows, cols] = hfull
        else:
            hs = hs_scr.at[hd]
            hs[...] = hf_scr[hf_rows, cols] + hfull
            for sg in range(N_SEG):
                rows = slice(sg * tl, (sg + 1) * tl)
                hsum = hs[pl.ds(sg, tl, stride=N_SEG), :]
                mix_ref[rows, cols] = (gy_ref[rows, cols] * hsum).astype(BF16)

    return [conv_and_gate_matmul, gates_and_scan]


def _sgu_head(hd, u_ref, v_ref, sw_ref, sb_ref, mix_ref):
    tc, d_ch = u_ref.shape
    ch_head = d_ch // N_CH_HEADS
    d_rg = mix_ref.shape[1] - d_ch
    cols = slice(hd * ch_head, (hd + 1) * ch_head)
    for n in range(tc // CHUNK):
        rows = slice(n * CHUNK, (n + 1) * CHUNK)
        mixed = jnp.dot(sw_ref[hd], v_ref[rows, cols], preferred_element_type=F32) + sb_ref[hd]
        mix_ref[rows, d_rg + hd * ch_head:d_rg + (hd + 1) * ch_head] = (
            u_ref[rows, cols] * mixed).astype(BF16)


def _seq_items(reverse, edges, hf_rows, x_ref, gy_ref, u_ref, v_ref,
               cw_ref, cb_ref, gw_ref, gb_ref, lam_ref, sw_ref, sb_ref,
               mix_ref, hf_scr, carry_scr, xi_scr, hs_scr, fine):
    def together(parts):
        def run():
            for part in parts:
                part()
        return run

    items = []
    for hd in range(N_RG_HEADS):
        parts = _scan_head(hd, reverse, edges, hf_rows, x_ref, gy_ref, cw_ref, cb_ref, gw_ref, gb_ref,
                           lam_ref, mix_ref, hf_scr, carry_scr, xi_scr, hs_scr)
        if reverse:
            parts.append(functools.partial(_sgu_head, hd, u_ref, v_ref, sw_ref, sb_ref, mix_ref))
        items += parts if fine else [together(parts)]
    return items


def _seq_weight_specs(sp):
    full = lambda a: pl.BlockSpec(a.shape, lambda *_: (0,) * a.ndim, pipeline_mode=pl.Buffered(1))
    return [full(sp[k]) for k in ("conv_w", "conv_b", "gw", "gb", "lam", "sgu_w", "sgu_b")]


def _seq_weights(sp):
    return [sp[k] for k in ("conv_w", "conv_b", "gw", "gb", "lam", "sgu_w", "sgu_b")]


def _seq_scratch(seq_len, tc, d_rg):
    head = d_rg // N_RG_HEADS
    return [
        pltpu.VMEM((seq_len, d_rg), F32),
        pltpu.VMEM((1, d_rg), F32),
        pltpu.VMEM((N_RG_HEADS, tc + (CONV_W - 1) * N_SEG, head), F32),
        pltpu.VMEM((N_RG_HEADS, tc, head), F32),
    ]


N_INPROJ_IN = 5
N_INPROJ_OUT = 4
N_SEQ_IN = 12
N_SEQ_OUT = 2
N_SEQ_SCRATCH = 4


def _inproj_kernel(*refs, n_cast, with_seq):
    s = pl.program_id(0)
    refs = list(refs)
    take = lambda k: [refs.pop(0) for _ in range(k)]
    proj_in = take(N_INPROJ_IN)
    seq_in = take(N_SEQ_IN) if with_seq else None
    cast_src = take(n_cast)
    proj_out = take(N_INPROJ_OUT)
    seq_out = take(N_SEQ_OUT) if with_seq else None
    cast_dst = take(n_cast)
    casts = _cast_items(list(zip(cast_src, cast_dst)))
    if not with_seq:
        _inproj_body(*proj_in, *proj_out, between=casts)
        return

    x_ref, gy_ref, u_ref, v_ref, h0_ref = seq_in[:5]
    weights = seq_in[5:]
    mix_ref, st_ref = seq_out
    hf_scr, carry_scr, xi_scr, hs_scr = take(N_SEQ_SCRATCH)
    tc = x_ref.shape[0]

    def direction(reverse):
        dirn = 1 if reverse else 0
        work = _seq_items(reverse, None, slice(0, tc), x_ref, gy_ref, u_ref, v_ref, *weights,
                          mix_ref, hf_scr, carry_scr, xi_scr, hs_scr, fine=False)

        def first():
            carry_scr[...] = h0_ref[dirn:dirn + 1, :]
            work[0]()

        def last():
            work[-1]()
            st_ref[dirn:dirn + 1, :] = carry_scr[...]

        items = [first] + work[1:-1] + [last]
        _inproj_body(*proj_in, *proj_out, between=_merge(items, casts) if casts else items)

    @pl.when(s % 2 == 0)
    def _():
        direction(False)

    @pl.when(s % 2 == 1)
    def _():
        direction(True)


def _inproj_seq(x, mod, grp, norm_g, w_in, seq, sp, to_cast):
    m, d = x.shape
    n_steps = m // IN_ROW_TILE
    in_specs, out_specs, out_shape = _inproj_specs(m, d, w_in.shape[1], grp)
    args = [x, norm_g.reshape(1, d), mod, mod, w_in]
    scratch = []
    if seq is not None:
        xrg, gy, u, v, h0, seq_len = seq
        ms, d_rg = xrg.shape
        assert 2 * (ms // seq_len) == n_steps
        blk = lambda i: (i // 2, 0)
        per_seq = pl.BlockSpec((None, 2, d_rg), lambda i: (i // 2, 0, 0))
        in_specs = in_specs + [pl.BlockSpec((seq_len, d_rg), blk)] * 4 + [per_seq] + _seq_weight_specs(sp)
        args = args + [xrg, gy, u, v, h0] + _seq_weights(sp)
        out_specs = out_specs + [pl.BlockSpec((seq_len, 2 * d_rg), blk), per_seq]
        out_shape = out_shape + [jax.ShapeDtypeStruct((ms, 2 * d_rg), BF16),
                                 jax.ShapeDtypeStruct((ms // seq_len, 2, d_rg), F32)]
        scratch = _seq_scratch(seq_len, seq_len, d_rg)
    for w in to_cast:
        rows, cols = w.shape[0] // n_steps, w.shape[1]
        assert rows * n_steps == w.shape[0] and rows % BF16_ROWS == 0
        in_specs = in_specs + [pl.BlockSpec((rows, cols), lambda i: (i, 0))]
        args = args + [w]
    out_specs = out_specs + [pl.BlockSpec((w.shape[0] // n_steps, w.shape[1]), lambda i: (i, 0))
                             for w in to_cast]
    out_shape = out_shape + [jax.ShapeDtypeStruct(w.shape, BF16) for w in to_cast]
    return pl.pallas_call(
        functools.partial(_inproj_kernel, n_cast=len(to_cast), with_seq=seq is not None),
        grid=(n_steps,),
        in_specs=in_specs,
        out_specs=out_specs,
        out_shape=out_shape,
        scratch_shapes=scratch,
        compiler_params=_params(dimension_semantics=("arbitrary",)),
        name="in_projection" if seq is None else "in_projection_seqmix",
    )(*args)


def _ffn_first(x_ref, mix_ref, g1_ref, sc_ref, sh_ref, ng_ref, wo_ref, o_ref, h2_scr, acc_scr):
    mix = jnp.dot(mix_ref[...], wo_ref[...], preferred_element_type=F32)
    x1 = x_ref[...] + g1_ref[...] * mix
    o_ref[...] = x1
    h2 = (_rms(x1) * ng_ref[...]) * (1.0 + sc_ref[...]) + sh_ref[...]
    h2_scr[...] = h2.astype(BF16)
    acc_scr[...] = jnp.zeros_like(acc_scr)


def _ffn_step(w1_ref, w2_ref, h2_scr, acc_scr, between=()):
    n_w1 = w1_ref.shape[1] // FF_PIECE
    w2_parts = w2_ref.shape[1] // FF_PIECE if between else W2_PARTS
    n_pieces = n_w1 + w2_parts
    work = _spread(list(between), n_pieces) if between else [[]] * n_pieces
    part_d = w2_ref.shape[1] // w2_parts

    h2 = h2_scr[...]
    acts = []
    for n in range(n_w1):
        for item in work[n]:
            item()
        a = jnp.dot(h2, w1_ref[:, n * FF_PIECE:(n + 1) * FF_PIECE], preferred_element_type=F32)
        acts.append(jnp.square(jnp.maximum(a, 0.0)).astype(BF16))
    act = jnp.concatenate(acts, axis=1)
    for n in range(w2_parts):
        for item in work[n_w1 + n]:
            item()
        cols = slice(n * part_d, (n + 1) * part_d)
        acc_scr[:, cols] += jnp.dot(act, w2_ref[:, cols], preferred_element_type=F32)


def _ffn_last(g2_ref, fg_ref, o_ref, acc_scr, final_norm):
    x2 = o_ref[...] + g2_ref[...] * acc_scr[...]
    if final_norm:
        x2 = _rms(x2) * fg_ref[...]
    o_ref[...] = x2


def _ffn_kernel(x_ref, mix_ref, g1_ref, sc_ref, sh_ref, g2_ref, ng_ref, fg_ref,
                wo_ref, w1_ref, w2_ref, o_ref, h2_scr, acc_scr, *, final_norm):
    j = pl.program_id(1)

    @pl.when(j == 0)
    def _():
        _ffn_first(x_ref, mix_ref, g1_ref, sc_ref, sh_ref, ng_ref, wo_ref, o_ref, h2_scr, acc_scr)

    _ffn_step(w1_ref, w2_ref, h2_scr, acc_scr)

    @pl.when(j == pl.num_programs(1) - 1)
    def _():
        _ffn_last(g2_ref, fg_ref, o_ref, acc_scr, final_norm)


def _ffn_seq_kernel(x_ref, mix_ref, g1_ref, sc_ref, sh_ref, g2_ref, ng_ref, fg_ref,
                    wo_ref, w1_ref, w2_ref,
                    sx_ref, sprev_ref, snext_ref, gy_ref, u_ref, v_ref, h0_ref,
                    cw_ref, cb_ref, gw_ref, gb_ref, lam_ref, sw_ref, sb_ref,
                    o_ref, smix_ref,
                    h2_scr, acc_scr, hf_scr, carry_scr, xi_scr, hs_scr, *, final_norm):
    i = pl.program_id(0)
    j = pl.program_id(1)
    n_chunks = pl.num_programs(1)
    tc = sx_ref.shape[0]
    backward = i % 2 == 1
    jj = jnp.where(backward, n_chunks - 1 - j, j)
    hf_rows = pl.ds(pl.multiple_of(jj * tc, tc), tc)
    edges = (jj == 0, sprev_ref, jj == n_chunks - 1, snext_ref)

    @pl.when(j == 0)
    def _():
        _ffn_first(x_ref, mix_ref, g1_ref, sc_ref, sh_ref, ng_ref, wo_ref, o_ref, h2_scr, acc_scr)
        carry_scr[...] = h0_ref[pl.ds(i % 2, 1), :]

    def both(reverse):
        items = _seq_items(reverse, edges, hf_rows, sx_ref, gy_ref, u_ref, v_ref,
                           cw_ref, cb_ref, gw_ref, gb_ref, lam_ref, sw_ref, sb_ref,
                           smix_ref, hf_scr, carry_scr, xi_scr, hs_scr, fine=True)
        _ffn_step(w1_ref, w2_ref, h2_scr, acc_scr, between=items)

    @pl.when(jnp.logical_not(backward))
    def _():
        both(False)

    @pl.when(backward)
    def _():
        both(True)

    @pl.when(j == n_chunks - 1)
    def _():
        _ffn_last(g2_ref, fg_ref, o_ref, acc_scr, final_norm)


def _ffn(x, mix, mod, grp, norm_g, final_g, w_out, w_ff1, w_ff2, final_norm, ff_tile, seq, sp):
    m, d = x.shape
    d_mix = mix.shape[1]
    n_steps = w_ff1.shape[1] // ff_tile
    row = lambda i, j: (i, 0)
    vec = pl.BlockSpec((1, d), lambda i, j: (0, 0))
    in_specs = [
        pl.BlockSpec((ROW_TILE, d), row),
        pl.BlockSpec((ROW_TILE, d_mix), row),
        _mod_spec(2, grp, d),
        _mod_spec(4, grp, d),
        _mod_spec(3, grp, d),
        _mod_spec(5, grp, d),
        vec,
        vec,
        pl.BlockSpec((d_mix, d), lambda i, j: (0, 0), pipeline_mode=pl.Buffered(1)),
        pl.BlockSpec((d, ff_tile), lambda i, j: (0, j)),
        pl.BlockSpec((ff_tile, d), lambda i, j: (j, 0)),
    ]
    args = [x, mix, mod, mod, mod, mod, norm_g.reshape(1, d), final_g.reshape(1, d), w_out, w_ff1, w_ff2]
    out_specs = [pl.BlockSpec((ROW_TILE, d), row)]
    out_shape = [jax.ShapeDtypeStruct((m, d), F32)]
    scratch = [
        pltpu.VMEM((ROW_TILE, d), BF16),
        pltpu.VMEM((ROW_TILE, d), F32),
    ]
    if seq is None:
        kern, name = _ffn_kernel, "out_projection_ffn"
    else:
        kern, name = _ffn_seq_kernel, "out_projection_ffn_seqmix"
        xrg, gy, u, v, h0, seq_len = seq
        ms, d_rg = xrg.shape
        n_seq = ms // seq_len
        tc = seq_len // n_steps
        assert 2 * n_seq == m // ROW_TILE and tc % (N_SEG * SUBLANES) == 0 and tc % CHUNK == 0
        per_tile = tc // HALO
        n_halo = ms // HALO

        def chunk(i, j):
            return (i // 2) * n_steps + jnp.where(i % 2 == 1, n_steps - 1 - j, j)

        def out_chunk(i, j):
            return ((i // 2) * n_steps + jnp.where(i % 2 == 1, n_steps - 1 - j, n_steps - 1), 0)

        blk = lambda rows, cols, index: pl.BlockSpec((rows, cols), index)
        in_specs = in_specs + [
            blk(tc, d_rg, lambda i, j: (chunk(i, j), 0)),
            blk(HALO, d_rg, lambda i, j: (jnp.maximum(chunk(i, j) * per_tile - 1, 0), 0)),
            blk(HALO, d_rg, lambda i, j: (jnp.minimum((chunk(i, j) + 1) * per_tile, n_halo - 1), 0)),
            blk(tc, d_rg, out_chunk),
            blk(tc, d_rg, out_chunk),
            blk(tc, d_rg, out_chunk),
            pl.BlockSpec((None, 2, d_rg), lambda i, j: (i // 2, 0, 0)),
        ] + _seq_weight_specs(sp)
        args = args + [xrg, xrg, xrg, gy, u, v, h0] + _seq_weights(sp)
        out_specs = out_specs + [blk(tc, 2 * d_rg, out_chunk)]
        out_shape = out_shape + [jax.ShapeDtypeStruct((ms, 2 * d_rg), BF16)]
        scratch = scratch + _seq_scratch(seq_len, tc, d_rg)
    return pl.pallas_call(
        functools.partial(kern, final_norm=final_norm),
        grid=(m // ROW_TILE, n_steps),
        in_specs=in_specs,
        out_specs=out_specs,
        out_shape=out_shape,
        scratch_shapes=scratch,
        compiler_params=_params(dimension_semantics=("arbitrary", "arbitrary")),
        name=name,
    )(*args)


def kernel(x_prompt, x_sample, c, state_rglru, c_ctx, norm1_g, w_ada, b_ada, w_in, conv_w, conv_b,
           ga_w, ga_b, gi_w, gi_b, lru_lambda, sgu_w, sgu_b, w_out, norm2_g, w_ff1, w_ff2, final_g):
    b_ctx, l_ctx, d = x_prompt.shape
    b_lat, l_lat, _ = x_sample.shape
    depth = w_in.shape[0]
    d_rg = lru_lambda.shape[-1]
    assert b_lat + 1 <= MOD_ROWS
    assert (b_ctx * l_ctx) % ROW_TILE == 0 and l_lat % ROW_TILE == 0

    cond = jnp.concatenate(
        [c, c_ctx[None, :], jnp.zeros((MOD_ROWS - b_lat - 1, d), F32)], axis=0)
    grp_ctx = (b_lat, b_ctx * l_ctx)
    grp_lat = (0, l_lat)
    xp = x_prompt.reshape(b_ctx * l_ctx, d)
    xs = x_sample.reshape(b_lat * l_lat, d)
    h0_ctx = jnp.zeros((b_ctx, 2, d_rg), F32)
    ctx_states = []
    for l in range(depth):
        sp = {
            "conv_w": conv_w[l], "conv_b": conv_b[l].reshape(1, d_rg),
            "gw": jnp.concatenate([ga_w[l], gi_w[l]], axis=-1).astype(BF16),
            "gb": jnp.stack([ga_b[l], gi_b[l]], axis=1),
            "lam": lru_lambda[l],
            "sgu_w": sgu_w[l].astype(BF16), "sgu_b": sgu_b[l][:, :, None],
        }
        win = w_in[l].astype(BF16)
        last = l == depth - 1
        mod = _modulation(cond, w_ada[l], b_ada[l])
        h0_lat = state_rglru[:, l].astype(F32)

        gy_c, xrg_c, u_c, v_c, wff2, wout = _inproj_seq(
            xp, mod, grp_ctx, norm1_g[l], win, None, None, [w_ff2[l], w_out[l]])
        gy_l, xrg_l, u_l, v_l, mix_c, st, wff1 = _inproj_seq(
            xs, mod, grp_lat, norm1_g[l], win, (xrg_c, gy_c, u_c, v_c, h0_ctx, l_ctx), sp, [w_ff1[l]])
        xp, mix_l = _ffn(xp, mix_c, mod, grp_ctx, norm2_g[l], final_g, wout, wff1, wff2, last,
                         FF_TILE_FUSED, (xrg_l, gy_l, u_l, v_l, h0_lat, l_lat), sp)
        xs, = _ffn(xs, mix_l, mod, grp_lat, norm2_g[l], final_g, wout, wff1, wff2, last,
                   FF_TILE_ALONE, None, None)
        ctx_states.append(st)
    y_prompt = xp.reshape(b_ctx, l_ctx, d)
    y_sample = xs.reshape(b_lat, l_lat, d)
    new_state = jnp.stack(ctx_states, axis=1).astype(x_prompt.dtype)
    return (y_prompt, y_sample, new_state)
```

```python
import functools

import jax
import jax.numpy as jnp
from jax import lax
from jax.experimental import pallas as pl
from jax.experimental.pallas import tpu as pltpu

F32 = jnp.float32
BF16 = jnp.bfloat16

N_RG_HEADS = 8
N_CH_HEADS = 8
CHUNK = 128
CONV_W = 4
CONV_LEFT = 2
RG_C = 8.0
N_MOD = 6
EPS = 1e-6

SUBLANES = 8
BF16_ROWS = 16
MOD_ROWS = 8
VMEM_LIMIT_BYTES = 62 * 1024 * 1024

ROW_TILE = 512
IN_ROW_TILE = 256
CAST_ITEM_BYTES = 1 << 19
FF_TILE_FUSED = 512
FF_TILE_ALONE = 1024
RING_SLOTS = 3
FF_PIECE = 256
W2_PARTS = 4
MOD_COL_TILE = 1024
PROJ_PIECE = 256
N_SEG = SUBLANES
HALO = SUBLANES


def _rms(x):
    return x * lax.rsqrt(jnp.mean(x * x, axis=-1, keepdims=True) + EPS)


def _params(**kw):
    return pltpu.CompilerParams(vmem_limit_bytes=VMEM_LIMIT_BYTES, **kw)


def _spread(items, n_slots):
    slots = [[] for _ in range(n_slots)]
    for k, item in enumerate(items):
        slots[k * n_slots // len(items)].append(item)
    return slots


def _merge(a, b):
    keyed = [((k + 0.5) / len(a), 0, k, item) for k, item in enumerate(a)]
    keyed += [((k + 0.5) / len(b), 1, k, item) for k, item in enumerate(b)]
    return [entry[-1] for entry in sorted(keyed, key=lambda entry: entry[:3])]


def _mod_kernel(c_ref, w_ref, b_ref, o_ref):
    s = jax.nn.silu(c_ref[...]).astype(BF16)
    o_ref[...] = jnp.dot(s, w_ref[...].astype(BF16), preferred_element_type=F32) + b_ref[...]


def _modulation(cond, w_ada, b_ada):
    d = cond.shape[1]
    n = w_ada.shape[1]
    out = pl.pallas_call(
        _mod_kernel,
        grid=(n // MOD_COL_TILE,),
        in_specs=[
            pl.BlockSpec((MOD_ROWS, d), lambda j: (0, 0)),
            pl.BlockSpec((d, MOD_COL_TILE), lambda j: (0, j)),
            pl.BlockSpec((1, MOD_COL_TILE), lambda j: (0, j)),
        ],
        out_specs=pl.BlockSpec((MOD_ROWS, MOD_COL_TILE), lambda j: (0, j)),
        out_shape=jax.ShapeDtypeStruct((MOD_ROWS, n), F32),
        compiler_params=_params(dimension_semantics=("arbitrary",)),
        name="modulation",
    )(cond, w_ada, b_ada.reshape(1, n))
    return out.reshape(MOD_ROWS, N_MOD, 1, d)


def _mod_spec(which, grp, d, row_tile=ROW_TILE):
    mod_base, rows_per_mod = grp

    def index(i, *_):
        return (mod_base + (i * row_tile) // rows_per_mod, which, 0, 0)
    return pl.BlockSpec((None, None, 1, d), index)


def _cast_items(pairs):
    items = []
    for src, dst in pairs:
        rows = max(CAST_ITEM_BYTES // (4 * src.shape[1]), BF16_ROWS)
        for r in range(0, src.shape[0], rows):
            def run(src=src, dst=dst, part=slice(r, r + rows)):
                dst[part, :] = src[part, :].astype(BF16)
            items.append(run)
    return items


def _inproj_body(x_ref, g_ref, sc_ref, sh_ref, w_ref, gy_ref, xrg_ref, u_ref, v_ref, between=()):
    d_rg = gy_ref.shape[1]
    h = (_rms(x_ref[...]) * g_ref[...]) * (1.0 + sc_ref[...]) + sh_ref[...]
    hb = h.astype(BF16)
    outs = (gy_ref, xrg_ref, u_ref, v_ref)
    n_pieces = w_ref.shape[1] // PROJ_PIECE
    work = _spread(list(between), n_pieces) if between else [[]] * n_pieces
    for n in range(n_pieces):
        part = jnp.dot(hb, w_ref[:, n * PROJ_PIECE:(n + 1) * PROJ_PIECE], preferred_element_type=F32)
        dest, col = divmod(n * PROJ_PIECE, d_rg)
        if dest != 1:
            part = jax.nn.gelu(part)
        outs[dest][:, col:col + PROJ_PIECE] = part.astype(outs[dest].dtype)
        for item in work[n]:
            item()


def _inproj_specs(m, d, d_in, grp):
    d_rg = d_in // 4
    row = lambda i: (i, 0)
    in_specs = [
        pl.BlockSpec((IN_ROW_TILE, d), row),
        pl.BlockSpec((1, d), lambda i: (0, 0)),
        _mod_spec(1, grp, d, IN_ROW_TILE),
        _mod_spec(0, grp, d, IN_ROW_TILE),
        pl.BlockSpec((d, d_in), lambda i: (0, 0), pipeline_mode=pl.Buffered(1)),
    ]
    out_specs = [pl.BlockSpec((IN_ROW_TILE, d_rg), row)] * 4
    out_shape = [
        jax.ShapeDtypeStruct((m, d_rg), F32),
        jax.ShapeDtypeStruct((m, d_rg), F32),
        jax.ShapeDtypeStruct((m, d_rg), F32),
        jax.ShapeDtypeStruct((m, d_rg), BF16),
    ]
    return in_specs, out_specs, out_shape


def _scan_head(hd, reverse, edges, hf_rows, x_ref, gy_ref, cw_ref, cb_ref, gw_ref, gb_ref, lam_ref,
               mix_ref, hf_scr, carry_scr, xi_scr, hs_scr):
    tc, d_rg = x_ref.shape
    tl = tc // N_SEG
    head = d_rg // N_RG_HEADS
    lead = CONV_LEFT * N_SEG
    dirn = 1 if reverse else 0
    steps = range(tl - 1, -1, -1) if reverse else range(tl)
    segs = range(N_SEG - 1, -1, -1) if reverse else range(N_SEG)
    cols = slice(hd * head, (hd + 1) * head)
    xi = xi_scr.at[hd]
    held = {}

    def step_rows(t):
        return xi[lead + t * N_SEG:lead + (t + 1) * N_SEG, :]

    def conv_and_gate_matmul():
        sub = lax.broadcasted_iota(jnp.int32, (N_SEG, head), 0)
        for sg in range(N_SEG):
            xi[pl.ds(lead + sg, tl, stride=N_SEG), :] = x_ref[sg * tl:(sg + 1) * tl, cols]
        for k in range(CONV_LEFT):
            if edges is None:
                edge = 0.0
            else:
                row = HALO - CONV_LEFT + k
                edge = jnp.where(edges[0], 0.0, edges[1][row:row + 1, cols])
            prev_step = pltpu.roll(step_rows(tl - CONV_LEFT + k), 1, axis=0)
            xi[k * N_SEG:(k + 1) * N_SEG, :] = jnp.where(sub == 0, edge, prev_step)
        for k in range(CONV_W - 1 - CONV_LEFT):
            edge = 0.0 if edges is None else jnp.where(edges[2], 0.0, edges[3][k:k + 1, cols])
            next_step = pltpu.roll(step_rows(k), N_SEG - 1, axis=0)
            xi[lead + (tl + k) * N_SEG:lead + (tl + k + 1) * N_SEG, :] = jnp.where(
                sub == N_SEG - 1, edge, next_step)
        xc = cb_ref[:, cols]
        for k in range(CONV_W):
            xc = xc + cw_ref[k:k + 1, cols] * xi[k * N_SEG:k * N_SEG + tc, :]
        held["xc"] = xc
        held["g"] = jnp.dot(xc.astype(BF16), gw_ref[dirn, hd], preferred_element_type=F32)

    def gates_and_scan():
        sub = lax.broadcasted_iota(jnp.int32, (N_SEG, head), 0)
        xc, g = held["xc"], held["g"]
        r = jax.nn.sigmoid(g[:, :head] + gb_ref[dirn, 0:1, cols])
        i = jax.nn.sigmoid(g[:, head:] + gb_ref[dirn, 1:2, cols])
        log_a = r * (-RG_C * jax.nn.softplus(-lam_ref[dirn:dirn + 1, cols]))
        a = jnp.exp(log_a)
        m2 = jnp.maximum(-jnp.tanh(log_a) * (a * a + 1.0), 0.0)
        mult = jnp.where(m2 > 0.0, m2 * lax.rsqrt(m2), 0.0)
        bx = mult * (i * xc)

        a3 = a.reshape(tl, N_SEG, head)
        b3 = bx.reshape(tl, N_SEG, head)
        hloc = [None] * tl
        prod = [None] * tl
        hprev = pprev = None
        for t in steps:
            if hprev is None:
                hprev, pprev = b3[t], a3[t]
            else:
                hprev, pprev = a3[t] * hprev + b3[t], a3[t] * pprev
            hloc[t], prod[t] = hprev, pprev
        state = carry_scr[:, cols]
        init = jnp.zeros((N_SEG, head), F32)
        for sg in segs:
            init = jnp.where(sub == sg, state, init)
            state = hprev[sg:sg + 1, :] + pprev[sg:sg + 1, :] * state
        carry_scr[:, cols] = state
        hfull = jnp.concatenate([hloc[t] + prod[t] * init for t in range(tl)], axis=0)
        if not reverse:
            hf_scr[hf_rows, cols] = hfull
        else:
            hs = hs_scr.at[hd]
            hs[...] = hf_scr[hf_rows, cols] + hfull
            for sg in range(N_SEG):
                rows = slice(sg * tl, (sg + 1) * tl)
                hsum = hs[pl.ds(sg, tl, stride=N_SEG), :]
                mix_ref[rows, cols] = (gy_ref[rows, cols] * hsum).astype(BF16)

    return [conv_and_gate_matmul, gates_and_scan]


def _sgu_head(hd, u_ref, v_ref, sw_ref, sb_ref, mix_ref):
    tc, d_ch = u_ref.shape
    ch_head = d_ch // N_CH_HEADS
    d_rg = mix_ref.shape[1] - d_ch
    cols = slice(hd * ch_head, (hd + 1) * ch_head)
    for n in range(tc // CHUNK):
        rows = slice(n * CHUNK, (n + 1) * CHUNK)
        mixed = jnp.dot(sw_ref[hd], v_ref[rows, cols], preferred_element_type=F32) + sb_ref[hd]
        mix_ref[rows, d_rg + hd * ch_head:d_rg + (hd + 1) * ch_head] = (
            u_ref[rows, cols] * mixed).astype(BF16)


def _seq_items(reverse, edges, hf_rows, x_ref, gy_ref, u_ref, v_ref,
               cw_ref, cb_ref, gw_ref, gb_ref, lam_ref, sw_ref, sb_ref,
               mix_ref, hf_scr, carry_scr, xi_scr, hs_scr, fine):
    def together(parts):
        def run():
            for part in parts:
                part()
        return run

    items = []
    for hd in range(N_RG_HEADS):
        parts = _scan_head(hd, reverse, edges, hf_rows, x_ref, gy_ref, cw_ref, cb_ref, gw_ref, gb_ref,
                           lam_ref, mix_ref, hf_scr, carry_scr, xi_scr, hs_scr)
        if reverse:
            parts.append(functools.partial(_sgu_head, hd, u_ref, v_ref, sw_ref, sb_ref, mix_ref))
        items += parts if fine else [together(parts)]
    return items


def _seq_weight_specs(sp):
    full = lambda a: pl.BlockSpec(a.shape, lambda *_: (0,) * a.ndim, pipeline_mode=pl.Buffered(1))
    return [full(sp[k]) for k in ("conv_w", "conv_b", "gw", "gb", "lam", "sgu_w", "sgu_b")]


def _seq_weights(sp):
    return [sp[k] for k in ("conv_w", "conv_b", "gw", "gb", "lam", "sgu_w", "sgu_b")]


def _seq_scratch(seq_len, tc, d_rg):
    head = d_rg // N_RG_HEADS
    return [
        pltpu.VMEM((seq_len, d_rg), F32),
        pltpu.VMEM((1, d_rg), F32),
        pltpu.VMEM((N_RG_HEADS, tc + (CONV_W - 1) * N_SEG, head), F32),
        pltpu.VMEM((N_RG_HEADS, tc, head), F32),
    ]


N_INPROJ_IN = 5
N_INPROJ_OUT = 4
N_SEQ_IN = 12
N_SEQ_OUT = 2
N_SEQ_SCRATCH = 4


def _inproj_kernel(*refs, n_cast, with_seq):
    s = pl.program_id(0)
    refs = list(refs)
    take = lambda k: [refs.pop(0) for _ in range(k)]
    proj_in = take(N_INPROJ_IN)
    seq_in = take(N_SEQ_IN) if with_seq else None
    cast_src = take(n_cast)
    proj_out = take(N_INPROJ_OUT)
    seq_out = take(N_SEQ_OUT) if with_seq else None
    cast_dst = take(n_cast)
    casts = _cast_items(list(zip(cast_src, cast_dst)))
    if not with_seq:
        _inproj_body(*proj_in, *proj_out, between=casts)
        return

    x_ref, gy_ref, u_ref, v_ref, h0_ref = seq_in[:5]
    weights = seq_in[5:]
    mix_ref, st_ref = seq_out
    hf_scr, carry_scr, xi_scr, hs_scr = take(N_SEQ_SCRATCH)
    tc = x_ref.shape[0]

    def direction(reverse):
        dirn = 1 if reverse else 0
        work = _seq_items(reverse, None, slice(0, tc), x_ref, gy_ref, u_ref, v_ref, *weights,
                          mix_ref, hf_scr, carry_scr, xi_scr, hs_scr, fine=False)

        def first():
            carry_scr[...] = h0_ref[dirn:dirn + 1, :]
            work[0]()

        def last():
            work[-1]()
            st_ref[dirn:dirn + 1, :] = carry_scr[...]

        items = [first] + work[1:-1] + [last]
        _inproj_body(*proj_in, *proj_out, between=_merge(items, casts) if casts else items)

    @pl.when(s % 2 == 0)
    def _():
        direction(False)

    @pl.when(s % 2 == 1)
    def _():
        direction(True)


def _inproj_seq(x, mod, grp, norm_g, w_in, seq, sp, to_cast):
    m, d = x.shape
    n_steps = m // IN_ROW_TILE
    in_specs, out_specs, out_shape = _inproj_specs(m, d, w_in.shape[1], grp)
    args = [x, norm_g.reshape(1, d), mod, mod, w_in]
    scratch = []
    if seq is not None:
        xrg, gy, u, v, h0, seq_len = seq
        ms, d_rg = xrg.shape
        assert 2 * (ms // seq_len) == n_steps
        blk = lambda i: (i // 2, 0)
        per_seq = pl.BlockSpec((None, 2, d_rg), lambda i: (i // 2, 0, 0))
        in_specs = in_specs + [pl.BlockSpec((seq_len, d_rg), blk)] * 4 + [per_seq] + _seq_weight_specs(sp)
        args = args + [xrg, gy, u, v, h0] + _seq_weights(sp)
        out_specs = out_specs + [pl.BlockSpec((seq_len, 2 * d_rg), blk), per_seq]
        out_shape = out_shape + [jax.ShapeDtypeStruct((ms, 2 * d_rg), BF16),
                                 jax.ShapeDtypeStruct((ms // seq_len, 2, d_rg), F32)]
        scratch = _seq_scratch(seq_len, seq_len, d_rg)
    for w in to_cast:
        rows, cols = w.shape[0] // n_steps, w.shape[1]
        assert rows * n_steps == w.shape[0] and rows % BF16_ROWS == 0
        in_specs = in_specs + [pl.BlockSpec((rows, cols), lambda i: (i, 0))]
        args = args + [w]
    out_specs = out_specs + [pl.BlockSpec((w.shape[0] // n_steps, w.shape[1]), lambda i: (i, 0))
                             for w in to_cast]
    out_shape = out_shape + [jax.ShapeDtypeStruct(w.shape, BF16) for w in to_cast]
    return pl.pallas_call(
        functools.partial(_inproj_kernel, n_cast=len(to_cast), with_seq=seq is not None),
        grid=(n_steps,),
        in_specs=in_specs,
        out_specs=out_specs,
        out_shape=out_shape,
        scratch_shapes=scratch,
        compiler_params=_params(dimension_semantics=("arbitrary",)),
        name="in_projection" if seq is None else "in_projection_seqmix",
    )(*args)


def _ffn_first(x_ref, mix_ref, g1_ref, sc_ref, sh_ref, ng_ref, wo_ref, o_ref, h2_scr, acc_scr):
    mix = jnp.dot(mix_ref[...], wo_ref[...], preferred_element_type=F32)
    x1 = x_ref[...] + g1_ref[...] * mix
    o_ref[...] = x1
    h2 = (_rms(x1) * ng_ref[...]) * (1.0 + sc_ref[...]) + sh_ref[...]
    h2_scr[...] = h2.astype(BF16)
    acc_scr[...] = jnp.zeros_like(acc_scr)


def _ffn_step(w1_ref, w2_ref, h2_scr, acc_scr, between=()):
    n_w1 = w1_ref.shape[1] // FF_PIECE
    w2_parts = w2_ref.shape[1] // FF_PIECE if between else W2_PARTS
    n_pieces = n_w1 + w2_parts
    work = _spread(list(between), n_pieces) if between else [[]] * n_pieces
    part_d = w2_ref.shape[1] // w2_parts

    h2 = h2_scr[...]
    acts = []
    for n in range(n_w1):
        for item in work[n]:
            item()
        a = jnp.dot(h2, w1_ref[:, n * FF_PIECE:(n + 1) * FF_PIECE], preferred_element_type=F32)
        acts.append(jnp.square(jnp.maximum(a, 0.0)).astype(BF16))
    act = jnp.concatenate(acts, axis=1)
    for n in range(w2_parts):
        for item in work[n_w1 + n]:
            item()
        cols = slice(n * part_d, (n + 1) * part_d)
        acc_scr[:, cols] += jnp.dot(act, w2_ref[:, cols], preferred_element_type=F32)


def _ffn_last(g2_ref, fg_ref, o_ref, acc_scr, final_norm):
    x2 = o_ref[...] + g2_ref[...] * acc_scr[...]
    if final_norm:
        x2 = _rms(x2) * fg_ref[...]
    o_ref[...] = x2


def _ffn_kernel(x_ref, mix_ref, g1_ref, sc_ref, sh_ref, g2_ref, ng_ref, fg_ref,
                wo_ref, w1_ref, w2_ref, o_ref, h2_scr, acc_scr, *, final_norm):
    j = pl.program_id(1)

    @pl.when(j == 0)
    def _():
        _ffn_first(x_ref, mix_ref, g1_ref, sc_ref, sh_ref, ng_ref, wo_ref, o_ref, h2_scr, acc_scr)

    _ffn_step(w1_ref, w2_ref, h2_scr, acc_scr)

    @pl.when(j == pl.num_programs(1) - 1)
    def _():
        _ffn_last(g2_ref, fg_ref, o_ref, acc_scr, final_norm)


def _ffn_ring_kernel(x_ref, mix_ref, g1_ref, sc_ref, sh_ref, g2_ref, ng_ref, fg_ref,
                     wo_ref, w1_hbm, w2_hbm, o_ref, h2_scr, acc_scr, w1_ring, w2_ring, sems,
                     *, final_norm):
    i = pl.program_id(0)
    j = pl.program_id(1)
    steps = pl.num_programs(1)
    total = pl.num_programs(0) * steps
    t = i * steps + j
    ff_tile = w1_ring.shape[2]

    def copy1(tt):
        slot = tt % w1_ring.shape[0]
        col0 = pl.multiple_of((tt % steps) * ff_tile, ff_tile)
        return pltpu.make_async_copy(w1_hbm.at[:, pl.ds(col0, ff_tile)], w1_ring.at[slot], sems.at[0, slot])

    def copy2(tt):
        slot = tt % w2_ring.shape[0]
        col0 = pl.multiple_of((tt % steps) * ff_tile, ff_tile)
        return pltpu.make_async_copy(w2_hbm.at[pl.ds(col0, ff_tile), :], w2_ring.at[slot], sems.at[1, slot])

    for copy, ring in ((copy1, w1_ring), (copy2, w2_ring)):
        ahead = ring.shape[0] - 1

        @pl.when(t == 0)
        def _():
            for tt in range(ahead):
                copy(tt).start()

        @pl.when(t + ahead < total)
        def _():
            copy(t + ahead).start()

    copy1(t).wait()
    copy2(t).wait()

    @pl.when(j == 0)
    def _():
        _ffn_first(x_ref, mix_ref, g1_ref, sc_ref, sh_ref, ng_ref, wo_ref, o_ref, h2_scr, acc_scr)

    _ffn_step(w1_ring.at[t % w1_ring.shape[0]], w2_ring.at[t % w2_ring.shape[0]], h2_scr, acc_scr)

    @pl.when(j == steps - 1)
    def _():
        _ffn_last(g2_ref, fg_ref, o_ref, acc_scr, final_norm)


def _ffn_seq_kernel(x_ref, mix_ref, g1_ref, sc_ref, sh_ref, g2_ref, ng_ref, fg_ref,
                    wo_ref, w1_ref, w2_ref,
                    sx_ref, sprev_ref, snext_ref, gy_ref, u_ref, v_ref, h0_ref,
                    cw_ref, cb_ref, gw_ref, gb_ref, lam_ref, sw_ref, sb_ref,
                    o_ref, smix_ref,
                    h2_scr, acc_scr, hf_scr, carry_scr, xi_scr, hs_scr, *, final_norm):
    i = pl.program_id(0)
    j = pl.program_id(1)
    n_chunks = pl.num_programs(1)
    tc = sx_ref.shape[0]
    backward = i % 2 == 1
    jj = jnp.where(backward, n_chunks - 1 - j, j)
    hf_rows = pl.ds(pl.multiple_of(jj * tc, tc), tc)
    edges = (jj == 0, sprev_ref, jj == n_chunks - 1, snext_ref)

    @pl.when(j == 0)
    def _():
        _ffn_first(x_ref, mix_ref, g1_ref, sc_ref, sh_ref, ng_ref, wo_ref, o_ref, h2_scr, acc_scr)
        carry_scr[...] = h0_ref[pl.ds(i % 2, 1), :]

    def both(reverse):
        items = _seq_items(reverse, edges, hf_rows, sx_ref, gy_ref, u_ref, v_ref,
                           cw_ref, cb_ref, gw_ref, gb_ref, lam_ref, sw_ref, sb_ref,
                           smix_ref, hf_scr, carry_scr, xi_scr, hs_scr, fine=True)
        _ffn_step(w1_ref, w2_ref, h2_scr, acc_scr, between=items)

    @pl.when(jnp.logical_not(backward))
    def _():
        both(False)

    @pl.when(backward)
    def _():
        both(True)

    @pl.when(j == n_chunks - 1)
    def _():
        _ffn_last(g2_ref, fg_ref, o_ref, acc_scr, final_norm)


def _ffn(x, mix, mod, grp, norm_g, final_g, w_out, w_ff1, w_ff2, final_norm, ff_tile, seq, sp):
    m, d = x.shape
    d_mix = mix.shape[1]
    n_steps = w_ff1.shape[1] // ff_tile
    row = lambda i, j: (i, 0)
    vec = pl.BlockSpec((1, d), lambda i, j: (0, 0))
    in_specs = [
        pl.BlockSpec((ROW_TILE, d), row),
        pl.BlockSpec((ROW_TILE, d_mix), row),
        _mod_spec(2, grp, d),
        _mod_spec(4, grp, d),
        _mod_spec(3, grp, d),
        _mod_spec(5, grp, d),
        vec,
        vec,
        pl.BlockSpec((d_mix, d), lambda i, j: (0, 0), pipeline_mode=pl.Buffered(1)),
        pl.BlockSpec((d, ff_tile), lambda i, j: (0, j)),
        pl.BlockSpec((ff_tile, d), lambda i, j: (j, 0)),
    ]
    args = [x, mix, mod, mod, mod, mod, norm_g.reshape(1, d), final_g.reshape(1, d), w_out, w_ff1, w_ff2]
    out_specs = [pl.BlockSpec((ROW_TILE, d), row)]
    out_shape = [jax.ShapeDtypeStruct((m, d), F32)]
    scratch = [
        pltpu.VMEM((ROW_TILE, d), BF16),
        pltpu.VMEM((ROW_TILE, d), F32),
    ]
    if seq is None:
        kern, name = _ffn_ring_kernel, "out_projection_ffn"
        assert (m // ROW_TILE) * n_steps >= RING_SLOTS
        in_specs[-2:] = [pl.BlockSpec(memory_space=pl.ANY)] * 2
        scratch = scratch + [
            pltpu.VMEM((RING_SLOTS, d, ff_tile), BF16),
            pltpu.VMEM((2, ff_tile, d), BF16),
            pltpu.SemaphoreType.DMA((2, RING_SLOTS)),
        ]
    else:
        kern, name = _ffn_seq_kernel, "out_projection_ffn_seqmix"
        xrg, gy, u, v, h0, seq_len = seq
        ms, d_rg = xrg.shape
        n_seq = ms // seq_len
        tc = seq_len // n_steps
        assert 2 * n_seq == m // ROW_TILE and tc % (N_SEG * SUBLANES) == 0 and tc % CHUNK == 0
        per_tile = tc // HALO
        n_halo = ms // HALO

        def chunk(i, j):
            return (i // 2) * n_steps + jnp.where(i % 2 == 1, n_steps - 1 - j, j)

        def out_chunk(i, j):
            return ((i // 2) * n_steps + jnp.where(i % 2 == 1, n_steps - 1 - j, n_steps - 1), 0)

        blk = lambda rows, cols, index: pl.BlockSpec((rows, cols), index)
        in_specs = in_specs + [
            blk(tc, d_rg, lambda i, j: (chunk(i, j), 0)),
            blk(HALO, d_rg, lambda i, j: (jnp.maximum(chunk(i, j) * per_tile - 1, 0), 0)),
            blk(HALO, d_rg, lambda i, j: (jnp.minimum((chunk(i, j) + 1) * per_tile, n_halo - 1), 0)),
            blk(tc, d_rg, out_chunk),
            blk(tc, d_rg, out_chunk),
            blk(tc, d_rg, out_chunk),
            pl.BlockSpec((None, 2, d_rg), lambda i, j: (i // 2, 0, 0)),
        ] + _seq_weight_specs(sp)
        args = args + [xrg, xrg, xrg, gy, u, v, h0] + _seq_weights(sp)
        out_specs = out_specs + [blk(tc, 2 * d_rg, out_chunk)]
        out_shape = out_shape + [jax.ShapeDtypeStruct((ms, 2 * d_rg), BF16)]
        scratch = scratch + _seq_scratch(seq_len, tc, d_rg)
    return pl.pallas_call(
        functools.partial(kern, final_norm=final_norm),
        grid=(m // ROW_TILE, n_steps),
        in_specs=in_specs,
        out_specs=out_specs,
        out_shape=out_shape,
        scratch_shapes=scratch,
        compiler_params=_params(dimension_semantics=("arbitrary", "arbitrary")),
        name=name,
    )(*args)


def kernel(x_prompt, x_sample, c, state_rglru, c_ctx, norm1_g, w_ada, b_ada, w_in, conv_w, conv_b,
           ga_w, ga_b, gi_w, gi_b, lru_lambda, sgu_w, sgu_b, w_out, norm2_g, w_ff1, w_ff2, final_g):
    b_ctx, l_ctx, d = x_prompt.shape
    b_lat, l_lat, _ = x_sample.shape
    depth = w_in.shape[0]
    d_rg = lru_lambda.shape[-1]
    assert b_lat + 1 <= MOD_ROWS
    assert (b_ctx * l_ctx) % ROW_TILE == 0 and l_lat % ROW_TILE == 0

    cond = jnp.concatenate(
        [c, c_ctx[None, :], jnp.zeros((MOD_ROWS - b_lat - 1, d), F32)], axis=0)
    grp_ctx = (b_lat, b_ctx * l_ctx)
    grp_lat = (0, l_lat)
    xp = x_prompt.reshape(b_ctx * l_ctx, d)
    xs = x_sample.reshape(b_lat * l_lat, d)
    h0_ctx = jnp.zeros((b_ctx, 2, d_rg), F32)
    ctx_states = []
    for l in range(depth):
        sp = {
            "conv_w": conv_w[l], "conv_b": conv_b[l].reshape(1, d_rg),
            "gw": jnp.concatenate([ga_w[l], gi_w[l]], axis=-1).astype(BF16),
            "gb": jnp.stack([ga_b[l], gi_b[l]], axis=1),
            "lam": lru_lambda[l],
            "sgu_w": sgu_w[l].astype(BF16), "sgu_b": sgu_b[l][:, :, None],
        }
        win = w_in[l].astype(BF16)
        last = l == depth - 1
        mod = _modulation(cond, w_ada[l], b_ada[l])
        h0_lat = state_rglru[:, l].astype(F32)

        gy_c, xrg_c, u_c, v_c, wff2, wout = _inproj_seq(
            xp, mod, grp_ctx, norm1_g[l], win, None, None, [w_ff2[l], w_out[l]])
        gy_l, xrg_l, u_l, v_l, mix_c, st, wff1 = _inproj_seq(
            xs, mod, grp_lat, norm1_g[l], win, (xrg_c, gy_c, u_c, v_c, h0_ctx, l_ctx), sp, [w_ff1[l]])
        xp, mix_l = _ffn(xp, mix_c, mod, grp_ctx, norm2_g[l], final_g, wout, wff1, wff2, last,
                         FF_TILE_FUSED, (xrg_l, gy_l, u_l, v_l, h0_lat, l_lat), sp)
        xs, = _ffn(xs, mix_l, mod, grp_lat, norm2_g[l], final_g, wout, wff1, wff2, last,
                   FF_TILE_ALONE, None, None)
        ctx_states.append(st)
    y_prompt = xp.reshape(b_ctx, l_ctx, d)
    y_sample = xs.reshape(b_lat, l_lat, d)
    new_state = jnp.stack(ctx_states, axis=1).astype(x_prompt.dtype)
    return (y_prompt, y_sample, new_state)
```

```python
import functools

import jax
import jax.numpy as jnp
from jax import lax
from jax.experimental import pallas as pl
from jax.experimental.pallas import tpu as pltpu

F32 = jnp.float32
BF16 = jnp.bfloat16

N_RG_HEADS = 8
N_CH_HEADS = 8
CHUNK = 128
CONV_W = 4
CONV_LEFT = 2
RG_C = 8.0
N_MOD = 6
EPS = 1e-6

SUBLANES = 8
BF16_ROWS = 16
MOD_ROWS = 8
VMEM_LIMIT_BYTES = 62 * 1024 * 1024

ROW_TILE = 512
IN_ROW_TILE = 256
CAST_ITEM_BYTES = 1 << 19
FF_TILE_FUSED = 512
FF_TILE_ALONE = 1024
FF_PIECE = 256
W2_PARTS = 4
MOD_COL_TILE = 1024
PROJ_PIECE = 256
N_SEG = SUBLANES
HALO = SUBLANES


def _rms(x):
    return x * lax.rsqrt(jnp.mean(x * x, axis=-1, keepdims=True) + EPS)


def _params(**kw):
    return pltpu.CompilerParams(vmem_limit_bytes=VMEM_LIMIT_BYTES, **kw)


def _spread(items, n_slots):
    slots = [[] for _ in range(n_slots)]
    for k, item in enumerate(items):
        slots[k * n_slots // len(items)].append(item)
    return slots


def _merge(a, b):
    keyed = [((k + 0.5) / len(a), 0, k, item) for k, item in enumerate(a)]
    keyed += [((k + 0.5) / len(b), 1, k, item) for k, item in enumerate(b)]
    return [entry[-1] for entry in sorted(keyed, key=lambda entry: entry[:3])]


def _mod_kernel(c_ref, cc_ref, w_ref, b_ref, o_ref, cond_scr):
    n_lat = c_ref.shape[0]
    cond_scr[...] = jnp.zeros_like(cond_scr)
    cond_scr[0:n_lat, :] = c_ref[...]
    cond_scr[n_lat:n_lat + 1, :] = cc_ref[...]
    s = jax.nn.silu(cond_scr[...]).astype(BF16)
    val = jnp.dot(s, w_ref[...].astype(BF16), preferred_element_type=F32) + b_ref[...]
    for r in range(MOD_ROWS):
        o_ref[r] = val[r:r + 1, :]


def _modulation(c, c_ctx, w_ada, b_ada):
    n_lat, d = c.shape
    n = w_ada.shape[1]
    per = d // MOD_COL_TILE
    assert d % MOD_COL_TILE == 0 and n == N_MOD * d and n_lat + 1 <= MOD_ROWS
    return pl.pallas_call(
        _mod_kernel,
        grid=(n // MOD_COL_TILE,),
        in_specs=[
            pl.BlockSpec((n_lat, d), lambda j: (0, 0)),
            pl.BlockSpec((1, d), lambda j: (0, 0)),
            pl.BlockSpec((d, MOD_COL_TILE), lambda j: (0, j)),
            pl.BlockSpec((1, MOD_COL_TILE), lambda j: (0, j)),
        ],
        out_specs=pl.BlockSpec((MOD_ROWS, None, 1, MOD_COL_TILE), lambda j: (0, j // per, 0, j % per)),
        out_shape=jax.ShapeDtypeStruct((MOD_ROWS, N_MOD, 1, d), F32),
        scratch_shapes=[pltpu.VMEM((MOD_ROWS, d), F32)],
        compiler_params=_params(dimension_semantics=("arbitrary",)),
        name="modulation",
    )(c, c_ctx, w_ada, b_ada.reshape(1, n))


def _mod_spec(which, grp, d, row_tile=ROW_TILE):
    mod_base, rows_per_mod = grp

    def index(i, *_):
        return (mod_base + (i * row_tile) // rows_per_mod, which, 0, 0)
    return pl.BlockSpec((None, None, 1, d), index)


def _cast_items(pairs):
    items = []
    for src, dst in pairs:
        rows = max(CAST_ITEM_BYTES // (4 * src.shape[1]), BF16_ROWS)
        for r in range(0, src.shape[0], rows):
            def run(src=src, dst=dst, part=slice(r, r + rows)):
                dst[part, :] = src[part, :].astype(BF16)
            items.append(run)
    return items


def _inproj_body(x_ref, g_ref, sc_ref, sh_ref, w_ref, gy_ref, xrg_ref, u_ref, v_ref, between=()):
    d_rg = gy_ref.shape[1]
    h = (_rms(x_ref[...]) * g_ref[...]) * (1.0 + sc_ref[...]) + sh_ref[...]
    hb = h.astype(BF16)
    outs = (gy_ref, xrg_ref, u_ref, v_ref)
    n_pieces = w_ref.shape[1] // PROJ_PIECE
    work = _spread(list(between), n_pieces) if between else [[]] * n_pieces
    for n in range(n_pieces):
        part = jnp.dot(hb, w_ref[:, n * PROJ_PIECE:(n + 1) * PROJ_PIECE], preferred_element_type=F32)
        dest, col = divmod(n * PROJ_PIECE, d_rg)
        if dest != 1:
            part = jax.nn.gelu(part)
        outs[dest][:, col:col + PROJ_PIECE] = part.astype(outs[dest].dtype)
        for item in work[n]:
            item()


def _inproj_specs(m, d, d_in, grp):
    d_rg = d_in // 4
    row = lambda i: (i, 0)
    in_specs = [
        pl.BlockSpec((IN_ROW_TILE, d), row),
        pl.BlockSpec((1, d), lambda i: (0, 0)),
        _mod_spec(1, grp, d, IN_ROW_TILE),
        _mod_spec(0, grp, d, IN_ROW_TILE),
        pl.BlockSpec((d, d_in), lambda i: (0, 0), pipeline_mode=pl.Buffered(1)),
    ]
    out_specs = [pl.BlockSpec((IN_ROW_TILE, d_rg), row)] * 4
    out_shape = [
        jax.ShapeDtypeStruct((m, d_rg), F32),
        jax.ShapeDtypeStruct((m, d_rg), F32),
        jax.ShapeDtypeStruct((m, d_rg), F32),
        jax.ShapeDtypeStruct((m, d_rg), BF16),
    ]
    return in_specs, out_specs, out_shape


def _scan_head(hd, reverse, edges, hf_rows, x_ref, gy_ref, cw_ref, cb_ref, gw_ref, gb_ref, lam_ref,
               mix_ref, hf_scr, carry_scr, xi_scr, hs_scr):
    tc, d_rg = x_ref.shape
    tl = tc // N_SEG
    head = d_rg // N_RG_HEADS
    lead = CONV_LEFT * N_SEG
    dirn = 1 if reverse else 0
    steps = range(tl - 1, -1, -1) if reverse else range(tl)
    segs = range(N_SEG - 1, -1, -1) if reverse else range(N_SEG)
    cols = slice(hd * head, (hd + 1) * head)
    xi = xi_scr.at[hd]
    held = {}

    def step_rows(t):
        return xi[lead + t * N_SEG:lead + (t + 1) * N_SEG, :]

    def conv_and_gate_matmul():
        sub = lax.broadcasted_iota(jnp.int32, (N_SEG, head), 0)
        for sg in range(N_SEG):
            xi[pl.ds(lead + sg, tl, stride=N_SEG), :] = x_ref[sg * tl:(sg + 1) * tl, cols]
        for k in range(CONV_LEFT):
            if edges is None:
                edge = 0.0
            else:
                row = HALO - CONV_LEFT + k
                edge = jnp.where(edges[0], 0.0, edges[1][row:row + 1, cols])
            prev_step = pltpu.roll(step_rows(tl - CONV_LEFT + k), 1, axis=0)
            xi[k * N_SEG:(k + 1) * N_SEG, :] = jnp.where(sub == 0, edge, prev_step)
        for k in range(CONV_W - 1 - CONV_LEFT):
            edge = 0.0 if edges is None else jnp.where(edges[2], 0.0, edges[3][k:k + 1, cols])
            next_step = pltpu.roll(step_rows(k), N_SEG - 1, axis=0)
            xi[lead + (tl + k) * N_SEG:lead + (tl + k + 1) * N_SEG, :] = jnp.where(
                sub == N_SEG - 1, edge, next_step)
        xc = cb_ref[:, cols]
        for k in range(CONV_W):
            xc = xc + cw_ref[k:k + 1, cols] * xi[k * N_SEG:k * N_SEG + tc, :]
        held["xc"] = xc
        held["g"] = jnp.dot(xc.astype(BF16), gw_ref[dirn, hd], preferred_element_type=F32)

    def gates_and_scan():
        sub = lax.broadcasted_iota(jnp.int32, (N_SEG, head), 0)
        xc, g = held["xc"], held["g"]
        r = jax.nn.sigmoid(g[:, :head] + gb_ref[dirn, 0:1, cols])
        i = jax.nn.sigmoid(g[:, head:] + gb_ref[dirn, 1:2, cols])
        log_a = r * (-RG_C * jax.nn.softplus(-lam_ref[dirn:dirn + 1, cols]))
        a = jnp.exp(log_a)
        m2 = jnp.maximum(-jnp.tanh(log_a) * (a * a + 1.0), 0.0)
        mult = jnp.where(m2 > 0.0, m2 * lax.rsqrt(m2), 0.0)
        bx = mult * (i * xc)

        a3 = a.reshape(tl, N_SEG, head)
        b3 = bx.reshape(tl, N_SEG, head)
        hloc = [None] * tl
        prod = [None] * tl
        hprev = pprev = None
        for t in steps:
            if hprev is None:
                hprev, pprev = b3[t], a3[t]
            else:
                hprev, pprev = a3[t] * hprev + b3[t], a3[t] * pprev
            hloc[t], prod[t] = hprev, pprev
        state = carry_scr[:, cols]
        init = jnp.zeros((N_SEG, head), F32)
        for sg in segs:
            init = jnp.where(sub == sg, state, init)
            state = hprev[sg:sg + 1, :] + pprev[sg:sg + 1, :] * state
        carry_scr[:, cols] = state
        hfull = jnp.concatenate([hloc[t] + prod[t] * init for t in range(tl)], axis=0)
        if not reverse:
            hf_scr[hf_rows, cols] = hfull
        else:
            hs = hs_scr.at[hd]
            hs[...] = hf_scr[hf_rows, cols] + hfull
            for sg in range(N_SEG):
                rows = slice(sg * tl, (sg + 1) * tl)
                hsum = hs[pl.ds(sg, tl, stride=N_SEG), :]
                mix_ref[rows, cols] = (gy_ref[rows, cols] * hsum).astype(BF16)

    return [conv_and_gate_matmul, gates_and_scan]


def _sgu_head(hd, u_ref, v_ref, sw_ref, sb_ref, mix_ref):
    tc, d_ch = u_ref.shape
    ch_head = d_ch // N_CH_HEADS
    d_rg = mix_ref.shape[1] - d_ch
    cols = slice(hd * ch_head, (hd + 1) * ch_head)
    for n in range(tc // CHUNK):
        rows = slice(n * CHUNK, (n + 1) * CHUNK)
        mixed = jnp.dot(sw_ref[hd], v_ref[rows, cols], preferred_element_type=F32) + sb_ref[hd]
        mix_ref[rows, d_rg + hd * ch_head:d_rg + (hd + 1) * ch_head] = (
            u_ref[rows, cols] * mixed).astype(BF16)


def _seq_items(reverse, edges, hf_rows, x_ref, gy_ref, u_ref, v_ref,
               cw_ref, cb_ref, gw_ref, gb_ref, lam_ref, sw_ref, sb_ref,
               mix_ref, hf_scr, carry_scr, xi_scr, hs_scr, fine):
    def together(parts):
        def run():
            for part in parts:
                part()
        return run

    items = []
    for hd in range(N_RG_HEADS):
        parts = _scan_head(hd, reverse, edges, hf_rows, x_ref, gy_ref, cw_ref, cb_ref, gw_ref, gb_ref,
                           lam_ref, mix_ref, hf_scr, carry_scr, xi_scr, hs_scr)
        if reverse:
            parts.append(functools.partial(_sgu_head, hd, u_ref, v_ref, sw_ref, sb_ref, mix_ref))
        items += parts if fine else [together(parts)]
    return items


def _seq_weight_specs(sp):
    full = lambda a: pl.BlockSpec(a.shape, lambda *_: (0,) * a.ndim, pipeline_mode=pl.Buffered(1))
    return [full(sp[k]) for k in ("conv_w", "conv_b", "gw", "gb", "lam", "sgu_w", "sgu_b")]


def _seq_weights(sp):
    return [sp[k] for k in ("conv_w", "conv_b", "gw", "gb", "lam", "sgu_w", "sgu_b")]


def _seq_scratch(seq_len, tc, d_rg):
    head = d_rg // N_RG_HEADS
    return [
        pltpu.VMEM((seq_len, d_rg), F32),
        pltpu.VMEM((1, d_rg), F32),
        pltpu.VMEM((N_RG_HEADS, tc + (CONV_W - 1) * N_SEG, head), F32),
        pltpu.VMEM((N_RG_HEADS, tc, head), F32),
    ]


N_INPROJ_IN = 5
N_INPROJ_OUT = 4
N_SEQ_IN = 12
N_SEQ_OUT = 2
N_SEQ_SCRATCH = 4


def _inproj_kernel(*refs, n_cast, with_seq):
    s = pl.program_id(0)
    refs = list(refs)
    take = lambda k: [refs.pop(0) for _ in range(k)]
    proj_in = take(N_INPROJ_IN)
    seq_in = take(N_SEQ_IN) if with_seq else None
    cast_src = take(n_cast)
    proj_out = take(N_INPROJ_OUT)
    seq_out = take(N_SEQ_OUT) if with_seq else None
    cast_dst = take(n_cast)
    casts = _cast_items(list(zip(cast_src, cast_dst)))
    if not with_seq:
        _inproj_body(*proj_in, *proj_out, between=casts)
        return

    x_ref, gy_ref, u_ref, v_ref, h0_ref = seq_in[:5]
    weights = seq_in[5:]
    mix_ref, st_ref = seq_out
    hf_scr, carry_scr, xi_scr, hs_scr = take(N_SEQ_SCRATCH)
    tc = x_ref.shape[0]

    def direction(reverse):
        dirn = 1 if reverse else 0
        work = _seq_items(reverse, None, slice(0, tc), x_ref, gy_ref, u_ref, v_ref, *weights,
                          mix_ref, hf_scr, carry_scr, xi_scr, hs_scr, fine=False)

        def first():
            carry_scr[...] = h0_ref[dirn:dirn + 1, :]
            work[0]()

        def last():
            work[-1]()
            st_ref[dirn:dirn + 1, :] = carry_scr[...]

        items = [first] + work[1:-1] + [last]
        _inproj_body(*proj_in, *proj_out, between=_merge(items, casts) if casts else items)

    @pl.when(s % 2 == 0)
    def _():
        direction(False)

    @pl.when(s % 2 == 1)
    def _():
        direction(True)


def _inproj_seq(x, mod, grp, norm_g, w_in, seq, sp, to_cast):
    m, d = x.shape
    n_steps = m // IN_ROW_TILE
    in_specs, out_specs, out_shape = _inproj_specs(m, d, w_in.shape[1], grp)
    args = [x, norm_g.reshape(1, d), mod, mod, w_in]
    scratch = []
    if seq is not None:
        xrg, gy, u, v, h0, seq_len = seq
        ms, d_rg = xrg.shape
        assert 2 * (ms // seq_len) == n_steps
        blk = lambda i: (i // 2, 0)
        per_seq = pl.BlockSpec((None, 2, d_rg), lambda i: (i // 2, 0, 0))
        in_specs = in_specs + [pl.BlockSpec((seq_len, d_rg), blk)] * 4 + [per_seq] + _seq_weight_specs(sp)
        args = args + [xrg, gy, u, v, h0] + _seq_weights(sp)
        out_specs = out_specs + [pl.BlockSpec((seq_len, 2 * d_rg), blk), per_seq]
        out_shape = out_shape + [jax.ShapeDtypeStruct((ms, 2 * d_rg), BF16),
                                 jax.ShapeDtypeStruct((ms // seq_len, 2, d_rg), F32)]
        scratch = _seq_scratch(seq_len, seq_len, d_rg)
    for w in to_cast:
        rows, cols = w.shape[0] // n_steps, w.shape[1]
        assert rows * n_steps == w.shape[0] and rows % BF16_ROWS == 0
        in_specs = in_specs + [pl.BlockSpec((rows, cols), lambda i: (i, 0))]
        args = args + [w]
    out_specs = out_specs + [pl.BlockSpec((w.shape[0] // n_steps, w.shape[1]), lambda i: (i, 0))
                             for w in to_cast]
    out_shape = out_shape + [jax.ShapeDtypeStruct(w.shape, BF16) for w in to_cast]
    return pl.pallas_call(
        functools.partial(_inproj_kernel, n_cast=len(to_cast), with_seq=seq is not None),
        grid=(n_steps,),
        in_specs=in_specs,
        out_specs=out_specs,
        out_shape=out_shape,
        scratch_shapes=scratch,
        compiler_params=_params(dimension_semantics=("arbitrary",)),
        name="in_projection" if seq is None else "in_projection_seqmix",
    )(*args)


def _ffn_first(x_ref, mix_ref, g1_ref, sc_ref, sh_ref, ng_ref, wo_ref, o_ref, h2_scr, acc_scr):
    mix = jnp.dot(mix_ref[...], wo_ref[...], preferred_element_type=F32)
    x1 = x_ref[...] + g1_ref[...] * mix
    o_ref[...] = x1
    h2 = (_rms(x1) * ng_ref[...]) * (1.0 + sc_ref[...]) + sh_ref[...]
    h2_scr[...] = h2.astype(BF16)
    acc_scr[...] = jnp.zeros_like(acc_scr)


def _ffn_step(w1_ref, w2_ref, h2_scr, acc_scr, between=()):
    n_w1 = w1_ref.shape[1] // FF_PIECE
    w2_parts = w2_ref.shape[1] // FF_PIECE if between else W2_PARTS
    n_pieces = n_w1 + w2_parts
    work = _spread(list(between), n_pieces) if between else [[]] * n_pieces
    part_d = w2_ref.shape[1] // w2_parts

    h2 = h2_scr[...]
    acts = []
    for n in range(n_w1):
        for item in work[n]:
            item()
        a = jnp.dot(h2, w1_ref[:, n * FF_PIECE:(n + 1) * FF_PIECE], preferred_element_type=F32)
        acts.append(jnp.square(jnp.maximum(a, 0.0)).astype(BF16))
    act = jnp.concatenate(acts, axis=1)
    for n in range(w2_parts):
        for item in work[n_w1 + n]:
            item()
        cols = slice(n * part_d, (n + 1) * part_d)
        acc_scr[:, cols] += jnp.dot(act, w2_ref[:, cols], preferred_element_type=F32)


def _ffn_last(g2_ref, fg_ref, o_ref, acc_scr, final_norm):
    x2 = o_ref[...] + g2_ref[...] * acc_scr[...]
    if final_norm:
        x2 = _rms(x2) * fg_ref[...]
    o_ref[...] = x2


def _ffn_kernel(x_ref, mix_ref, g1_ref, sc_ref, sh_ref, g2_ref, ng_ref, fg_ref,
                wo_ref, w1_ref, w2_ref, o_ref, h2_scr, acc_scr, *, final_norm):
    j = pl.program_id(1)

    @pl.when(j == 0)
    def _():
        _ffn_first(x_ref, mix_ref, g1_ref, sc_ref, sh_ref, ng_ref, wo_ref, o_ref, h2_scr, acc_scr)

    _ffn_step(w1_ref, w2_ref, h2_scr, acc_scr)

    @pl.when(j == pl.num_programs(1) - 1)
    def _():
        _ffn_last(g2_ref, fg_ref, o_ref, acc_scr, final_norm)


def _ffn_seq_kernel(x_ref, mix_ref, g1_ref, sc_ref, sh_ref, g2_ref, ng_ref, fg_ref,
                    wo_ref, w1_ref, w2_ref,
                    sx_ref, sprev_ref, snext_ref, gy_ref, u_ref, v_ref, h0_ref,
                    cw_ref, cb_ref, gw_ref, gb_ref, lam_ref, sw_ref, sb_ref,
                    o_ref, smix_ref,
                    h2_scr, acc_scr, hf_scr, carry_scr, xi_scr, hs_scr, *, final_norm):
    i = pl.program_id(0)
    j = pl.program_id(1)
    n_chunks = pl.num_programs(1)
    tc = sx_ref.shape[0]
    backward = i % 2 == 1
    jj = jnp.where(backward, n_chunks - 1 - j, j)
    hf_rows = pl.ds(pl.multiple_of(jj * tc, tc), tc)
    edges = (jj == 0, sprev_ref, jj == n_chunks - 1, snext_ref)

    @pl.when(j == 0)
    def _():
        _ffn_first(x_ref, mix_ref, g1_ref, sc_ref, sh_ref, ng_ref, wo_ref, o_ref, h2_scr, acc_scr)
        carry_scr[...] = h0_ref[pl.ds(i % 2, 1), :]

    def both(reverse):
        items = _seq_items(reverse, edges, hf_rows, sx_ref, gy_ref, u_ref, v_ref,
                           cw_ref, cb_ref, gw_ref, gb_ref, lam_ref, sw_ref, sb_ref,
                           smix_ref, hf_scr, carry_scr, xi_scr, hs_scr, fine=True)
        _ffn_step(w1_ref, w2_ref, h2_scr, acc_scr, between=items)

    @pl.when(jnp.logical_not(backward))
    def _():
        both(False)

    @pl.when(backward)
    def _():
        both(True)

    @pl.when(j == n_chunks - 1)
    def _():
        _ffn_last(g2_ref, fg_ref, o_ref, acc_scr, final_norm)


def _ffn(x, mix, mod, grp, norm_g, final_g, w_out, w_ff1, w_ff2, final_norm, ff_tile, seq, sp):
    m, d = x.shape
    d_mix = mix.shape[1]
    n_steps = w_ff1.shape[1] // ff_tile
    row = lambda i, j: (i, 0)
    vec = pl.BlockSpec((1, d), lambda i, j: (0, 0))
    in_specs = [
        pl.BlockSpec((ROW_TILE, d), row),
        pl.BlockSpec((ROW_TILE, d_mix), row),
        _mod_spec(2, grp, d),
        _mod_spec(4, grp, d),
        _mod_spec(3, grp, d),
        _mod_spec(5, grp, d),
        vec,
        vec,
        pl.BlockSpec((d_mix, d), lambda i, j: (0, 0), pipeline_mode=pl.Buffered(1)),
        pl.BlockSpec((d, ff_tile), lambda i, j: (0, j)),
        pl.BlockSpec((ff_tile, d), lambda i, j: (j, 0)),
    ]
    args = [x, mix, mod, mod, mod, mod, norm_g.reshape(1, d), final_g.reshape(1, d), w_out, w_ff1, w_ff2]
    out_specs = [pl.BlockSpec((ROW_TILE, d), row)]
    out_shape = [jax.ShapeDtypeStruct((m, d), F32)]
    scratch = [
        pltpu.VMEM((ROW_TILE, d), BF16),
        pltpu.VMEM((ROW_TILE, d), F32),
    ]
    if seq is None:
        kern, name = _ffn_kernel, "out_projection_ffn"
    else:
        kern, name = _ffn_seq_kernel, "out_projection_ffn_seqmix"
        xrg, gy, u, v, h0, seq_len = seq
        ms, d_rg = xrg.shape
        n_seq = ms // seq_len
        tc = seq_len // n_steps
        assert 2 * n_seq == m // ROW_TILE and tc % (N_SEG * SUBLANES) == 0 and tc % CHUNK == 0
        per_tile = tc // HALO
        n_halo = ms // HALO

        def chunk(i, j):
            return (i // 2) * n_steps + jnp.where(i % 2 == 1, n_steps - 1 - j, j)

        def out_chunk(i, j):
            return ((i // 2) * n_steps + jnp.where(i % 2 == 1, n_steps - 1 - j, n_steps - 1), 0)

        blk = lambda rows, cols, index: pl.BlockSpec((rows, cols), index)
        in_specs = in_specs + [
            blk(tc, d_rg, lambda i, j: (chunk(i, j), 0)),
            blk(HALO, d_rg, lambda i, j: (jnp.maximum(chunk(i, j) * per_tile - 1, 0), 0)),
            blk(HALO, d_rg, lambda i, j: (jnp.minimum((chunk(i, j) + 1) * per_tile, n_halo - 1), 0)),
            blk(tc, d_rg, out_chunk),
            blk(tc, d_rg, out_chunk),
            blk(tc, d_rg, out_chunk),
            pl.BlockSpec((None, 2, d_rg), lambda i, j: (i // 2, 0, 0)),
        ] + _seq_weight_specs(sp)
        args = args + [xrg, xrg, xrg, gy, u, v, h0] + _seq_weights(sp)
        out_specs = out_specs + [blk(tc, 2 * d_rg, out_chunk)]
        out_shape = out_shape + [jax.ShapeDtypeStruct((ms, 2 * d_rg), BF16)]
        scratch = scratch + _seq_scratch(seq_len, tc, d_rg)
    return pl.pallas_call(
        functools.partial(kern, final_norm=final_norm),
        grid=(m // ROW_TILE, n_steps),
        in_specs=in_specs,
        out_specs=out_specs,
        out_shape=out_shape,
        scratch_shapes=scratch,
        compiler_params=_params(dimension_semantics=("arbitrary", "arbitrary")),
        name=name,
    )(*args)


def kernel(x_prompt, x_sample, c, state_rglru, c_ctx, norm1_g, w_ada, b_ada, w_in, conv_w, conv_b,
           ga_w, ga_b, gi_w, gi_b, lru_lambda, sgu_w, sgu_b, w_out, norm2_g, w_ff1, w_ff2, final_g):
    b_ctx, l_ctx, d = x_prompt.shape
    b_lat, l_lat, _ = x_sample.shape
    depth = w_in.shape[0]
    d_rg = lru_lambda.shape[-1]
    assert b_lat + 1 <= MOD_ROWS
    assert (b_ctx * l_ctx) % ROW_TILE == 0 and l_lat % ROW_TILE == 0

    grp_ctx = (b_lat, b_ctx * l_ctx)
    grp_lat = (0, l_lat)
    xp = x_prompt.reshape(b_ctx * l_ctx, d)
    xs = x_sample.reshape(b_lat * l_lat, d)
    h0_ctx = jnp.zeros((b_ctx, 2, d_rg), F32)
    ctx_states = []
    for l in range(depth):
        sp = {
            "conv_w": conv_w[l], "conv_b": conv_b[l].reshape(1, d_rg),
            "gw": jnp.concatenate([ga_w[l], gi_w[l]], axis=-1).astype(BF16),
            "gb": jnp.stack([ga_b[l], gi_b[l]], axis=1),
            "lam": lru_lambda[l],
            "sgu_w": sgu_w[l].astype(BF16), "sgu_b": sgu_b[l][:, :, None],
        }
        win = w_in[l].astype(BF16)
        last = l == depth - 1
        mod = _modulation(c, c_ctx.reshape(1, d), w_ada[l], b_ada[l])
        h0_lat = state_rglru[:, l].astype(F32)

        gy_c, xrg_c, u_c, v_c, wff2, wout = _inproj_seq(
            xp, mod, grp_ctx, norm1_g[l], win, None, None, [w_ff2[l], w_out[l]])
        gy_l, xrg_l, u_l, v_l, mix_c, st, wff1 = _inproj_seq(
            xs, mod, grp_lat, norm1_g[l], win, (xrg_c, gy_c, u_c, v_c, h0_ctx, l_ctx), sp, [w_ff1[l]])
        xp, mix_l = _ffn(xp, mix_c, mod, grp_ctx, norm2_g[l], final_g, wout, wff1, wff2, last,
                         FF_TILE_FUSED, (xrg_l, gy_l, u_l, v_l, h0_lat, l_lat), sp)
        xs, = _ffn(xs, mix_l, mod, grp_lat, norm2_g[l], final_g, wout, wff1, wff2, last,
                   FF_TILE_ALONE, None, None)
        ctx_states.append(st)
    y_prompt = xp.reshape(b_ctx, l_ctx, d)
    y_sample = xs.reshape(b_lat, l_lat, d)
    new_state = jnp.stack(ctx_states, axis=1).astype(x_prompt.dtype)
    return (y_prompt, y_sample, new_state)
```

```python
import functools

import jax
import jax.numpy as jnp
from jax import lax
from jax.experimental import pallas as pl
from jax.experimental.pallas import tpu as pltpu

F32 = jnp.float32
BF16 = jnp.bfloat16

N_RG_HEADS = 8
N_CH_HEADS = 8
CHUNK = 128
CONV_W = 4
CONV_LEFT = 2
RG_C = 8.0
N_MOD = 6
EPS = 1e-6

SUBLANES = 8
BF16_ROWS = 16
MOD_ROWS = 8
VMEM_LIMIT_BYTES = 62 * 1024 * 1024

ROW_TILE = 512
IN_ROW_TILE = 256
CAST_ITEM_BYTES = 1 << 19
FF_TILE_FUSED = 512
FF_TILE_ALONE = 1024
FF_PIECE = 256
W2_PARTS = 4
MOD_COL_TILE = 1024
PROJ_PIECE = 256
N_SEG = SUBLANES
HALO = SUBLANES


def _rms(x):
    return x * lax.rsqrt(jnp.mean(x * x, axis=-1, keepdims=True) + EPS)


def _params(**kw):
    return pltpu.CompilerParams(vmem_limit_bytes=VMEM_LIMIT_BYTES, **kw)


def _spread(items, n_slots):
    slots = [[] for _ in range(n_slots)]
    for k, item in enumerate(items):
        slots[k * n_slots // len(items)].append(item)
    return slots


def _merge(a, b):
    keyed = [((k + 0.5) / len(a), 0, k, item) for k, item in enumerate(a)]
    keyed += [((k + 0.5) / len(b), 1, k, item) for k, item in enumerate(b)]
    return [entry[-1] for entry in sorted(keyed, key=lambda entry: entry[:3])]


def _mod_kernel(c_ref, cc_ref, w_ref, b_ref, o_ref, cond_scr):
    n_lat = c_ref.shape[0]
    cond_scr[...] = jnp.zeros_like(cond_scr)
    cond_scr[0:n_lat, :] = c_ref[...]
    cond_scr[n_lat:n_lat + 1, :] = cc_ref[...]
    s = jax.nn.silu(cond_scr[...]).astype(BF16)
    val = jnp.dot(s, w_ref[...].astype(BF16), preferred_element_type=F32) + b_ref[...]
    for r in range(MOD_ROWS):
        o_ref[r] = val[r:r + 1, :]


def _modulation(c, c_ctx, w_ada, b_ada):
    n_lat, d = c.shape
    n = w_ada.shape[1]
    per = d // MOD_COL_TILE
    assert d % MOD_COL_TILE == 0 and n == N_MOD * d and n_lat + 1 <= MOD_ROWS
    return pl.pallas_call(
        _mod_kernel,
        grid=(n // MOD_COL_TILE,),
        in_specs=[
            pl.BlockSpec((n_lat, d), lambda j: (0, 0)),
            pl.BlockSpec((1, d), lambda j: (0, 0)),
            pl.BlockSpec((d, MOD_COL_TILE), lambda j: (0, j)),
            pl.BlockSpec((1, MOD_COL_TILE), lambda j: (0, j)),
        ],
        out_specs=pl.BlockSpec((MOD_ROWS, None, 1, MOD_COL_TILE), lambda j: (0, j // per, 0, j % per)),
        out_shape=jax.ShapeDtypeStruct((MOD_ROWS, N_MOD, 1, d), F32),
        scratch_shapes=[pltpu.VMEM((MOD_ROWS, d), F32)],
        compiler_params=_params(dimension_semantics=("arbitrary",)),
        name="modulation",
    )(c, c_ctx, w_ada, b_ada.reshape(1, n))


def _mod_spec(which, grp, d, row_tile=ROW_TILE):
    mod_base, rows_per_mod = grp

    def index(i, *_):
        return (mod_base + (i * row_tile) // rows_per_mod, which, 0, 0)
    return pl.BlockSpec((None, None, 1, d), index)


def _cast_items(pairs):
    items = []
    for src, dst in pairs:
        rows = max(CAST_ITEM_BYTES // (4 * src.shape[1]), BF16_ROWS)
        for r in range(0, src.shape[0], rows):
            def run(src=src, dst=dst, part=slice(r, min(r + rows, src.shape[0]))):
                dst[part, :] = src[part, :].astype(BF16)
            items.append(run)
    return items


def _inproj_body(x_ref, g_ref, sc_ref, sh_ref, w_ref, gy_ref, xrg_ref, u_ref, v_ref, between=()):
    d_rg = gy_ref.shape[1]
    h = (_rms(x_ref[...]) * g_ref[...]) * (1.0 + sc_ref[...]) + sh_ref[...]
    hb = h.astype(BF16)
    outs = (gy_ref, xrg_ref, u_ref, v_ref)
    n_pieces = w_ref.shape[1] // PROJ_PIECE
    work = _spread(list(between), n_pieces) if between else [[]] * n_pieces
    for n in range(n_pieces):
        part = jnp.dot(hb, w_ref[:, n * PROJ_PIECE:(n + 1) * PROJ_PIECE], preferred_element_type=F32)
        dest, col = divmod(n * PROJ_PIECE, d_rg)
        if dest != 1:
            part = jax.nn.gelu(part)
        outs[dest][:, col:col + PROJ_PIECE] = part.astype(outs[dest].dtype)
        for item in work[n]:
            item()


def _inproj_specs(m, d, d_in, grp):
    d_rg = d_in // 4
    row = lambda i: (i, 0)
    in_specs = [
        pl.BlockSpec((IN_ROW_TILE, d), row),
        pl.BlockSpec((1, d), lambda i: (0, 0)),
        _mod_spec(1, grp, d, IN_ROW_TILE),
        _mod_spec(0, grp, d, IN_ROW_TILE),
        pl.BlockSpec((d, d_in), lambda i: (0, 0), pipeline_mode=pl.Buffered(1)),
    ]
    out_specs = [pl.BlockSpec((IN_ROW_TILE, d_rg), row)] * 4
    out_shape = [
        jax.ShapeDtypeStruct((m, d_rg), F32),
        jax.ShapeDtypeStruct((m, d_rg), F32),
        jax.ShapeDtypeStruct((m, d_rg), F32),
        jax.ShapeDtypeStruct((m, d_rg), BF16),
    ]
    return in_specs, out_specs, out_shape


def _scan_head(hd, reverse, edges, hf_rows, x_ref, gy_ref, cw_ref, cb_ref, gw_ref, gb_ref, lam_ref,
               mix_ref, hf_scr, carry_scr, xi_scr, hs_scr):
    tc, d_rg = x_ref.shape
    tl = tc // N_SEG
    head = d_rg // N_RG_HEADS
    lead = CONV_LEFT * N_SEG
    dirn = 1 if reverse else 0
    steps = range(tl - 1, -1, -1) if reverse else range(tl)
    segs = range(N_SEG - 1, -1, -1) if reverse else range(N_SEG)
    cols = slice(hd * head, (hd + 1) * head)
    xi = xi_scr.at[hd]
    held = {}

    def step_rows(t):
        return xi[lead + t * N_SEG:lead + (t + 1) * N_SEG, :]

    def conv_and_gate_matmul():
        sub = lax.broadcasted_iota(jnp.int32, (N_SEG, head), 0)
        for sg in range(N_SEG):
            xi[pl.ds(lead + sg, tl, stride=N_SEG), :] = x_ref[sg * tl:(sg + 1) * tl, cols]
        for k in range(CONV_LEFT):
            if edges is None:
                edge = 0.0
            else:
                row = HALO - CONV_LEFT + k
                edge = jnp.where(edges[0], 0.0, edges[1][row:row + 1, cols])
            prev_step = pltpu.roll(step_rows(tl - CONV_LEFT + k), 1, axis=0)
            xi[k * N_SEG:(k + 1) * N_SEG, :] = jnp.where(sub == 0, edge, prev_step)
        for k in range(CONV_W - 1 - CONV_LEFT):
            edge = 0.0 if edges is None else jnp.where(edges[2], 0.0, edges[3][k:k + 1, cols])
            next_step = pltpu.roll(step_rows(k), N_SEG - 1, axis=0)
            xi[lead + (tl + k) * N_SEG:lead + (tl + k + 1) * N_SEG, :] = jnp.where(
                sub == N_SEG - 1, edge, next_step)
        xc = cb_ref[:, cols]
        for k in range(CONV_W):
            xc = xc + cw_ref[k:k + 1, cols] * xi[k * N_SEG:k * N_SEG + tc, :]
        held["xc"] = xc
        gw = jnp.concatenate([ref[dirn, hd] for ref in gw_ref], axis=-1)
        held["g"] = jnp.dot(xc.astype(BF16), gw, preferred_element_type=F32)

    def gates_and_scan():
        sub = lax.broadcasted_iota(jnp.int32, (N_SEG, head), 0)
        xc, g = held["xc"], held["g"]
        r = jax.nn.sigmoid(g[:, :head] + gb_ref[dirn, 0:1, cols])
        i = jax.nn.sigmoid(g[:, head:] + gb_ref[dirn, 1:2, cols])
        log_a = r * (-RG_C * jax.nn.softplus(-lam_ref[dirn:dirn + 1, cols]))
        a = jnp.exp(log_a)
        m2 = jnp.maximum(-jnp.tanh(log_a) * (a * a + 1.0), 0.0)
        mult = jnp.where(m2 > 0.0, m2 * lax.rsqrt(m2), 0.0)
        bx = mult * (i * xc)

        a3 = a.reshape(tl, N_SEG, head)
        b3 = bx.reshape(tl, N_SEG, head)
        hloc = [None] * tl
        prod = [None] * tl
        hprev = pprev = None
        for t in steps:
            if hprev is None:
                hprev, pprev = b3[t], a3[t]
            else:
                hprev, pprev = a3[t] * hprev + b3[t], a3[t] * pprev
            hloc[t], prod[t] = hprev, pprev
        state = carry_scr[:, cols]
        init = jnp.zeros((N_SEG, head), F32)
        for sg in segs:
            init = jnp.where(sub == sg, state, init)
            state = hprev[sg:sg + 1, :] + pprev[sg:sg + 1, :] * state
        carry_scr[:, cols] = state
        hfull = jnp.concatenate([hloc[t] + prod[t] * init for t in range(tl)], axis=0)
        if not reverse:
            hf_scr[hf_rows, cols] = hfull
        else:
            hs = hs_scr.at[hd]
            hs[...] = hf_scr[hf_rows, cols] + hfull
            for sg in range(N_SEG):
                rows = slice(sg * tl, (sg + 1) * tl)
                hsum = hs[pl.ds(sg, tl, stride=N_SEG), :]
                mix_ref[rows, cols] = (gy_ref[rows, cols] * hsum).astype(BF16)

    return [conv_and_gate_matmul, gates_and_scan]


def _sgu_head(hd, u_ref, v_ref, sw_ref, sb_ref, mix_ref):
    tc, d_ch = u_ref.shape
    ch_head = d_ch // N_CH_HEADS
    d_rg = mix_ref.shape[1] - d_ch
    cols = slice(hd * ch_head, (hd + 1) * ch_head)
    for n in range(tc // CHUNK):
        rows = slice(n * CHUNK, (n + 1) * CHUNK)
        mixed = jnp.dot(sw_ref[hd], v_ref[rows, cols], preferred_element_type=F32) + sb_ref[hd]
        mix_ref[rows, d_rg + hd * ch_head:d_rg + (hd + 1) * ch_head] = (
            u_ref[rows, cols] * mixed).astype(BF16)


def _seq_items(reverse, edges, hf_rows, x_ref, gy_ref, u_ref, v_ref,
               cw_ref, cb_ref, gw_ref, gb_ref, lam_ref, sw_ref, sb_ref,
               mix_ref, hf_scr, carry_scr, xi_scr, hs_scr, fine):
    def together(parts):
        def run():
            for part in parts:
                part()
        return run

    items = []
    for hd in range(N_RG_HEADS):
        parts = _scan_head(hd, reverse, edges, hf_rows, x_ref, gy_ref, cw_ref, cb_ref, gw_ref, gb_ref,
                           lam_ref, mix_ref, hf_scr, carry_scr, xi_scr, hs_scr)
        if reverse:
            parts.append(functools.partial(_sgu_head, hd, u_ref, v_ref, sw_ref, sb_ref, mix_ref))
        items += parts if fine else [together(parts)]
    return items


SEQ_WEIGHTS = ("conv_w", "conv_b", "ga", "gi", "gb", "lam", "sgu_w", "sgu_b")


def _seq_weight_specs(sp):
    full = lambda a: pl.BlockSpec(a.shape, lambda *_: (0,) * a.ndim, pipeline_mode=pl.Buffered(1))
    return [full(sp[k]) for k in SEQ_WEIGHTS]


def _seq_weights(sp):
    return [sp[k] for k in SEQ_WEIGHTS]


def _seq_scratch(seq_len, tc, d_rg):
    head = d_rg // N_RG_HEADS
    return [
        pltpu.VMEM((seq_len, d_rg), F32),
        pltpu.VMEM((1, d_rg), F32),
        pltpu.VMEM((N_RG_HEADS, tc + (CONV_W - 1) * N_SEG, head), F32),
        pltpu.VMEM((N_RG_HEADS, tc, head), F32),
    ]


N_INPROJ_IN = 5
N_INPROJ_OUT = 4
N_SEQ_IN = 13
N_SEQ_OUT = 2
N_SEQ_SCRATCH = 4


def _inproj_kernel(*refs, n_cast, with_seq):
    s = pl.program_id(0)
    refs = list(refs)
    take = lambda k: [refs.pop(0) for _ in range(k)]
    proj_in = take(N_INPROJ_IN)
    seq_in = take(N_SEQ_IN) if with_seq else None
    cast_src = take(n_cast)
    proj_out = take(N_INPROJ_OUT)
    seq_out = take(N_SEQ_OUT) if with_seq else None
    cast_dst = take(n_cast)
    casts = _cast_items(list(zip(cast_src, cast_dst)))
    if not with_seq:
        _inproj_body(*proj_in, *proj_out, between=casts)
        return

    x_ref, gy_ref, u_ref, v_ref, h0_ref = seq_in[:5]
    cw_ref, cb_ref, ga_ref, gi_ref = seq_in[5:9]
    weights = [cw_ref, cb_ref, (ga_ref, gi_ref)] + seq_in[9:]
    mix_ref, st_ref = seq_out
    hf_scr, carry_scr, xi_scr, hs_scr = take(N_SEQ_SCRATCH)
    tc = x_ref.shape[0]

    def direction(reverse):
        dirn = 1 if reverse else 0
        work = _seq_items(reverse, None, slice(0, tc), x_ref, gy_ref, u_ref, v_ref, *weights,
                          mix_ref, hf_scr, carry_scr, xi_scr, hs_scr, fine=False)

        def first():
            carry_scr[...] = h0_ref[dirn:dirn + 1, :]
            work[0]()

        def last():
            work[-1]()
            st_ref[dirn:dirn + 1, :] = carry_scr[...]

        items = [first] + work[1:-1] + [last]
        _inproj_body(*proj_in, *proj_out, between=_merge(items, casts) if casts else items)

    @pl.when(s % 2 == 0)
    def _():
        direction(False)

    @pl.when(s % 2 == 1)
    def _():
        direction(True)


def _inproj_seq(x, mod, grp, norm_g, w_in, seq, sp, to_cast):
    m, d = x.shape
    n_steps = m // IN_ROW_TILE
    in_specs, out_specs, out_shape = _inproj_specs(m, d, w_in.shape[1], grp)
    args = [x, norm_g.reshape(1, d), mod, mod, w_in]
    scratch = []
    if seq is not None:
        xrg, gy, u, v, h0, seq_len = seq
        ms, d_rg = xrg.shape
        assert 2 * (ms // seq_len) == n_steps
        blk = lambda i: (i // 2, 0)
        per_seq = pl.BlockSpec((None, 2, d_rg), lambda i: (i // 2, 0, 0))
        in_specs = in_specs + [pl.BlockSpec((seq_len, d_rg), blk)] * 4 + [per_seq] + _seq_weight_specs(sp)
        args = args + [xrg, gy, u, v, h0] + _seq_weights(sp)
        out_specs = out_specs + [pl.BlockSpec((seq_len, 2 * d_rg), blk), per_seq]
        out_shape = out_shape + [jax.ShapeDtypeStruct((ms, 2 * d_rg), BF16),
                                 jax.ShapeDtypeStruct((ms // seq_len, 2, d_rg), F32)]
        scratch = _seq_scratch(seq_len, seq_len, d_rg)
    for w in to_cast:
        rows, cols = w.shape[0] // n_steps, w.shape[1]
        assert rows * n_steps == w.shape[0] and rows % BF16_ROWS == 0
        in_specs = in_specs + [pl.BlockSpec((rows, cols), lambda i: (i, 0))]
        args = args + [w]
    out_specs = out_specs + [pl.BlockSpec((w.shape[0] // n_steps, w.shape[1]), lambda i: (i, 0))
                             for w in to_cast]
    out_shape = out_shape + [jax.ShapeDtypeStruct(w.shape, BF16) for w in to_cast]
    return pl.pallas_call(
        functools.partial(_inproj_kernel, n_cast=len(to_cast), with_seq=seq is not None),
        grid=(n_steps,),
        in_specs=in_specs,
        out_specs=out_specs,
        out_shape=out_shape,
        scratch_shapes=scratch,
        compiler_params=_params(dimension_semantics=("arbitrary",)),
        name="in_projection" if seq is None else "in_projection_seqmix",
    )(*args)


def _ffn_first(x_ref, mix_ref, g1_ref, sc_ref, sh_ref, ng_ref, wo_ref, o_ref, h2_scr, acc_scr):
    mix = jnp.dot(mix_ref[...], wo_ref[...], preferred_element_type=F32)
    x1 = x_ref[...] + g1_ref[...] * mix
    o_ref[...] = x1
    h2 = (_rms(x1) * ng_ref[...]) * (1.0 + sc_ref[...]) + sh_ref[...]
    h2_scr[...] = h2.astype(BF16)
    acc_scr[...] = jnp.zeros_like(acc_scr)


def _ffn_step(w1_ref, w2_ref, h2_scr, acc_scr, between=()):
    n_w1 = w1_ref.shape[1] // FF_PIECE
    w2_parts = w2_ref.shape[1] // FF_PIECE if between else W2_PARTS
    n_pieces = n_w1 + w2_parts
    work = _spread(list(between), n_pieces) if between else [[]] * n_pieces
    part_d = w2_ref.shape[1] // w2_parts

    h2 = h2_scr[...]
    acts = []
    for n in range(n_w1):
        for item in work[n]:
            item()
        a = jnp.dot(h2, w1_ref[:, n * FF_PIECE:(n + 1) * FF_PIECE], preferred_element_type=F32)
        acts.append(jnp.square(jnp.maximum(a, 0.0)).astype(BF16))
    act = jnp.concatenate(acts, axis=1)
    for n in range(w2_parts):
        for item in work[n_w1 + n]:
            item()
        cols = slice(n * part_d, (n + 1) * part_d)
        acc_scr[:, cols] += jnp.dot(act, w2_ref[:, cols], preferred_element_type=F32)


def _ffn_last(g2_ref, fg_ref, o_ref, acc_scr, final_norm):
    x2 = o_ref[...] + g2_ref[...] * acc_scr[...]
    if final_norm:
        x2 = _rms(x2) * fg_ref[...]
    o_ref[...] = x2


def _ffn_kernel(x_ref, mix_ref, g1_ref, sc_ref, sh_ref, g2_ref, ng_ref, fg_ref,
                wo_ref, w1_ref, w2_ref, o_ref, h2_scr, acc_scr, *, final_norm):
    j = pl.program_id(1)

    @pl.when(j == 0)
    def _():
        _ffn_first(x_ref, mix_ref, g1_ref, sc_ref, sh_ref, ng_ref, wo_ref, o_ref, h2_scr, acc_scr)

    _ffn_step(w1_ref, w2_ref, h2_scr, acc_scr)

    @pl.when(j == pl.num_programs(1) - 1)
    def _():
        _ffn_last(g2_ref, fg_ref, o_ref, acc_scr, final_norm)


def _ffn_seq_kernel(x_ref, mix_ref, g1_ref, sc_ref, sh_ref, g2_ref, ng_ref, fg_ref,
                    wo_ref, w1_ref, w2_ref,
                    sx_ref, sprev_ref, snext_ref, gy_ref, u_ref, v_ref, h0_ref,
                    cw_ref, cb_ref, ga_ref, gi_ref, gb_ref, lam_ref, sw_ref, sb_ref,
                    o_ref, smix_ref,
                    h2_scr, acc_scr, hf_scr, carry_scr, xi_scr, hs_scr, *, final_norm):
    i = pl.program_id(0)
    j = pl.program_id(1)
    n_chunks = pl.num_programs(1)
    tc = sx_ref.shape[0]
    backward = i % 2 == 1
    jj = jnp.where(backward, n_chunks - 1 - j, j)
    hf_rows = pl.ds(pl.multiple_of(jj * tc, tc), tc)
    edges = (jj == 0, sprev_ref, jj == n_chunks - 1, snext_ref)

    @pl.when(j == 0)
    def _():
        _ffn_first(x_ref, mix_ref, g1_ref, sc_ref, sh_ref, ng_ref, wo_ref, o_ref, h2_scr, acc_scr)
        carry_scr[...] = h0_ref[pl.ds(i % 2, 1), :]

    def both(reverse):
        items = _seq_items(reverse, edges, hf_rows, sx_ref, gy_ref, u_ref, v_ref,
                           cw_ref, cb_ref, (ga_ref, gi_ref), gb_ref, lam_ref, sw_ref, sb_ref,
                           smix_ref, hf_scr, carry_scr, xi_scr, hs_scr, fine=True)
        _ffn_step(w1_ref, w2_ref, h2_scr, acc_scr, between=items)

    @pl.when(jnp.logical_not(backward))
    def _():
        both(False)

    @pl.when(backward)
    def _():
        both(True)

    @pl.when(j == n_chunks - 1)
    def _():
        _ffn_last(g2_ref, fg_ref, o_ref, acc_scr, final_norm)


def _ffn(x, mix, mod, grp, norm_g, final_g, w_out, w_ff1, w_ff2, final_norm, ff_tile, seq, sp):
    m, d = x.shape
    d_mix = mix.shape[1]
    n_steps = w_ff1.shape[1] // ff_tile
    row = lambda i, j: (i, 0)
    vec = pl.BlockSpec((1, d), lambda i, j: (0, 0))
    in_specs = [
        pl.BlockSpec((ROW_TILE, d), row),
        pl.BlockSpec((ROW_TILE, d_mix), row),
        _mod_spec(2, grp, d),
        _mod_spec(4, grp, d),
        _mod_spec(3, grp, d),
        _mod_spec(5, grp, d),
        vec,
        vec,
        pl.BlockSpec((d_mix, d), lambda i, j: (0, 0), pipeline_mode=pl.Buffered(1)),
        pl.BlockSpec((d, ff_tile), lambda i, j: (0, j)),
        pl.BlockSpec((ff_tile, d), lambda i, j: (j, 0)),
    ]
    args = [x, mix, mod, mod, mod, mod, norm_g.reshape(1, d), final_g.reshape(1, d), w_out, w_ff1, w_ff2]
    out_specs = [pl.BlockSpec((ROW_TILE, d), row)]
    out_shape = [jax.ShapeDtypeStruct((m, d), F32)]
    scratch = [
        pltpu.VMEM((ROW_TILE, d), BF16),
        pltpu.VMEM((ROW_TILE, d), F32),
    ]
    if seq is None:
        kern, name = _ffn_kernel, "out_projection_ffn"
    else:
        kern, name = _ffn_seq_kernel, "out_projection_ffn_seqmix"
        xrg, gy, u, v, h0, seq_len = seq
        ms, d_rg = xrg.shape
        n_seq = ms // seq_len
        tc = seq_len // n_steps
        assert 2 * n_seq == m // ROW_TILE and tc % (N_SEG * SUBLANES) == 0 and tc % CHUNK == 0
        per_tile = tc // HALO
        n_halo = ms // HALO

        def chunk(i, j):
            return (i // 2) * n_steps + jnp.where(i % 2 == 1, n_steps - 1 - j, j)

        def out_chunk(i, j):
            return ((i // 2) * n_steps + jnp.where(i % 2 == 1, n_steps - 1 - j, n_steps - 1), 0)

        blk = lambda rows, cols, index: pl.BlockSpec((rows, cols), index)
        in_specs = in_specs + [
            blk(tc, d_rg, lambda i, j: (chunk(i, j), 0)),
            blk(HALO, d_rg, lambda i, j: (jnp.maximum(chunk(i, j) * per_tile - 1, 0), 0)),
            blk(HALO, d_rg, lambda i, j: (jnp.minimum((chunk(i, j) + 1) * per_tile, n_halo - 1), 0)),
            blk(tc, d_rg, out_chunk),
            blk(tc, d_rg, out_chunk),
            blk(tc, d_rg, out_chunk),
            pl.BlockSpec((None, 2, d_rg), lambda i, j: (i // 2, 0, 0)),
        ] + _seq_weight_specs(sp)
        args = args + [xrg, xrg, xrg, gy, u, v, h0] + _seq_weights(sp)
        out_specs = out_specs + [blk(tc, 2 * d_rg, out_chunk)]
        out_shape = out_shape + [jax.ShapeDtypeStruct((ms, 2 * d_rg), BF16)]
        scratch = scratch + _seq_scratch(seq_len, tc, d_rg)
    return pl.pallas_call(
        functools.partial(kern, final_norm=final_norm),
        grid=(m // ROW_TILE, n_steps),
        in_specs=in_specs,
        out_specs=out_specs,
        out_shape=out_shape,
        scratch_shapes=scratch,
        compiler_params=_params(dimension_semantics=("arbitrary", "arbitrary")),
        name=name,
    )(*args)


def kernel(x_prompt, x_sample, c, state_rglru, c_ctx, norm1_g, w_ada, b_ada, w_in, conv_w, conv_b,
           ga_w, ga_b, gi_w, gi_b, lru_lambda, sgu_w, sgu_b, w_out, norm2_g, w_ff1, w_ff2, final_g):
    b_ctx, l_ctx, d = x_prompt.shape
    b_lat, l_lat, _ = x_sample.shape
    depth = w_in.shape[0]
    d_rg = lru_lambda.shape[-1]
    assert b_lat + 1 <= MOD_ROWS
    assert (b_ctx * l_ctx) % ROW_TILE == 0 and l_lat % ROW_TILE == 0

    grp_ctx = (b_lat, b_ctx * l_ctx)
    grp_lat = (0, l_lat)
    xp = x_prompt.reshape(b_ctx * l_ctx, d)
    xs = x_sample.reshape(b_lat * l_lat, d)
    h0_ctx = jnp.zeros((b_ctx, 2, d_rg), F32)
    ctx_states = []
    for l in range(depth):
        flat = lambda w: w.reshape(-1, w.shape[-1])
        win = w_in[l].astype(BF16)
        last = l == depth - 1
        mod = _modulation(c, c_ctx.reshape(1, d), w_ada[l], b_ada[l])
        h0_lat = state_rglru[:, l].astype(F32)

        gy_c, xrg_c, u_c, v_c, wff2, wout, ga, gi, sgu = _inproj_seq(
            xp, mod, grp_ctx, norm1_g[l], win, None, None,
            [w_ff2[l], w_out[l], flat(ga_w[l]), flat(gi_w[l]), flat(sgu_w[l])])
        sp = {
            "conv_w": conv_w[l], "conv_b": conv_b[l].reshape(1, d_rg),
            "ga": ga.reshape(ga_w[l].shape), "gi": gi.reshape(gi_w[l].shape),
            "gb": jnp.stack([ga_b[l], gi_b[l]], axis=1),
            "lam": lru_lambda[l],
            "sgu_w": sgu.reshape(sgu_w[l].shape), "sgu_b": sgu_b[l][:, :, None],
        }
        gy_l, xrg_l, u_l, v_l, mix_c, st, wff1 = _inproj_seq(
            xs, mod, grp_lat, norm1_g[l], win, (xrg_c, gy_c, u_c, v_c, h0_ctx, l_ctx), sp, [w_ff1[l]])
        xp, mix_l = _ffn(xp, mix_c, mod, grp_ctx, norm2_g[l], final_g, wout, wff1, wff2, last,
                         FF_TILE_FUSED, (xrg_l, gy_l, u_l, v_l, h0_lat, l_lat), sp)
        xs, = _ffn(xs, mix_l, mod, grp_lat, norm2_g[l], final_g, wout, wff1, wff2, last,
                   FF_TILE_ALONE, None, None)
        ctx_states.append(st)
    y_prompt = xp.reshape(b_ctx, l_ctx, d)
    y_sample = xs.reshape(b_lat, l_lat, d)
    new_state = jnp.stack(ctx_states, axis=1).astype(x_prompt.dtype)
    return (y_prompt, y_sample, new_state)
```

```python
import functools

import jax
import jax.numpy as jnp
from jax import lax
from jax.experimental import pallas as pl
from jax.experimental.pallas import tpu as pltpu

F32 = jnp.float32
BF16 = jnp.bfloat16

N_RG_HEADS = 8
N_CH_HEADS = 8
CHUNK = 128
CONV_W = 4
CONV_LEFT = 2
RG_C = 8.0
N_MOD = 6
EPS = 1e-6

SUBLANES = 8
BF16_ROWS = 16
MOD_ROWS = 8
VMEM_LIMIT_BYTES = 62 * 1024 * 1024

ROW_TILE = 512
IN_ROW_TILE = 256
CAST_ITEM_BYTES = 1 << 19
FF_TILE_FUSED = 512
FF_TILE_ALONE = 1024
FF_PIECE = 256
W2_PARTS = 4
MOD_COL_TILE = 1024
PROJ_PIECE = 256
N_SEG = SUBLANES
HALO = SUBLANES


def _rms(x):
    return x * lax.rsqrt(jnp.mean(x * x, axis=-1, keepdims=True) + EPS)


def _params(**kw):
    return pltpu.CompilerParams(vmem_limit_bytes=VMEM_LIMIT_BYTES, **kw)


def _spread(items, n_slots):
    slots = [[] for _ in range(n_slots)]
    for k, item in enumerate(items):
        slots[k * n_slots // len(items)].append(item)
    return slots


def _merge(a, b):
    keyed = [((k + 0.5) / len(a), 0, k, item) for k, item in enumerate(a)]
    keyed += [((k + 0.5) / len(b), 1, k, item) for k, item in enumerate(b)]
    return [entry[-1] for entry in sorted(keyed, key=lambda entry: entry[:3])]


def _mod_kernel(c_ref, cc_ref, w_ref, b_ref, o_ref, cond_scr):
    n_lat = c_ref.shape[0]
    cond_scr[...] = jnp.zeros_like(cond_scr)
    cond_scr[0:n_lat, :] = c_ref[...]
    cond_scr[n_lat:n_lat + 1, :] = cc_ref[...]
    s = jax.nn.silu(cond_scr[...]).astype(BF16)
    val = jnp.dot(s, w_ref[...].astype(BF16), preferred_element_type=F32) + b_ref[...]
    for r in range(MOD_ROWS):
        o_ref[r] = val[r:r + 1, :]


def _modulation(c, c_ctx, w_ada, b_ada):
    n_lat, d = c.shape
    n = w_ada.shape[1]
    per = d // MOD_COL_TILE
    assert d % MOD_COL_TILE == 0 and n == N_MOD * d and n_lat + 1 <= MOD_ROWS
    return pl.pallas_call(
        _mod_kernel,
        grid=(n // MOD_COL_TILE,),
        in_specs=[
            pl.BlockSpec((n_lat, d), lambda j: (0, 0)),
            pl.BlockSpec((1, d), lambda j: (0, 0)),
            pl.BlockSpec((d, MOD_COL_TILE), lambda j: (0, j)),
            pl.BlockSpec((1, MOD_COL_TILE), lambda j: (0, j)),
        ],
        out_specs=pl.BlockSpec((MOD_ROWS, None, 1, MOD_COL_TILE), lambda j: (0, j // per, 0, j % per)),
        out_shape=jax.ShapeDtypeStruct((MOD_ROWS, N_MOD, 1, d), F32),
        scratch_shapes=[pltpu.VMEM((MOD_ROWS, d), F32)],
        compiler_params=_params(dimension_semantics=("arbitrary",)),
        name="modulation",
    )(c, c_ctx, w_ada, b_ada.reshape(1, n))


def _mod_spec(which, grp, d, row_tile=ROW_TILE):
    mod_base, rows_per_mod = grp

    def index(i, *_):
        return (mod_base + (i * row_tile) // rows_per_mod, which, 0, 0)
    return pl.BlockSpec((None, None, 1, d), index)


def _cast_items(pairs):
    items = []
    for src, dst in pairs:
        rows = max(CAST_ITEM_BYTES // (4 * src.shape[1]), BF16_ROWS)
        for r in range(0, src.shape[0], rows):
            def run(src=src, dst=dst, part=slice(r, min(r + rows, src.shape[0]))):
                dst[part, :] = src[part, :].astype(BF16)
            items.append(run)
    return items


def _inproj_body(x_ref, g_ref, sc_ref, sh_ref, w_ref, gy_ref, xrg_ref, u_ref, v_ref, between=()):
    d_rg = gy_ref.shape[1]
    h = (_rms(x_ref[...]) * g_ref[...]) * (1.0 + sc_ref[...]) + sh_ref[...]
    hb = h.astype(BF16)
    outs = (gy_ref, xrg_ref, u_ref, v_ref)
    n_pieces = w_ref.shape[1] // PROJ_PIECE
    work = _spread(list(between), n_pieces) if between else [[]] * n_pieces
    for n in range(n_pieces):
        part = jnp.dot(hb, w_ref[:, n * PROJ_PIECE:(n + 1) * PROJ_PIECE], preferred_element_type=F32)
        dest, col = divmod(n * PROJ_PIECE, d_rg)
        if dest != 1:
            part = jax.nn.gelu(part)
        outs[dest][:, col:col + PROJ_PIECE] = part.astype(outs[dest].dtype)
        for item in work[n]:
            item()


def _inproj_specs(m, d, d_in, grp):
    d_rg = d_in // 4
    row = lambda i: (i, 0)
    in_specs = [
        pl.BlockSpec((IN_ROW_TILE, d), row),
        pl.BlockSpec((1, d), lambda i: (0, 0)),
        _mod_spec(1, grp, d, IN_ROW_TILE),
        _mod_spec(0, grp, d, IN_ROW_TILE),
        pl.BlockSpec((d, d_in), lambda i: (0, 0), pipeline_mode=pl.Buffered(1)),
    ]
    out_specs = [pl.BlockSpec((IN_ROW_TILE, d_rg), row)] * 4
    out_shape = [
        jax.ShapeDtypeStruct((m, d_rg), F32),
        jax.ShapeDtypeStruct((m, d_rg), F32),
        jax.ShapeDtypeStruct((m, d_rg), F32),
        jax.ShapeDtypeStruct((m, d_rg), BF16),
    ]
    return in_specs, out_specs, out_shape


def _scan_head(hd, reverse, edges, hf_rows, x_ref, gy_ref, cw_ref, cb_ref, gw_ref, gb_ref, lam_ref,
               mix_ref, hf_scr, carry_scr, xi_scr, hs_scr):
    tc, d_rg = x_ref.shape
    tl = tc // N_SEG
    head = d_rg // N_RG_HEADS
    lead = CONV_LEFT * N_SEG
    dirn = 1 if reverse else 0
    steps = range(tl - 1, -1, -1) if reverse else range(tl)
    segs = range(N_SEG - 1, -1, -1) if reverse else range(N_SEG)
    cols = slice(hd * head, (hd + 1) * head)
    xi = xi_scr.at[hd]
    held = {}

    def step_rows(t):
        return xi[lead + t * N_SEG:lead + (t + 1) * N_SEG, :]

    def conv_and_gate_matmul():
        sub = lax.broadcasted_iota(jnp.int32, (N_SEG, head), 0)
        for sg in range(N_SEG):
            xi[pl.ds(lead + sg, tl, stride=N_SEG), :] = x_ref[sg * tl:(sg + 1) * tl, cols]
        for k in range(CONV_LEFT):
            if edges is None:
                edge = 0.0
            else:
                row = HALO - CONV_LEFT + k
                edge = jnp.where(edges[0], 0.0, edges[1][row:row + 1, cols])
            prev_step = pltpu.roll(step_rows(tl - CONV_LEFT + k), 1, axis=0)
            xi[k * N_SEG:(k + 1) * N_SEG, :] = jnp.where(sub == 0, edge, prev_step)
        for k in range(CONV_W - 1 - CONV_LEFT):
            edge = 0.0 if edges is None else jnp.where(edges[2], 0.0, edges[3][k:k + 1, cols])
            next_step = pltpu.roll(step_rows(k), N_SEG - 1, axis=0)
            xi[lead + (tl + k) * N_SEG:lead + (tl + k + 1) * N_SEG, :] = jnp.where(
                sub == N_SEG - 1, edge, next_step)
        xc = cb_ref[:, cols]
        for k in range(CONV_W):
            xc = xc + cw_ref[k:k + 1, cols] * xi[k * N_SEG:k * N_SEG + tc, :]
        held["xc"] = xc
        gw = jnp.concatenate([ref[dirn, hd] for ref in gw_ref], axis=-1)
        held["g"] = jnp.dot(xc.astype(BF16), gw, preferred_element_type=F32)

    def gates_and_scan():
        sub = lax.broadcasted_iota(jnp.int32, (N_SEG, head), 0)
        xc, g = held["xc"], held["g"]
        ga_b_ref, gi_b_ref = gb_ref
        r = jax.nn.sigmoid(g[:, :head] + ga_b_ref[dirn:dirn + 1, cols])
        i = jax.nn.sigmoid(g[:, head:] + gi_b_ref[dirn:dirn + 1, cols])
        log_a = r * (-RG_C * jax.nn.softplus(-lam_ref[dirn:dirn + 1, cols]))
        a = jnp.exp(log_a)
        m2 = jnp.maximum(-jnp.tanh(log_a) * (a * a + 1.0), 0.0)
        mult = jnp.where(m2 > 0.0, m2 * lax.rsqrt(m2), 0.0)
        bx = mult * (i * xc)

        a3 = a.reshape(tl, N_SEG, head)
        b3 = bx.reshape(tl, N_SEG, head)
        hloc = [None] * tl
        prod = [None] * tl
        hprev = pprev = None
        for t in steps:
            if hprev is None:
                hprev, pprev = b3[t], a3[t]
            else:
                hprev, pprev = a3[t] * hprev + b3[t], a3[t] * pprev
            hloc[t], prod[t] = hprev, pprev
        state = carry_scr[:, cols]
        init = jnp.zeros((N_SEG, head), F32)
        for sg in segs:
            init = jnp.where(sub == sg, state, init)
            state = hprev[sg:sg + 1, :] + pprev[sg:sg + 1, :] * state
        carry_scr[:, cols] = state
        hfull = jnp.concatenate([hloc[t] + prod[t] * init for t in range(tl)], axis=0)
        if not reverse:
            hf_scr[hf_rows, cols] = hfull
        else:
            hs = hs_scr.at[hd]
            hs[...] = hf_scr[hf_rows, cols] + hfull
            for sg in range(N_SEG):
                rows = slice(sg * tl, (sg + 1) * tl)
                hsum = hs[pl.ds(sg, tl, stride=N_SEG), :]
                mix_ref[rows, cols] = (gy_ref[rows, cols] * hsum).astype(BF16)

    return [conv_and_gate_matmul, gates_and_scan]


def _sgu_head(hd, u_ref, v_ref, sw_ref, sb_ref, mix_ref):
    tc, d_ch = u_ref.shape
    ch_head = d_ch // N_CH_HEADS
    d_rg = mix_ref.shape[1] - d_ch
    cols = slice(hd * ch_head, (hd + 1) * ch_head)
    for n in range(tc // CHUNK):
        rows = slice(n * CHUNK, (n + 1) * CHUNK)
        mixed = jnp.dot(sw_ref[hd], v_ref[rows, cols], preferred_element_type=F32) + sb_ref[hd]
        mix_ref[rows, d_rg + hd * ch_head:d_rg + (hd + 1) * ch_head] = (
            u_ref[rows, cols] * mixed).astype(BF16)


def _seq_items(reverse, edges, hf_rows, x_ref, gy_ref, u_ref, v_ref,
               cw_ref, cb_ref, gw_ref, gb_ref, lam_ref, sw_ref, sb_ref,
               mix_ref, hf_scr, carry_scr, xi_scr, hs_scr, fine):
    def together(parts):
        def run():
            for part in parts:
                part()
        return run

    items = []
    for hd in range(N_RG_HEADS):
        parts = _scan_head(hd, reverse, edges, hf_rows, x_ref, gy_ref, cw_ref, cb_ref, gw_ref, gb_ref,
                           lam_ref, mix_ref, hf_scr, carry_scr, xi_scr, hs_scr)
        if reverse:
            parts.append(functools.partial(_sgu_head, hd, u_ref, v_ref, sw_ref, sb_ref, mix_ref))
        items += parts if fine else [together(parts)]
    return items


SEQ_WEIGHTS = ("conv_w", "conv_b", "ga", "gi", "ga_b", "gi_b", "lam", "sgu_w", "sgu_b")


def _seq_weight_specs(sp):
    def spec(a):
        if isinstance(a, tuple):
            a, layer = a
            return pl.BlockSpec((None,) + a.shape[1:], lambda *_: (layer,) + (0,) * (a.ndim - 1),
                                pipeline_mode=pl.Buffered(1))
        return pl.BlockSpec(a.shape, lambda *_: (0,) * a.ndim, pipeline_mode=pl.Buffered(1))
    return [spec(sp[k]) for k in SEQ_WEIGHTS]


def _seq_weights(sp):
    return [sp[k][0] if isinstance(sp[k], tuple) else sp[k] for k in SEQ_WEIGHTS]


def _seq_scratch(seq_len, tc, d_rg):
    head = d_rg // N_RG_HEADS
    return [
        pltpu.VMEM((seq_len, d_rg), F32),
        pltpu.VMEM((1, d_rg), F32),
        pltpu.VMEM((N_RG_HEADS, tc + (CONV_W - 1) * N_SEG, head), F32),
        pltpu.VMEM((N_RG_HEADS, tc, head), F32),
    ]


N_INPROJ_IN = 5
N_INPROJ_OUT = 4
N_SEQ_IN = 14
N_SEQ_OUT = 2
N_SEQ_SCRATCH = 4


def _inproj_kernel(*refs, n_cast, with_seq):
    s = pl.program_id(0)
    refs = list(refs)
    take = lambda k: [refs.pop(0) for _ in range(k)]
    proj_in = take(N_INPROJ_IN)
    seq_in = take(N_SEQ_IN) if with_seq else None
    cast_src = take(n_cast)
    proj_out = take(N_INPROJ_OUT)
    seq_out = take(N_SEQ_OUT) if with_seq else None
    cast_dst = take(n_cast)
    casts = _cast_items(list(zip(cast_src, cast_dst)))
    if not with_seq:
        _inproj_body(*proj_in, *proj_out, between=casts)
        return

    x_ref, gy_ref, u_ref, v_ref, h0_ref = seq_in[:5]
    cw_ref, cb_ref, ga_ref, gi_ref, ga_b_ref, gi_b_ref = seq_in[5:11]
    weights = [cw_ref, cb_ref, (ga_ref, gi_ref), (ga_b_ref, gi_b_ref)] + seq_in[11:]
    mix_ref, st_ref = seq_out
    hf_scr, carry_scr, xi_scr, hs_scr = take(N_SEQ_SCRATCH)
    tc = x_ref.shape[0]

    def direction(reverse):
        dirn = 1 if reverse else 0
        work = _seq_items(reverse, None, slice(0, tc), x_ref, gy_ref, u_ref, v_ref, *weights,
                          mix_ref, hf_scr, carry_scr, xi_scr, hs_scr, fine=False)

        def first():
            carry_scr[...] = h0_ref[dirn:dirn + 1, :]
            work[0]()

        def last():
            work[-1]()
            st_ref[dirn:dirn + 1, :] = carry_scr[...]

        items = [first] + work[1:-1] + [last]
        _inproj_body(*proj_in, *proj_out, between=_merge(items, casts) if casts else items)

    @pl.when(s % 2 == 0)
    def _():
        direction(False)

    @pl.when(s % 2 == 1)
    def _():
        direction(True)


def _inproj_seq(x, mod, grp, norm_g, w_in, seq, sp, to_cast):
    m, d = x.shape
    n_steps = m // IN_ROW_TILE
    in_specs, out_specs, out_shape = _inproj_specs(m, d, w_in.shape[1], grp)
    args = [x, norm_g.reshape(1, d), mod, mod, w_in]
    scratch = []
    if seq is not None:
        xrg, gy, u, v, h0, seq_len = seq
        ms, d_rg = xrg.shape
        assert 2 * (ms // seq_len) == n_steps
        blk = lambda i: (i // 2, 0)
        per_seq = pl.BlockSpec((None, 2, d_rg), lambda i: (i // 2, 0, 0))
        in_specs = in_specs + [pl.BlockSpec((seq_len, d_rg), blk)] * 4 + [per_seq] + _seq_weight_specs(sp)
        args = args + [xrg, gy, u, v, h0] + _seq_weights(sp)
        out_specs = out_specs + [pl.BlockSpec((seq_len, 2 * d_rg), blk), per_seq]
        out_shape = out_shape + [jax.ShapeDtypeStruct((ms, 2 * d_rg), BF16),
                                 jax.ShapeDtypeStruct((ms // seq_len, 2, d_rg), F32)]
        scratch = _seq_scratch(seq_len, seq_len, d_rg)
    for w in to_cast:
        rows, cols = w.shape[0] // n_steps, w.shape[1]
        assert rows * n_steps == w.shape[0] and rows % BF16_ROWS == 0
        in_specs = in_specs + [pl.BlockSpec((rows, cols), lambda i: (i, 0))]
        args = args + [w]
    out_specs = out_specs + [pl.BlockSpec((w.shape[0] // n_steps, w.shape[1]), lambda i: (i, 0))
                             for w in to_cast]
    out_shape = out_shape + [jax.ShapeDtypeStruct(w.shape, BF16) for w in to_cast]
    return pl.pallas_call(
        functools.partial(_inproj_kernel, n_cast=len(to_cast), with_seq=seq is not None),
        grid=(n_steps,),
        in_specs=in_specs,
        out_specs=out_specs,
        out_shape=out_shape,
        scratch_shapes=scratch,
        compiler_params=_params(dimension_semantics=("arbitrary",)),
        name="in_projection" if seq is None else "in_projection_seqmix",
    )(*args)


def _ffn_first(x_ref, mix_ref, g1_ref, sc_ref, sh_ref, ng_ref, wo_ref, o_ref, h2_scr, acc_scr):
    mix = jnp.dot(mix_ref[...], wo_ref[...], preferred_element_type=F32)
    x1 = x_ref[...] + g1_ref[...] * mix
    o_ref[...] = x1
    h2 = (_rms(x1) * ng_ref[...]) * (1.0 + sc_ref[...]) + sh_ref[...]
    h2_scr[...] = h2.astype(BF16)
    acc_scr[...] = jnp.zeros_like(acc_scr)


def _ffn_step(w1_ref, w2_ref, h2_scr, acc_scr, between=()):
    n_w1 = w1_ref.shape[1] // FF_PIECE
    w2_parts = w2_ref.shape[1] // FF_PIECE if between else W2_PARTS
    n_pieces = n_w1 + w2_parts
    work = _spread(list(between), n_pieces) if between else [[]] * n_pieces
    part_d = w2_ref.shape[1] // w2_parts

    h2 = h2_scr[...]
    acts = []
    for n in range(n_w1):
        for item in work[n]:
            item()
        a = jnp.dot(h2, w1_ref[:, n * FF_PIECE:(n + 1) * FF_PIECE], preferred_element_type=F32)
        acts.append(jnp.square(jnp.maximum(a, 0.0)).astype(BF16))
    act = jnp.concatenate(acts, axis=1)
    for n in range(w2_parts):
        for item in work[n_w1 + n]:
            item()
        cols = slice(n * part_d, (n + 1) * part_d)
        acc_scr[:, cols] += jnp.dot(act, w2_ref[:, cols], preferred_element_type=F32)


def _ffn_last(g2_ref, fg_ref, o_ref, acc_scr, final_norm):
    x2 = o_ref[...] + g2_ref[...] * acc_scr[...]
    if final_norm:
        x2 = _rms(x2) * fg_ref[...]
    o_ref[...] = x2


def _ffn_kernel(x_ref, mix_ref, g1_ref, sc_ref, sh_ref, g2_ref, ng_ref, fg_ref,
                wo_ref, w1_ref, w2_ref, o_ref, h2_scr, acc_scr, *, final_norm):
    j = pl.program_id(1)

    @pl.when(j == 0)
    def _():
        _ffn_first(x_ref, mix_ref, g1_ref, sc_ref, sh_ref, ng_ref, wo_ref, o_ref, h2_scr, acc_scr)

    _ffn_step(w1_ref, w2_ref, h2_scr, acc_scr)

    @pl.when(j == pl.num_programs(1) - 1)
    def _():
        _ffn_last(g2_ref, fg_ref, o_ref, acc_scr, final_norm)


def _ffn_seq_kernel(x_ref, mix_ref, g1_ref, sc_ref, sh_ref, g2_ref, ng_ref, fg_ref,
                    wo_ref, w1_ref, w2_ref,
                    sx_ref, sprev_ref, snext_ref, gy_ref, u_ref, v_ref, h0_ref,
                    cw_ref, cb_ref, ga_ref, gi_ref, ga_b_ref, gi_b_ref, lam_ref, sw_ref, sb_ref,
                    o_ref, smix_ref,
                    h2_scr, acc_scr, hf_scr, carry_scr, xi_scr, hs_scr, *, final_norm):
    i = pl.program_id(0)
    j = pl.program_id(1)
    n_chunks = pl.num_programs(1)
    tc = sx_ref.shape[0]
    backward = i % 2 == 1
    jj = jnp.where(backward, n_chunks - 1 - j, j)
    hf_rows = pl.ds(pl.multiple_of(jj * tc, tc), tc)
    edges = (jj == 0, sprev_ref, jj == n_chunks - 1, snext_ref)

    @pl.when(j == 0)
    def _():
        _ffn_first(x_ref, mix_ref, g1_ref, sc_ref, sh_ref, ng_ref, wo_ref, o_ref, h2_scr, acc_scr)
        carry_scr[...] = h0_ref[pl.ds(i % 2, 1), :]

    def both(reverse):
        items = _seq_items(reverse, edges, hf_rows, sx_ref, gy_ref, u_ref, v_ref,
                           cw_ref, cb_ref, (ga_ref, gi_ref), (ga_b_ref, gi_b_ref), lam_ref, sw_ref, sb_ref,
                           smix_ref, hf_scr, carry_scr, xi_scr, hs_scr, fine=True)
        _ffn_step(w1_ref, w2_ref, h2_scr, acc_scr, between=items)

    @pl.when(jnp.logical_not(backward))
    def _():
        both(False)

    @pl.when(backward)
    def _():
        both(True)

    @pl.when(j == n_chunks - 1)
    def _():
        _ffn_last(g2_ref, fg_ref, o_ref, acc_scr, final_norm)


def _ffn(x, mix, mod, grp, norm_g, final_g, w_out, w_ff1, w_ff2, final_norm, ff_tile, seq, sp):
    m, d = x.shape
    d_mix = mix.shape[1]
    n_steps = w_ff1.shape[1] // ff_tile
    row = lambda i, j: (i, 0)
    vec = pl.BlockSpec((1, d), lambda i, j: (0, 0))
    in_specs = [
        pl.BlockSpec((ROW_TILE, d), row),
        pl.BlockSpec((ROW_TILE, d_mix), row),
        _mod_spec(2, grp, d),
        _mod_spec(4, grp, d),
        _mod_spec(3, grp, d),
        _mod_spec(5, grp, d),
        vec,
        vec,
        pl.BlockSpec((d_mix, d), lambda i, j: (0, 0), pipeline_mode=pl.Buffered(1)),
        pl.BlockSpec((d, ff_tile), lambda i, j: (0, j)),
        pl.BlockSpec((ff_tile, d), lambda i, j: (j, 0)),
    ]
    args = [x, mix, mod, mod, mod, mod, norm_g.reshape(1, d), final_g.reshape(1, d), w_out, w_ff1, w_ff2]
    out_specs = [pl.BlockSpec((ROW_TILE, d), row)]
    out_shape = [jax.ShapeDtypeStruct((m, d), F32)]
    scratch = [
        pltpu.VMEM((ROW_TILE, d), BF16),
        pltpu.VMEM((ROW_TILE, d), F32),
    ]
    if seq is None:
        kern, name = _ffn_kernel, "out_projection_ffn"
    else:
        kern, name = _ffn_seq_kernel, "out_projection_ffn_seqmix"
        xrg, gy, u, v, h0, seq_len = seq
        ms, d_rg = xrg.shape
        n_seq = ms // seq_len
        tc = seq_len // n_steps
        assert 2 * n_seq == m // ROW_TILE and tc % (N_SEG * SUBLANES) == 0 and tc % CHUNK == 0
        per_tile = tc // HALO
        n_halo = ms // HALO

        def chunk(i, j):
            return (i // 2) * n_steps + jnp.where(i % 2 == 1, n_steps - 1 - j, j)

        def out_chunk(i, j):
            return ((i // 2) * n_steps + jnp.where(i % 2 == 1, n_steps - 1 - j, n_steps - 1), 0)

        blk = lambda rows, cols, index: pl.BlockSpec((rows, cols), index)
        in_specs = in_specs + [
            blk(tc, d_rg, lambda i, j: (chunk(i, j), 0)),
            blk(HALO, d_rg, lambda i, j: (jnp.maximum(chunk(i, j) * per_tile - 1, 0), 0)),
            blk(HALO, d_rg, lambda i, j: (jnp.minimum((chunk(i, j) + 1) * per_tile, n_halo - 1), 0)),
            blk(tc, d_rg, out_chunk),
            blk(tc, d_rg, out_chunk),
            blk(tc, d_rg, out_chunk),
            pl.BlockSpec((None, 2, d_rg), lambda i, j: (i // 2, 0, 0)),
        ] + _seq_weight_specs(sp)
        args = args + [xrg, xrg, xrg, gy, u, v, h0] + _seq_weights(sp)
        out_specs = out_specs + [blk(tc, 2 * d_rg, out_chunk)]
        out_shape = out_shape + [jax.ShapeDtypeStruct((ms, 2 * d_rg), BF16)]
        scratch = scratch + _seq_scratch(seq_len, tc, d_rg)
    return pl.pallas_call(
        functools.partial(kern, final_norm=final_norm),
        grid=(m // ROW_TILE, n_steps),
        in_specs=in_specs,
        out_specs=out_specs,
        out_shape=out_shape,
        scratch_shapes=scratch,
        compiler_params=_params(dimension_semantics=("arbitrary", "arbitrary")),
        name=name,
    )(*args)


def kernel(x_prompt, x_sample, c, state_rglru, c_ctx, norm1_g, w_ada, b_ada, w_in, conv_w, conv_b,
           ga_w, ga_b, gi_w, gi_b, lru_lambda, sgu_w, sgu_b, w_out, norm2_g, w_ff1, w_ff2, final_g):
    b_ctx, l_ctx, d = x_prompt.shape
    b_lat, l_lat, _ = x_sample.shape
    depth = w_in.shape[0]
    d_rg = lru_lambda.shape[-1]
    assert b_lat + 1 <= MOD_ROWS
    assert (b_ctx * l_ctx) % ROW_TILE == 0 and l_lat % ROW_TILE == 0

    grp_ctx = (b_lat, b_ctx * l_ctx)
    grp_lat = (0, l_lat)
    xp = x_prompt.reshape(b_ctx * l_ctx, d)
    xs = x_sample.reshape(b_lat * l_lat, d)
    h0_ctx = jnp.zeros((b_ctx, 2, d_rg), F32)
    ctx_states = []
    for l in range(depth):
        flat = lambda w: w.reshape(-1, w.shape[-1])
        win = w_in[l].astype(BF16)
        last = l == depth - 1
        mod = _modulation(c, c_ctx.reshape(1, d), w_ada[l], b_ada[l])
        h0_lat = state_rglru[:, l].astype(F32)

        gy_c, xrg_c, u_c, v_c, wff2, wout, ga, gi, sgu = _inproj_seq(
            xp, mod, grp_ctx, norm1_g[l], win, None, None,
            [w_ff2[l], w_out[l], flat(ga_w[l]), flat(gi_w[l]), flat(sgu_w[l])])
        sp = {
            "conv_w": (conv_w, l), "conv_b": conv_b[l].reshape(1, d_rg),
            "ga": ga.reshape(ga_w[l].shape), "gi": gi.reshape(gi_w[l].shape),
            "ga_b": (ga_b, l), "gi_b": (gi_b, l),
            "lam": (lru_lambda, l),
            "sgu_w": sgu.reshape(sgu_w[l].shape), "sgu_b": sgu_b[l][:, :, None],
        }
        gy_l, xrg_l, u_l, v_l, mix_c, st, wff1 = _inproj_seq(
            xs, mod, grp_lat, norm1_g[l], win, (xrg_c, gy_c, u_c, v_c, h0_ctx, l_ctx), sp, [w_ff1[l]])
        xp, mix_l = _ffn(xp, mix_c, mod, grp_ctx, norm2_g[l], final_g, wout, wff1, wff2, last,
                         FF_TILE_FUSED, (xrg_l, gy_l, u_l, v_l, h0_lat, l_lat), sp)
        xs, = _ffn(xs, mix_l, mod, grp_lat, norm2_g[l], final_g, wout, wff1, wff2, last,
                   FF_TILE_ALONE, None, None)
        ctx_states.append(st)
    y_prompt = xp.reshape(b_ctx, l_ctx, d)
    y_sample = xs.reshape(b_lat, l_lat, d)
    new_state = jnp.stack(ctx_states, axis=1).astype(x_prompt.dtype)
    return (y_prompt, y_sample, new_state)
```

```python
import functools

import jax
import jax.numpy as jnp
from jax import lax
from jax.experimental import pallas as pl
from jax.experimental.pallas import tpu as pltpu

F32 = jnp.float32
BF16 = jnp.bfloat16

N_RG_HEADS = 8
N_CH_HEADS = 8
CHUNK = 128
CONV_W = 4
CONV_LEFT = 2
RG_C = 8.0
N_MOD = 6
EPS = 1e-6

SUBLANES = 8
BF16_ROWS = 16
MOD_ROWS = 8
VMEM_LIMIT_BYTES = 62 * 1024 * 1024

ROW_TILE = 512
IN_ROW_TILE = 256
CAST_ITEM_BYTES = 1 << 19
FF_TILE_FUSED = 512
FF_TILE_ALONE = 1024
FF_PIECE = 256
W2_PARTS = 4
MOD_COL_TILE = 1024
PROJ_PIECE = 256
N_SEG = SUBLANES
HALO = SUBLANES


def _rms(x):
    return x * lax.rsqrt(jnp.mean(x * x, axis=-1, keepdims=True) + EPS)


def _params(**kw):
    return pltpu.CompilerParams(vmem_limit_bytes=VMEM_LIMIT_BYTES, **kw)


def _spread(items, n_slots):
    slots = [[] for _ in range(n_slots)]
    for k, item in enumerate(items):
        slots[k * n_slots // len(items)].append(item)
    return slots


def _merge(a, b):
    keyed = [((k + 0.5) / len(a), 0, k, item) for k, item in enumerate(a)]
    keyed += [((k + 0.5) / len(b), 1, k, item) for k, item in enumerate(b)]
    return [entry[-1] for entry in sorted(keyed, key=lambda entry: entry[:3])]


def _mod_kernel(c_ref, cc_ref, w_ref, b_ref, o_ref, cond_scr):
    n_lat = c_ref.shape[0]
    cond_scr[...] = jnp.zeros_like(cond_scr)
    cond_scr[0:n_lat, :] = c_ref[...]
    cond_scr[n_lat:n_lat + 1, :] = cc_ref[...]
    s = jax.nn.silu(cond_scr[...]).astype(BF16)
    val = jnp.dot(s, w_ref[...].astype(BF16), preferred_element_type=F32) + b_ref[...]
    for r in range(MOD_ROWS):
        o_ref[r] = val[r:r + 1, :]


def _modulation(c, c_ctx, w_ada, b_ada):
    n_lat, d = c.shape
    n = w_ada.shape[1]
    per = d // MOD_COL_TILE
    assert d % MOD_COL_TILE == 0 and n == N_MOD * d and n_lat + 1 <= MOD_ROWS
    return pl.pallas_call(
        _mod_kernel,
        grid=(n // MOD_COL_TILE,),
        in_specs=[
            pl.BlockSpec((n_lat, d), lambda j: (0, 0)),
            pl.BlockSpec((1, d), lambda j: (0, 0)),
            pl.BlockSpec((d, MOD_COL_TILE), lambda j: (0, j)),
            pl.BlockSpec((1, MOD_COL_TILE), lambda j: (0, j)),
        ],
        out_specs=pl.BlockSpec((MOD_ROWS, None, 1, MOD_COL_TILE), lambda j: (0, j // per, 0, j % per)),
        out_shape=jax.ShapeDtypeStruct((MOD_ROWS, N_MOD, 1, d), F32),
        scratch_shapes=[pltpu.VMEM((MOD_ROWS, d), F32)],
        compiler_params=_params(dimension_semantics=("arbitrary",)),
        name="modulation",
    )(c, c_ctx, w_ada, b_ada.reshape(1, n))


def _mod_spec(which, grp, d, row_tile=ROW_TILE):
    mod_base, rows_per_mod = grp

    def index(i, *_):
        return (mod_base + (i * row_tile) // rows_per_mod, which, 0, 0)
    return pl.BlockSpec((None, None, 1, d), index)


def _cast_items(pairs):
    items = []
    for src, dst in pairs:
        rows = max(CAST_ITEM_BYTES // (4 * src.shape[1]), BF16_ROWS)
        for r in range(0, src.shape[0], rows):
            def run(src=src, dst=dst, part=slice(r, min(r + rows, src.shape[0]))):
                dst[part, :] = src[part, :].astype(BF16)
            items.append(run)
    return items


def _inproj_body(x_ref, g_ref, sc_ref, sh_ref, w_ref, gy_ref, xrg_ref, u_ref, v_ref, between=()):
    d_rg = gy_ref.shape[1]
    h = (_rms(x_ref[...]) * g_ref[...]) * (1.0 + sc_ref[...]) + sh_ref[...]
    hb = h.astype(BF16)
    outs = (gy_ref, xrg_ref, u_ref, v_ref)
    n_pieces = w_ref.shape[1] // PROJ_PIECE
    work = _spread(list(between), n_pieces) if between else [[]] * n_pieces
    for n in range(n_pieces):
        part = jnp.dot(hb, w_ref[:, n * PROJ_PIECE:(n + 1) * PROJ_PIECE], preferred_element_type=F32)
        dest, col = divmod(n * PROJ_PIECE, d_rg)
        if dest != 1:
            part = jax.nn.gelu(part)
        outs[dest][:, col:col + PROJ_PIECE] = part.astype(outs[dest].dtype)
        for item in work[n]:
            item()


def _inproj_specs(m, d, d_in, grp):
    d_rg = d_in // 4
    row = lambda i: (i, 0)
    in_specs = [
        pl.BlockSpec((IN_ROW_TILE, d), row),
        pl.BlockSpec((1, d), lambda i: (0, 0)),
        _mod_spec(1, grp, d, IN_ROW_TILE),
        _mod_spec(0, grp, d, IN_ROW_TILE),
        pl.BlockSpec((d, d_in), lambda i: (0, 0), pipeline_mode=pl.Buffered(1)),
    ]
    out_specs = [pl.BlockSpec((IN_ROW_TILE, d_rg), row)] * 4
    out_shape = [
        jax.ShapeDtypeStruct((m, d_rg), F32),
        jax.ShapeDtypeStruct((m, d_rg), F32),
        jax.ShapeDtypeStruct((m, d_rg), F32),
        jax.ShapeDtypeStruct((m, d_rg), BF16),
    ]
    return in_specs, out_specs, out_shape


def _scan_head(hd, reverse, edges, hf_rows, x_ref, gy_ref, cw_ref, cb_ref, gw_ref, gb_ref, lam_ref,
               mix_ref, hf_scr, carry_scr, xi_scr, hs_scr):
    tc, d_rg = x_ref.shape
    tl = tc // N_SEG
    head = d_rg // N_RG_HEADS
    lead = CONV_LEFT * N_SEG
    dirn = 1 if reverse else 0
    steps = range(tl - 1, -1, -1) if reverse else range(tl)
    segs = range(N_SEG - 1, -1, -1) if reverse else range(N_SEG)
    cols = slice(hd * head, (hd + 1) * head)
    xi = xi_scr.at[hd]
    held = {}

    def step_rows(t):
        return xi[lead + t * N_SEG:lead + (t + 1) * N_SEG, :]

    def conv_and_gate_matmul():
        sub = lax.broadcasted_iota(jnp.int32, (N_SEG, head), 0)
        for sg in range(N_SEG):
            xi[pl.ds(lead + sg, tl, stride=N_SEG), :] = x_ref[sg * tl:(sg + 1) * tl, cols]
        for k in range(CONV_LEFT):
            if edges is None:
                edge = 0.0
            else:
                row = HALO - CONV_LEFT + k
                edge = jnp.where(edges[0], 0.0, edges[1][row:row + 1, cols])
            prev_step = pltpu.roll(step_rows(tl - CONV_LEFT + k), 1, axis=0)
            xi[k * N_SEG:(k + 1) * N_SEG, :] = jnp.where(sub == 0, edge, prev_step)
        for k in range(CONV_W - 1 - CONV_LEFT):
            edge = 0.0 if edges is None else jnp.where(edges[2], 0.0, edges[3][k:k + 1, cols])
            next_step = pltpu.roll(step_rows(k), N_SEG - 1, axis=0)
            xi[lead + (tl + k) * N_SEG:lead + (tl + k + 1) * N_SEG, :] = jnp.where(
                sub == N_SEG - 1, edge, next_step)
        xc = cb_ref[:, cols]
        for k in range(CONV_W):
            xc = xc + cw_ref[k:k + 1, cols] * xi[k * N_SEG:k * N_SEG + tc, :]
        held["xc"] = xc
        gw = jnp.concatenate([ref[dirn, hd] for ref in gw_ref], axis=-1)
        held["g"] = jnp.dot(xc.astype(BF16), gw, preferred_element_type=F32)

    def gates_and_scan():
        sub = lax.broadcasted_iota(jnp.int32, (N_SEG, head), 0)
        xc, g = held["xc"], held["g"]
        ga_b_ref, gi_b_ref = gb_ref
        r = jax.nn.sigmoid(g[:, :head] + ga_b_ref[dirn:dirn + 1, cols])
        i = jax.nn.sigmoid(g[:, head:] + gi_b_ref[dirn:dirn + 1, cols])
        log_a = r * (-RG_C * jax.nn.softplus(-lam_ref[dirn:dirn + 1, cols]))
        a = jnp.exp(log_a)
        m2 = jnp.maximum(-jnp.tanh(log_a) * (a * a + 1.0), 0.0)
        mult = jnp.where(m2 > 0.0, m2 * lax.rsqrt(m2), 0.0)
        bx = mult * (i * xc)

        a3 = a.reshape(tl, N_SEG, head)
        b3 = bx.reshape(tl, N_SEG, head)
        hloc = [None] * tl
        prod = [None] * tl
        hprev = pprev = None
        for t in steps:
            if hprev is None:
                hprev, pprev = b3[t], a3[t]
            else:
                hprev, pprev = a3[t] * hprev + b3[t], a3[t] * pprev
            hloc[t], prod[t] = hprev, pprev
        state = carry_scr[:, cols]
        init = jnp.zeros((N_SEG, head), F32)
        for sg in segs:
            init = jnp.where(sub == sg, state, init)
            state = hprev[sg:sg + 1, :] + pprev[sg:sg + 1, :] * state
        carry_scr[:, cols] = state
        hfull = jnp.concatenate([hloc[t] + prod[t] * init for t in range(tl)], axis=0)
        if not reverse:
            hf_scr[hf_rows, cols] = hfull
        else:
            hs = hs_scr.at[hd]
            hs[...] = hf_scr[hf_rows, cols] + hfull
            for sg in range(N_SEG):
                rows = slice(sg * tl, (sg + 1) * tl)
                hsum = hs[pl.ds(sg, tl, stride=N_SEG), :]
                mix_ref[rows, cols] = (gy_ref[rows, cols] * hsum).astype(BF16)

    return [conv_and_gate_matmul, gates_and_scan]


def _sgu_head(hd, u_ref, v_ref, sw_ref, sb_ref, mix_ref):
    tc, d_ch = u_ref.shape
    ch_head = d_ch // N_CH_HEADS
    d_rg = mix_ref.shape[1] - d_ch
    cols = slice(hd * ch_head, (hd + 1) * ch_head)
    for n in range(tc // CHUNK):
        rows = slice(n * CHUNK, (n + 1) * CHUNK)
        mixed = jnp.dot(sw_ref[hd], v_ref[rows, cols], preferred_element_type=F32) + sb_ref[hd]
        mix_ref[rows, d_rg + hd * ch_head:d_rg + (hd + 1) * ch_head] = (
            u_ref[rows, cols] * mixed).astype(BF16)


def _seq_items(reverse, edges, hf_rows, x_ref, gy_ref, u_ref, v_ref,
               cw_ref, cb_ref, gw_ref, gb_ref, lam_ref, sw_ref, sb_ref,
               mix_ref, hf_scr, carry_scr, xi_scr, hs_scr, fine):
    def together(parts):
        def run():
            for part in parts:
                part()
        return run

    items = []
    for hd in range(N_RG_HEADS):
        parts = _scan_head(hd, reverse, edges, hf_rows, x_ref, gy_ref, cw_ref, cb_ref, gw_ref, gb_ref,
                           lam_ref, mix_ref, hf_scr, carry_scr, xi_scr, hs_scr)
        if reverse:
            parts.append(functools.partial(_sgu_head, hd, u_ref, v_ref, sw_ref, sb_ref, mix_ref))
        items += parts if fine else [together(parts)]
    return items


SEQ_WEIGHTS = ("conv_w", "conv_b", "ga", "gi", "ga_b", "gi_b", "lam", "sgu_w", "sgu_b")


def _seq_weight_specs(sp):
    def spec(a):
        if isinstance(a, tuple):
            a, layer = a
            return pl.BlockSpec((None,) + a.shape[1:], lambda *_: (layer,) + (0,) * (a.ndim - 1),
                                pipeline_mode=pl.Buffered(1))
        return pl.BlockSpec(a.shape, lambda *_: (0,) * a.ndim, pipeline_mode=pl.Buffered(1))
    return [spec(sp[k]) for k in SEQ_WEIGHTS]


def _seq_weights(sp):
    return [sp[k][0] if isinstance(sp[k], tuple) else sp[k] for k in SEQ_WEIGHTS]


def _seq_scratch(seq_len, tc, d_rg):
    head = d_rg // N_RG_HEADS
    return [
        pltpu.VMEM((seq_len, d_rg), F32),
        pltpu.VMEM((1, d_rg), F32),
        pltpu.VMEM((N_RG_HEADS, tc + (CONV_W - 1) * N_SEG, head), F32),
        pltpu.VMEM((N_RG_HEADS, tc, head), F32),
    ]


N_INPROJ_IN = 5
N_INPROJ_OUT = 4
N_SEQ_IN = 14
N_SEQ_OUT = 2
N_SEQ_SCRATCH = 4


def _inproj_kernel(*refs, n_cast, with_seq, zero_state=False):
    s = pl.program_id(0)
    refs = list(refs)
    take = lambda k: [refs.pop(0) for _ in range(k)]
    proj_in = take(N_INPROJ_IN)
    seq_in = take(N_SEQ_IN - zero_state) if with_seq else None
    if with_seq and zero_state:
        seq_in.insert(4, None)
    cast_src = take(n_cast)
    proj_out = take(N_INPROJ_OUT)
    seq_out = take(N_SEQ_OUT) if with_seq else None
    cast_dst = take(n_cast)
    casts = _cast_items(list(zip(cast_src, cast_dst)))
    if not with_seq:
        _inproj_body(*proj_in, *proj_out, between=casts)
        return

    x_ref, gy_ref, u_ref, v_ref, h0_ref = seq_in[:5]
    cw_ref, cb_ref, ga_ref, gi_ref, ga_b_ref, gi_b_ref = seq_in[5:11]
    weights = [cw_ref, cb_ref, (ga_ref, gi_ref), (ga_b_ref, gi_b_ref)] + seq_in[11:]
    mix_ref, st_ref = seq_out
    hf_scr, carry_scr, xi_scr, hs_scr = take(N_SEQ_SCRATCH)
    tc = x_ref.shape[0]

    def direction(reverse):
        dirn = 1 if reverse else 0
        work = _seq_items(reverse, None, slice(0, tc), x_ref, gy_ref, u_ref, v_ref, *weights,
                          mix_ref, hf_scr, carry_scr, xi_scr, hs_scr, fine=False)

        def first():
            carry_scr[...] = jnp.zeros_like(carry_scr) if h0_ref is None else h0_ref[dirn:dirn + 1, :]
            work[0]()

        def last():
            work[-1]()
            st_ref[dirn:dirn + 1, :] = carry_scr[...]

        items = [first] + work[1:-1] + [last]
        _inproj_body(*proj_in, *proj_out, between=_merge(items, casts) if casts else items)

    @pl.when(s % 2 == 0)
    def _():
        direction(False)

    @pl.when(s % 2 == 1)
    def _():
        direction(True)


def _inproj_seq(x, mod, grp, norm_g, w_in, seq, sp, to_cast):
    m, d = x.shape
    n_steps = m // IN_ROW_TILE
    in_specs, out_specs, out_shape = _inproj_specs(m, d, w_in.shape[1], grp)
    args = [x, norm_g.reshape(1, d), mod, mod, w_in]
    scratch = []
    if seq is not None:
        xrg, gy, u, v, h0, seq_len = seq
        ms, d_rg = xrg.shape
        assert 2 * (ms // seq_len) == n_steps
        blk = lambda i: (i // 2, 0)
        per_seq = pl.BlockSpec((None, 2, d_rg), lambda i: (i // 2, 0, 0))
        state = [] if h0 is None else [(per_seq, h0)]
        in_specs = (in_specs + [pl.BlockSpec((seq_len, d_rg), blk)] * 4 + [s for s, _ in state]
                    + _seq_weight_specs(sp))
        args = args + [xrg, gy, u, v] + [a for _, a in state] + _seq_weights(sp)
        out_specs = out_specs + [pl.BlockSpec((seq_len, 2 * d_rg), blk), per_seq]
        out_shape = out_shape + [jax.ShapeDtypeStruct((ms, 2 * d_rg), BF16),
                                 jax.ShapeDtypeStruct((ms // seq_len, 2, d_rg), F32)]
        scratch = _seq_scratch(seq_len, seq_len, d_rg)
    for w in to_cast:
        rows, cols = w.shape[0] // n_steps, w.shape[1]
        assert rows * n_steps == w.shape[0] and rows % BF16_ROWS == 0
        in_specs = in_specs + [pl.BlockSpec((rows, cols), lambda i: (i, 0))]
        args = args + [w]
    out_specs = out_specs + [pl.BlockSpec((w.shape[0] // n_steps, w.shape[1]), lambda i: (i, 0))
                             for w in to_cast]
    out_shape = out_shape + [jax.ShapeDtypeStruct(w.shape, BF16) for w in to_cast]
    return pl.pallas_call(
        functools.partial(_inproj_kernel, n_cast=len(to_cast), with_seq=seq is not None,
                          zero_state=seq is not None and seq[4] is None),
        grid=(n_steps,),
        in_specs=in_specs,
        out_specs=out_specs,
        out_shape=out_shape,
        scratch_shapes=scratch,
        compiler_params=_params(dimension_semantics=("arbitrary",)),
        name="in_projection" if seq is None else "in_projection_seqmix",
    )(*args)


def _ffn_first(x_ref, mix_ref, g1_ref, sc_ref, sh_ref, ng_ref, wo_ref, o_ref, h2_scr, acc_scr):
    mix = jnp.dot(mix_ref[...], wo_ref[...], preferred_element_type=F32)
    x1 = x_ref[...] + g1_ref[...] * mix
    o_ref[...] = x1
    h2 = (_rms(x1) * ng_ref[...]) * (1.0 + sc_ref[...]) + sh_ref[...]
    h2_scr[...] = h2.astype(BF16)
    acc_scr[...] = jnp.zeros_like(acc_scr)


def _ffn_step(w1_ref, w2_ref, h2_scr, acc_scr, between=()):
    n_w1 = w1_ref.shape[1] // FF_PIECE
    w2_parts = w2_ref.shape[1] // FF_PIECE if between else W2_PARTS
    n_pieces = n_w1 + w2_parts
    work = _spread(list(between), n_pieces) if between else [[]] * n_pieces
    part_d = w2_ref.shape[1] // w2_parts

    h2 = h2_scr[...]
    acts = []
    for n in range(n_w1):
        for item in work[n]:
            item()
        a = jnp.dot(h2, w1_ref[:, n * FF_PIECE:(n + 1) * FF_PIECE], preferred_element_type=F32)
        acts.append(jnp.square(jnp.maximum(a, 0.0)).astype(BF16))
    act = jnp.concatenate(acts, axis=1)
    for n in range(w2_parts):
        for item in work[n_w1 + n]:
            item()
        cols = slice(n * part_d, (n + 1) * part_d)
        acc_scr[:, cols] += jnp.dot(act, w2_ref[:, cols], preferred_element_type=F32)


def _ffn_last(g2_ref, fg_ref, o_ref, acc_scr, final_norm):
    x2 = o_ref[...] + g2_ref[...] * acc_scr[...]
    if final_norm:
        x2 = _rms(x2) * fg_ref[...]
    o_ref[...] = x2


def _ffn_kernel(x_ref, mix_ref, g1_ref, sc_ref, sh_ref, g2_ref, ng_ref, fg_ref,
                wo_ref, w1_ref, w2_ref, o_ref, h2_scr, acc_scr, *, final_norm):
    j = pl.program_id(1)

    @pl.when(j == 0)
    def _():
        _ffn_first(x_ref, mix_ref, g1_ref, sc_ref, sh_ref, ng_ref, wo_ref, o_ref, h2_scr, acc_scr)

    _ffn_step(w1_ref, w2_ref, h2_scr, acc_scr)

    @pl.when(j == pl.num_programs(1) - 1)
    def _():
        _ffn_last(g2_ref, fg_ref, o_ref, acc_scr, final_norm)


def _ffn_seq_kernel(x_ref, mix_ref, g1_ref, sc_ref, sh_ref, g2_ref, ng_ref, fg_ref,
                    wo_ref, w1_ref, w2_ref,
                    sx_ref, sprev_ref, snext_ref, gy_ref, u_ref, v_ref, h0_ref,
                    cw_ref, cb_ref, ga_ref, gi_ref, ga_b_ref, gi_b_ref, lam_ref, sw_ref, sb_ref,
                    o_ref, smix_ref,
                    h2_scr, acc_scr, hf_scr, carry_scr, xi_scr, hs_scr, *, final_norm):
    i = pl.program_id(0)
    j = pl.program_id(1)
    n_chunks = pl.num_programs(1)
    tc = sx_ref.shape[0]
    backward = i % 2 == 1
    jj = jnp.where(backward, n_chunks - 1 - j, j)
    hf_rows = pl.ds(pl.multiple_of(jj * tc, tc), tc)
    edges = (jj == 0, sprev_ref, jj == n_chunks - 1, snext_ref)

    @pl.when(j == 0)
    def _():
        _ffn_first(x_ref, mix_ref, g1_ref, sc_ref, sh_ref, ng_ref, wo_ref, o_ref, h2_scr, acc_scr)
        carry_scr[...] = h0_ref[pl.ds(i % 2, 1), :]

    def both(reverse):
        items = _seq_items(reverse, edges, hf_rows, sx_ref, gy_ref, u_ref, v_ref,
                           cw_ref, cb_ref, (ga_ref, gi_ref), (ga_b_ref, gi_b_ref), lam_ref, sw_ref, sb_ref,
                           smix_ref, hf_scr, carry_scr, xi_scr, hs_scr, fine=True)
        _ffn_step(w1_ref, w2_ref, h2_scr, acc_scr, between=items)

    @pl.when(jnp.logical_not(backward))
    def _():
        both(False)

    @pl.when(backward)
    def _():
        both(True)

    @pl.when(j == n_chunks - 1)
    def _():
        _ffn_last(g2_ref, fg_ref, o_ref, acc_scr, final_norm)


def _ffn(x, mix, mod, grp, norm_g, final_g, w_out, w_ff1, w_ff2, final_norm, ff_tile, seq, sp):
    m, d = x.shape
    d_mix = mix.shape[1]
    n_steps = w_ff1.shape[1] // ff_tile
    row = lambda i, j: (i, 0)
    vec = pl.BlockSpec((1, d), lambda i, j: (0, 0))
    in_specs = [
        pl.BlockSpec((ROW_TILE, d), row),
        pl.BlockSpec((ROW_TILE, d_mix), row),
        _mod_spec(2, grp, d),
        _mod_spec(4, grp, d),
        _mod_spec(3, grp, d),
        _mod_spec(5, grp, d),
        vec,
        vec,
        pl.BlockSpec((d_mix, d), lambda i, j: (0, 0), pipeline_mode=pl.Buffered(1)),
        pl.BlockSpec((d, ff_tile), lambda i, j: (0, j)),
        pl.BlockSpec((ff_tile, d), lambda i, j: (j, 0)),
    ]
    args = [x, mix, mod, mod, mod, mod, norm_g.reshape(1, d), final_g.reshape(1, d), w_out, w_ff1, w_ff2]
    out_specs = [pl.BlockSpec((ROW_TILE, d), row)]
    out_shape = [jax.ShapeDtypeStruct((m, d), F32)]
    scratch = [
        pltpu.VMEM((ROW_TILE, d), BF16),
        pltpu.VMEM((ROW_TILE, d), F32),
    ]
    if seq is None:
        kern, name = _ffn_kernel, "out_projection_ffn"
    else:
        kern, name = _ffn_seq_kernel, "out_projection_ffn_seqmix"
        xrg, gy, u, v, h0, seq_len = seq
        ms, d_rg = xrg.shape
        n_seq = ms // seq_len
        tc = seq_len // n_steps
        assert 2 * n_seq == m // ROW_TILE and tc % (N_SEG * SUBLANES) == 0 and tc % CHUNK == 0
        per_tile = tc // HALO
        n_halo = ms // HALO

        def chunk(i, j):
            return (i // 2) * n_steps + jnp.where(i % 2 == 1, n_steps - 1 - j, j)

        def out_chunk(i, j):
            return ((i // 2) * n_steps + jnp.where(i % 2 == 1, n_steps - 1 - j, n_steps - 1), 0)

        blk = lambda rows, cols, index: pl.BlockSpec((rows, cols), index)
        in_specs = in_specs + [
            blk(tc, d_rg, lambda i, j: (chunk(i, j), 0)),
            blk(HALO, d_rg, lambda i, j: (jnp.maximum(chunk(i, j) * per_tile - 1, 0), 0)),
            blk(HALO, d_rg, lambda i, j: (jnp.minimum((chunk(i, j) + 1) * per_tile, n_halo - 1), 0)),
            blk(tc, d_rg, out_chunk),
            blk(tc, d_rg, out_chunk),
            blk(tc, d_rg, out_chunk),
            pl.BlockSpec((None, 2, d_rg), lambda i, j: (i // 2, 0, 0)),
        ] + _seq_weight_specs(sp)
        args = args + [xrg, xrg, xrg, gy, u, v, h0] + _seq_weights(sp)
        out_specs = out_specs + [blk(tc, 2 * d_rg, out_chunk)]
        out_shape = out_shape + [jax.ShapeDtypeStruct((ms, 2 * d_rg), BF16)]
        scratch = scratch + _seq_scratch(seq_len, tc, d_rg)
    return pl.pallas_call(
        functools.partial(kern, final_norm=final_norm),
        grid=(m // ROW_TILE, n_steps),
        in_specs=in_specs,
        out_specs=out_specs,
        out_shape=out_shape,
        scratch_shapes=scratch,
        compiler_params=_params(dimension_semantics=("arbitrary", "arbitrary")),
        name=name,
    )(*args)


def kernel(x_prompt, x_sample, c, state_rglru, c_ctx, norm1_g, w_ada, b_ada, w_in, conv_w, conv_b,
           ga_w, ga_b, gi_w, gi_b, lru_lambda, sgu_w, sgu_b, w_out, norm2_g, w_ff1, w_ff2, final_g):
    b_ctx, l_ctx, d = x_prompt.shape
    b_lat, l_lat, _ = x_sample.shape
    depth = w_in.shape[0]
    d_rg = lru_lambda.shape[-1]
    assert b_lat + 1 <= MOD_ROWS
    assert (b_ctx * l_ctx) % ROW_TILE == 0 and l_lat % ROW_TILE == 0

    grp_ctx = (b_lat, b_ctx * l_ctx)
    grp_lat = (0, l_lat)
    xp = x_prompt.reshape(b_ctx * l_ctx, d)
    xs = x_sample.reshape(b_lat * l_lat, d)
    ctx_states = []
    for l in range(depth):
        flat = lambda w: w.reshape(-1, w.shape[-1])
        win = w_in[l].astype(BF16)
        last = l == depth - 1
        mod = _modulation(c, c_ctx.reshape(1, d), w_ada[l], b_ada[l])
        h0_lat = state_rglru[:, l].astype(F32)

        gy_c, xrg_c, u_c, v_c, wff2, wout, ga, gi, sgu = _inproj_seq(
            xp, mod, grp_ctx, norm1_g[l], win, None, None,
            [w_ff2[l], w_out[l], flat(ga_w[l]), flat(gi_w[l]), flat(sgu_w[l])])
        sp = {
            "conv_w": (conv_w, l), "conv_b": conv_b[l].reshape(1, d_rg),
            "ga": ga.reshape(ga_w[l].shape), "gi": gi.reshape(gi_w[l].shape),
            "ga_b": (ga_b, l), "gi_b": (gi_b, l),
            "lam": (lru_lambda, l),
            "sgu_w": sgu.reshape(sgu_w[l].shape), "sgu_b": sgu_b[l][:, :, None],
        }
        gy_l, xrg_l, u_l, v_l, mix_c, st, wff1 = _inproj_seq(
            xs, mod, grp_lat, norm1_g[l], win, (xrg_c, gy_c, u_c, v_c, None, l_ctx), sp, [w_ff1[l]])
        xp, mix_l = _ffn(xp, mix_c, mod, grp_ctx, norm2_g[l], final_g, wout, wff1, wff2, last,
                         FF_TILE_FUSED, (xrg_l, gy_l, u_l, v_l, h0_lat, l_lat), sp)
        xs, = _ffn(xs, mix_l, mod, grp_lat, norm2_g[l], final_g, wout, wff1, wff2, last,
                   FF_TILE_ALONE, None, None)
        ctx_states.append(st)
    y_prompt = xp.reshape(b_ctx, l_ctx, d)
    y_sample = xs.reshape(b_lat, l_lat, d)
    new_state = jnp.stack(ctx_states, axis=1).astype(x_prompt.dtype)
    return (y_prompt, y_sample, new_state)
```
